```python
import math
import jax, jax.numpy as jnp
from jax import lax
import numpy as np

D_MODEL = 1024
BATCH = 4
SEQ = 8192
DEPTH = 4

CTX_LEN = 256
GRID_W = 64
N_ADA = 9
D_FF = 2816
EPS = 1e-6
ROPE_THETA = 10000.0
Q_BLOCK = 128

MLA_HEADS = 8
MLA_NOPE = 64
MLA_ROPE = 32
MLA_V = 64
MLA_Q_LORA = 384
MLA_KV_LORA = 256
MLA_SCALE = (MLA_NOPE + MLA_ROPE) ** -0.5

DIFF_HEADS = 8
DIFF_HD = 32
DIFF_V = 2 * DIFF_HD
DIFF_SCALE = DIFF_HD ** -0.5

POOL_WINDOWS = (2, 4, 8, 16)
POOL_GROUPS = 4
POOL_W = 512
POOL_G = POOL_W // POOL_GROUPS

N_BRANCH = 3
MLA_OUT_W = MLA_HEADS * MLA_V
DIFF_OUT_W = DIFF_HEADS * DIFF_V

IN_WIDTHS = (MLA_Q_LORA, MLA_KV_LORA, MLA_ROPE,
             DIFF_HEADS * 2 * DIFF_HD, DIFF_HEADS * 2 * DIFF_HD, DIFF_HEADS * DIFF_V,
             POOL_W, N_BRANCH * D_MODEL)
IN_COLS = sum(IN_WIDTHS)

kernel_name = "hybrid_mla_diffattn_pool_macaron_dit"

KEY_NAMES = ("mla_k", "mla_v", "dk1", "dk2", "dv")


def rmsnorm(x, g):
    x32 = x.astype(jnp.float32)
    y = x32 * lax.rsqrt(jnp.mean(x32 * x32, axis=-1, keepdims=True) + EPS)
    return (y * g.astype(jnp.float32)).astype(x.dtype)


def axial_rope_tables(n, dim):
    rows = n // GRID_W
    row_ids = jnp.repeat(jnp.arange(rows), GRID_W).astype(jnp.float32)
    col_ids = jnp.tile(jnp.arange(GRID_W), rows).astype(jnp.float32)
    n_freq = dim // 4
    inv_freq = ROPE_THETA ** (-jnp.arange(n_freq, dtype=jnp.float32) / n_freq)
    ang = jnp.concatenate([row_ids[:, None] * inv_freq, col_ids[:, None] * inv_freq], axis=-1)
    return jnp.cos(ang), jnp.sin(ang)


def apply_rope(x, tables):
    cos, sin = tables
    half = x.shape[-1] // 2
    x32 = x.astype(jnp.float32)
    x1, x2 = x32[..., :half], x32[..., half:]
    c, s = cos[:, None, :], sin[:, None, :]
    return jnp.concatenate([x1 * c - x2 * s, x1 * s + x2 * c], axis=-1).astype(x.dtype)


def swiglu(u, w_gate, w_up, w_down):
    return (jax.nn.silu(u @ w_gate) * (u @ w_up)) @ w_down


def over_query_blocks(fn, *qs):
    b, n = qs[0].shape[:2]
    nb = n // Q_BLOCK
    blocks = tuple(jnp.moveaxis(q.reshape(b, nb, Q_BLOCK, *q.shape[2:]), 1, 0) for q in qs)
    out = lax.map(lambda qb: fn(*qb), blocks)
    return jnp.moveaxis(out, 0, 1).reshape(b, n, *out.shape[3:])


def softmax_attention(q, k, v, scale):
    def block(qb):
        s = jnp.einsum('bqhd,bkhd->bhqk', qb, k, preferred_element_type=jnp.float32) * scale
        p = jax.nn.softmax(s, axis=-1).astype(v.dtype)
        return jnp.einsum('bhqk,bkhd->bqhd', p, v)
    return over_query_blocks(block, q)


def differential_attention(q1, q2, k1, k2, v, lam, scale):
    def block(q1b, q2b):
        s1 = jnp.einsum('bqhd,bkhd->bhqk', q1b, k1, preferred_element_type=jnp.float32) * scale
        s2 = jnp.einsum('bqhd,bkhd->bhqk', q2b, k2, preferred_element_type=jnp.float32) * scale
        p = (jax.nn.softmax(s1, axis=-1) - lam * jax.nn.softmax(s2, axis=-1)).astype(v.dtype)
        return jnp.einsum('bhqk,bkhd->bqhd', p, v)
    return over_query_blocks(block, q1, q2)


def multiscale_pool(p):
    n = p.shape[1]
    t = jnp.arange(n)
    p32 = p.astype(jnp.float32)
    outs = []
    for g, w in enumerate(POOL_WINDOWS):
        xg = p32[..., g * POOL_G:(g + 1) * POOL_G]
        cs = jnp.concatenate([jnp.zeros_like(xg[:, :1]), lax.cumsum(xg, axis=1)], axis=1)
        lo = jnp.clip(t - w // 2, 0, n)
        hi = jnp.clip(t + w - w // 2, 0, n)
        mean = (cs[:, hi] - cs[:, lo]) / (hi - lo).astype(jnp.float32)[None, :, None]
        outs.append(mean - xg)
    return jnp.stack(outs, axis=2).astype(p.dtype)


def mixer_streams(u, w_in_l, q_norm_g, kv_norm_g, w_uq, w_ukv, rope_mla, rope_diff):
    b, n, _ = u.shape
    offs = np.cumsum(IN_WIDTHS)[:-1].tolist()
    cq, ckv, kr, dq, dk, dv, pool_in, gates = jnp.split(u @ w_in_l, offs, axis=-1)
    q = (rmsnorm(cq, q_norm_g) @ w_uq).reshape(b, n, MLA_HEADS, MLA_NOPE + MLA_ROPE)
    kv = (rmsnorm(ckv, kv_norm_g) @ w_ukv).reshape(b, n, MLA_HEADS, MLA_NOPE + MLA_V)
    q_nope, q_rope = q[..., :MLA_NOPE], q[..., MLA_NOPE:]
    k_nope, v_mla = kv[..., :MLA_NOPE], kv[..., MLA_NOPE:]
    k_rope = kr[:, :, None, :]
    dq = dq.reshape(b, n, DIFF_HEADS, 2, DIFF_HD)
    dk = dk.reshape(b, n, DIFF_HEADS, 2, DIFF_HD)
    dq1, dq2, dk1, dk2 = dq[..., 0, :], dq[..., 1, :], dk[..., 0, :], dk[..., 1, :]
    if rope_mla is not None:
        q_rope, k_rope = apply_rope(q_rope, rope_mla), apply_rope(k_rope, rope_mla)
        dq1, dq2 = apply_rope(dq1, rope_diff), apply_rope(dq2, rope_diff)
        dk1, dk2 = apply_rope(dk1, rope_diff), apply_rope(dk2, rope_diff)
    mla_q = jnp.concatenate([q_nope, q_rope], axis=-1)
    mla_k = jnp.concatenate([k_nope, jnp.broadcast_to(k_rope, (b, n, MLA_HEADS, MLA_ROPE))], axis=-1)
    return {"mla_q": mla_q, "mla_k": mla_k, "mla_v": v_mla,
            "dq1": dq1, "dq2": dq2, "dk1": dk1, "dk2": dk2,
            "dv": dv.reshape(b, n, DIFF_HEADS, DIFF_V),
            "pool_in": pool_in, "gates": gates}


def mixer_output(s, keys, lam, lam_init, subln_g, pool_proj, pool_b, pool_scale,
                 b_gate, w_br_mla, w_br_diff, w_br_pool, w_out):
    b, n, _ = s["pool_in"].shape
    o_mla = softmax_attention(s["mla_q"], keys["mla_k"], keys["mla_v"], MLA_SCALE).reshape(b, n, MLA_OUT_W)
    o_diff = differential_attention(s["dq1"], s["dq2"], keys["dk1"], keys["dk2"], keys["dv"], lam, DIFF_SCALE)
    o_diff = (rmsnorm(o_diff, subln_g) * (1.0 - lam_init)).reshape(b, n, DIFF_OUT_W)
    pooled = multiscale_pool(s["pool_in"])
    o_pool = (jnp.einsum('bngc,gcd->bngd', pooled, pool_proj) + pool_b).reshape(b, n, POOL_W) * pool_scale
    g = jax.nn.sigmoid((s["gates"].reshape(b, n, N_BRANCH, D_MODEL) + b_gate).astype(jnp.float32))
    g = g.astype(o_pool.dtype)
    merged = (g[:, :, 0] * (o_mla @ w_br_mla) + g[:, :, 1] * (o_diff @ w_br_diff)
              + g[:, :, 2] * (o_pool @ w_br_pool))
    return merged @ w_out


def setup_inputs(seed: int = 0) -> dict:
    key = jax.random.key(seed)
    ks = jax.random.split(key, 32)
    f32 = jnp.float32
    L, D, F = DEPTH, D_MODEL, D_FF

    def nrm(k, shape, scale):
        return jax.random.normal(k, shape, f32) * scale

    def gain(k, shape):
        return 1.0 + 0.02 * jax.random.normal(k, shape, f32)

    return {
        "x": nrm(ks[0], (BATCH, SEQ, D), 1.0),
        "c": nrm(ks[1], (BATCH, D), 1.0),
        "ctx": nrm(ks[2], (BATCH, CTX_LEN, D), 1.0),
        "c_ctx": nrm(ks[3], (D,), 1.0),
        "ada_w": nrm(ks[4], (L, D, N_ADA * D), 0.5 * D ** -0.5),
        "ada_b": nrm(ks[5], (L, N_ADA * D), 0.01),
        "norm_g": gain(ks[6], (L, 3, D)),
        "ffa_w_gate": nrm(ks[7], (L, D, F), D ** -0.5),
        "ffa_w_up": nrm(ks[8], (L, D, F), D ** -0.5),
        "ffa_w_down": nrm(ks[9], (L, F, D), F ** -0.5),
        "ffb_w_gate": nrm(ks[10], (L, D, F), D ** -0.5),
        "ffb_w_up": nrm(ks[11], (L, D, F), D ** -0.5),
        "ffb_w_down": nrm(ks[12], (L, F, D), F ** -0.5),
        "w_in": nrm(ks[13], (L, D, IN_COLS), D ** -0.5),
        "b_gate": nrm(ks[14], (L, N_BRANCH, D), 0.01),
        "mla_q_norm_g": gain(ks[15], (L, MLA_Q_LORA)),
        "mla_kv_norm_g": gain(ks[16], (L, MLA_KV_LORA)),
        "mla_w_uq": nrm(ks[17], (L, MLA_Q_LORA, MLA_HEADS * (MLA_NOPE + MLA_ROPE)), MLA_Q_LORA ** -0.5),
        "mla_w_ukv": nrm(ks[18], (L, MLA_KV_LORA, MLA_HEADS * (MLA_NOPE + MLA_V)), MLA_KV_LORA ** -0.5),
        "diff_lambda": nrm(ks[19], (L, 4, DIFF_HD), 0.1),
        "diff_subln_g": gain(ks[20], (L, DIFF_V)),
        "pool_proj": nrm(ks[21], (L, POOL_GROUPS, POOL_G, POOL_G), POOL_G ** -0.5),
        "pool_b": nrm(ks[22], (L, POOL_GROUPS, POOL_G), 0.01),
        "pool_scale": gain(ks[23], (L, POOL_W)),
        "w_br_mla": nrm(ks[24], (L, MLA_OUT_W, D), MLA_OUT_W ** -0.5),
        "w_br_diff": nrm(ks[25], (L, DIFF_OUT_W, D), DIFF_OUT_W ** -0.5),
        "w_br_pool": nrm(ks[26], (L, POOL_W, D), POOL_W ** -0.5),
        "w_out": nrm(ks[27], (L, D, D), D ** -0.5),
        "final_g": gain(ks[28], (D,)),
    }


def reference(x, c, ctx, c_ctx, ada_w, ada_b, norm_g,
              ffa_w_gate, ffa_w_up, ffa_w_down, ffb_w_gate, ffb_w_up, ffb_w_down,
              w_in, b_gate, mla_q_norm_g, mla_kv_norm_g, mla_w_uq, mla_w_ukv,
              diff_lambda, diff_subln_g, pool_proj, pool_b, pool_scale,
              w_br_mla, w_br_diff, w_br_pool, w_out, final_g):
    b, n, d = x.shape
    rope_mla = axial_rope_tables(n, MLA_ROPE)
    rope_diff = axial_rope_tables(n, DIFF_HD)
    h = ctx

    def modulated(z, m, g, i):
        return rmsnorm(z, g) * (1.0 + m[:, :, 3 * i + 1]) + m[:, :, 3 * i]

    for l in range(DEPTH):
        last = l == DEPTH - 1
        mod_x = (jax.nn.silu(c) @ ada_w[l] + ada_b[l]).reshape(b, 1, N_ADA, d)
        mod_h = (jax.nn.silu(c_ctx)[None] @ ada_w[l] + ada_b[l]).reshape(1, 1, N_ADA, d)
        ffa = (ffa_w_gate[l], ffa_w_up[l], ffa_w_down[l])
        ffb = (ffb_w_gate[l], ffb_w_up[l], ffb_w_down[l])

        x = x + 0.5 * mod_x[:, :, 2] * swiglu(modulated(x, mod_x, norm_g[l, 0], 0), *ffa)
        h = h + 0.5 * mod_h[:, :, 2] * swiglu(modulated(h, mod_h, norm_g[l, 0], 0), *ffa)

        proj = (w_in[l], mla_q_norm_g[l], mla_kv_norm_g[l], mla_w_uq[l], mla_w_ukv[l])
        sx = mixer_streams(modulated(x, mod_x, norm_g[l, 1], 1), *proj, rope_mla, rope_diff)
        sh = mixer_streams(modulated(h, mod_h, norm_g[l, 1], 1), *proj, None, None)
        lam_init = 0.8 - 0.6 * math.exp(-0.3 * l)
        lq1, lk1, lq2, lk2 = (diff_lambda[l, i].astype(jnp.float32) for i in range(4))
        lam = jnp.exp(jnp.sum(lq1 * lk1)) - jnp.exp(jnp.sum(lq2 * lk2)) + lam_init
        out_params = (lam, lam_init, diff_subln_g[l], pool_proj[l], pool_b[l], pool_scale[l],
                      b_gate[l], w_br_mla[l], w_br_diff[l], w_br_pool[l], w_out[l])
        keys_x = {k: jnp.concatenate([sh[k], sx[k]], axis=1) for k in KEY_NAMES}
        x = x + mod_x[:, :, 5] * mixer_output(sx, keys_x, *out_params)
        if not last:
            h = h + mod_h[:, :, 5] * mixer_output(sh, sh, *out_params)

        x = x + 0.5 * mod_x[:, :, 8] * swiglu(modulated(x, mod_x, norm_g[l, 2], 2), *ffb)
        if not last:
            h = h + 0.5 * mod_h[:, :, 8] * swiglu(modulated(h, mod_h, norm_g[l, 2], 2), *ffb)

    return rmsnorm(x, final_g)
```

```python
import functools
import math

import numpy as np
import jax
import jax.numpy as jnp
from jax import lax
from jax.experimental import pallas as pl
from jax.experimental.pallas import tpu as pltpu

F32 = jnp.float32
BF16 = jnp.bfloat16

EPS = 1e-6
ROPE_THETA = 10000.0
GRID_W = 64
N_ADA = 9

HEADS = 8
MLA_NOPE = 64
MLA_ROPE = 32
MLA_V = 64
MLA_QK = MLA_NOPE + MLA_ROPE
MLA_SCALE = MLA_QK ** -0.5
DIFF_HD = 32
DIFF_V = 64
DIFF_SCALE = DIFF_HD ** -0.5
ROPE_HALF = 16
POOL_WINDOWS = (2, 4, 8, 16)
POOL_G = 128
POOL_HALO = 16

LANES = 128
TM = 256
TK = 768
TQ_MLA = 512
TQ_DIFF = 256
VMEM_LIMIT = 52 * 1024 * 1024

NT_DIMS = (((1,), (1,)), ((), ()))
TN_DIMS = (((0,), (0,)), ((), ()))


def _params():
    return pltpu.CompilerParams(vmem_limit_bytes=VMEM_LIMIT)


def _const_spec(shape):
    zeros = (0,) * len(shape)
    return pl.BlockSpec(shape, lambda *_: zeros, pipeline_mode=pl.Buffered(1))


def _dot(a, b):
    return jnp.dot(a, b, preferred_element_type=F32)


def _rms(x):
    return x * lax.rsqrt(jnp.mean(x * x, axis=-1, keepdims=True) + EPS)


def _mod_kernel(c_ref, w_ref, b_ref, o_ref):
    c = c_ref[...]
    a = c * jax.nn.sigmoid(c)
    o_ref[0] = jnp.dot(a, w_ref[0], preferred_element_type=F32,
                       precision=lax.Precision.HIGHEST) + b_ref[0]


def _modulation(c_rows, ada_w, ada_b):
    depth, d, n = ada_w.shape
    bn = n // 8
    return pl.pallas_call(
        _mod_kernel,
        grid=(depth, n // bn),
        in_specs=[pl.BlockSpec((8, d), lambda l, j: (0, 0)),
                  pl.BlockSpec((1, d, bn), lambda l, j: (l, 0, j)),
                  pl.BlockSpec((1, 1, bn), lambda l, j: (l, 0, j))],
        out_specs=pl.BlockSpec((1, 8, bn), lambda l, j: (l, 0, j)),
        out_shape=jax.ShapeDtypeStruct((depth, 8, n), F32),
        compiler_params=_params(),
        name="modulation",
    )(c_rows, ada_w, ada_b.reshape(depth, 1, n))


def _ffn_kernel(x_ref, mod_ref, g_ref, wg_ref, wu_ref, wd_ref, o_ref, *, mi, f_chunks):
    x = x_ref[0]
    m = mod_ref[0]
    shift, scale, gate = m[3 * mi:3 * mi + 1], m[3 * mi + 1:3 * mi + 2], m[3 * mi + 2:3 * mi + 3]
    u = (_rms(x) * g_ref[...]) * (1.0 + scale) + shift
    ub = u.astype(BF16)
    y = jnp.zeros(x.shape, F32)
    for lo, hi in f_chunks:
        a = _dot(ub, wg_ref[:, lo:hi])
        b = _dot(ub, wu_ref[:, lo:hi])
        hid = (a * jax.nn.sigmoid(a) * b).astype(BF16)
        y = y + _dot(hid, wd_ref[lo:hi, :])
    o_ref[0] = x + (0.5 * gate) * y


def _ffn(xs, mod_l, g, wg, wu, wd, *, mi, mod_map):
    b, nt, d = xs.shape
    f = wg.shape[1]
    chunk = 1024
    f_chunks = tuple((lo, min(lo + chunk, f)) for lo in range(0, f, chunk))
    tile = pl.BlockSpec((1, TM, d), lambda i, t: (i, t, 0))
    return pl.pallas_call(
        functools.partial(_ffn_kernel, mi=mi, f_chunks=f_chunks),
        grid=(b, nt // TM),
        in_specs=[tile, pl.BlockSpec((1, N_ADA, d), mod_map), _const_spec((1, d)),
                  _const_spec((d, f)), _const_spec((d, f)), _const_spec((f, d))],
        out_specs=tile,
        out_shape=jax.ShapeDtypeStruct(xs.shape, F32),
        input_output_aliases={0: 0},
        compiler_params=_params(),
        name=f"ffn{mi}",
    )(xs, mod_l, g, wg, wu, wd)


def _inproj_kernel(x_ref, mod_ref, g_ref, wcq, wckv, wkr2, wdqT, wdk, wdvT, wpool, wgates,
                   qng, kvng, wuqT, wkpad, wvT, place, perm,
                   cosT, sinT, cosN, sinN, c128, s128,
                   oq_mla, ok_mla, ov_mla, oq_diff, ok_diff, ov_diff, opool, ogates):
    x = x_ref[0]
    m = mod_ref[0]
    u = (_rms(x) * g_ref[...]) * (1.0 + m[4:5]) + m[3:4]
    ub = u.astype(BF16)
    tm = x.shape[0]
    cT, sT = cosT[...], sinT[...]
    hw = ROPE_HALF

    cqn = (_rms(_dot(ub, wcq[...])) * qng[...]).astype(BF16)
    qT = lax.dot_general(wuqT[...], cqn, NT_DIMS, preferred_element_type=F32) * MLA_SCALE
    nope_w = HEADS * MLA_NOPE
    x1, x2 = qT[nope_w:nope_w + LANES], qT[nope_w + LANES:nope_w + 2 * LANES]
    qn = qT[0:nope_w].astype(BF16)
    r1 = (x1 * cT - x2 * sT).astype(BF16)
    r2 = (x1 * sT + x2 * cT).astype(BF16)
    zpad = jnp.zeros((LANES - MLA_QK, tm), BF16)
    for h in range(HEADS):
        oq_mla[0, h, 0:MLA_NOPE, :] = qn[h * MLA_NOPE:(h + 1) * MLA_NOPE]
        oq_mla[0, h, MLA_NOPE:MLA_NOPE + hw, :] = r1[h * hw:(h + 1) * hw]
        oq_mla[0, h, MLA_NOPE + hw:MLA_QK, :] = r2[h * hw:(h + 1) * hw]
        oq_mla[0, h, MLA_QK:LANES, :] = zpad

    ckvn = (_rms(_dot(ub, wckv[...])) * kvng[...]).astype(BF16)
    kr2 = _dot(ub, wkr2[...])
    krr = (kr2[:, 0:LANES] * c128[...] + kr2[:, LANES:2 * LANES] * s128[...]).astype(BF16)
    kall = (_dot(ckvn, wkpad[...]) + _dot(krr, place[...])).astype(BF16)
    vT = lax.dot_general(wvT[...], ckvn, NT_DIMS, preferred_element_type=F32).astype(BF16)
    for h in range(HEADS):
        ok_mla[0, h] = kall[:, h * LANES:(h + 1) * LANES]
        ov_mla[0, h, 0] = vT[h * MLA_V:(h + 1) * MLA_V]

    dqT = lax.dot_general(wdqT[...], ub, NT_DIMS, preferred_element_type=F32) * DIFF_SCALE
    a1, a2, b1, b2 = (dqT[i * LANES:(i + 1) * LANES] for i in range(4))
    parts = [(a1 * cT - a2 * sT).astype(BF16), (a1 * sT + a2 * cT).astype(BF16),
             (b1 * cT - b2 * sT).astype(BF16), (b1 * sT + b2 * cT).astype(BF16)]
    zhalf = jnp.zeros((2 * DIFF_HD, tm), BF16)
    for h in range(HEADS):
        base = (h % 2) * 2 * DIFF_HD
        for i, part in enumerate(parts):
            oq_diff[0, h, base + i * hw:base + (i + 1) * hw, :] = part[h * hw:(h + 1) * hw]
        other = 2 * DIFF_HD - base
        oq_diff[0, h, other:other + 2 * DIFF_HD, :] = zhalf

    dk = _dot(ub, wdk[...])
    cN, sN = cosN[...], sinN[...]
    k1a, k1b, k2a, k2b = (dk[:, i * LANES:(i + 1) * LANES] for i in range(4))
    rk = jnp.concatenate([k1a * cN - k1b * sN, k1a * sN + k1b * cN,
                          k2a * cN - k2b * sN, k2a * sN + k2b * cN], axis=1).astype(BF16)
    kd = _dot(rk, perm[...]).astype(BF16)
    for p in range(HEADS // 2):
        ok_diff[0, p] = kd[:, p * LANES:(p + 1) * LANES]
    dvT = lax.dot_general(wdvT[...], ub, NT_DIMS, preferred_element_type=F32).astype(BF16)
    for h in range(HEADS):
        ov_diff[0, h, 0] = dvT[h * DIFF_V:(h + 1) * DIFF_V]

    opool[0] = _dot(ub, wpool[...])
    ogates[0] = _dot(ub, wgates[...]).astype(BF16)


def _inproj(xs, mod_l, g, w, tables, *, mod_map):
    b, nt, d = xs.shape
    nkc = nt // TK
    sub = TK // TM
    tile = pl.BlockSpec((1, TM, d), lambda i, t: (i, t, 0))
    tabT = pl.BlockSpec((LANES, TM), lambda i, t: (0, t))
    tabN = pl.BlockSpec((TM, LANES), lambda i, t: (t, 0))
    weights = [w["wcq"], w["wckv"], w["wkr2"], w["wdqT"], w["wdk"], w["wdvT"], w["wpool"], w["wgates"],
               w["qng"], w["kvng"], w["wuqT"], w["wkpad"], w["wvT"], w["place"], w["perm"]]
    vT_spec = pl.BlockSpec((1, HEADS, 1, MLA_V, TM), lambda i, t: (i, 0, t // sub, 0, t % sub))
    out_shapes = [
        jax.ShapeDtypeStruct((b, HEADS, LANES, nt), BF16),
        jax.ShapeDtypeStruct((b, HEADS, nt, LANES), BF16),
        jax.ShapeDtypeStruct((b, HEADS, nkc, MLA_V, TK), BF16),
        jax.ShapeDtypeStruct((b, HEADS, LANES, nt), BF16),
        jax.ShapeDtypeStruct((b, HEADS // 2, nt, LANES), BF16),
        jax.ShapeDtypeStruct((b, HEADS, nkc, DIFF_V, TK), BF16),
        jax.ShapeDtypeStruct((b, nt, 4 * POOL_G), F32),
        jax.ShapeDtypeStruct((b, nt, 3 * d), BF16),
    ]
    out_specs = [
        pl.BlockSpec((1, HEADS, LANES, TM), lambda i, t: (i, 0, 0, t)),
        pl.BlockSpec((1, HEADS, TM, LANES), lambda i, t: (i, 0, t, 0)),
        vT_spec,
        pl.BlockSpec((1, HEADS, LANES, TM), lambda i, t: (i, 0, 0, t)),
        pl.BlockSpec((1, HEADS // 2, TM, LANES), lambda i, t: (i, 0, t, 0)),
        vT_spec,
        pl.BlockSpec((1, TM, 4 * POOL_G), lambda i, t: (i, t, 0)),
        pl.BlockSpec((1, TM, 3 * d), lambda i, t: (i, t, 0)),
    ]
    return pl.pallas_call(
        _inproj_kernel,
        grid=(b, nt // TM),
        in_specs=([tile, pl.BlockSpec((1, N_ADA, d), mod_map), _const_spec((1, d))]
                  + [_const_spec(a.shape) for a in weights]
                  + [tabT, tabT, tabN, tabN, tabN, tabN]),
        out_specs=out_specs,
        out_shape=out_shapes,
        compiler_params=_params(),
        name="inproj",
    )(xs, mod_l, g, *weights, tables["cosT"], tables["sinT"], tables["cosN"], tables["sinN"],
      tables["c128"], tables["s128"])


def _attn_kernel(*refs, nkc, tk, diff, lam_init):
    refs = list(refs)
    q_ref, k_ref, v_ref = refs[:3]
    o_ref = refs[-1]
    q = q_ref[0, 0]
    tq = q.shape[1]
    if diff:
        dl_ref, sg_ref = refs[3:5]
        first = (lax.broadcasted_iota(jnp.int32, q.shape, 0) & DIFF_HD) == 0
        zero = jnp.zeros_like(q)
        rhs = jnp.concatenate([jnp.where(first, q, zero), jnp.where(first, zero, q)], axis=1)
    else:
        rhs = q
    n = rhs.shape[1]
    m = jnp.full((1, n), -1e30, F32)
    l = jnp.zeros((1, n), F32)
    acc = jnp.zeros((v_ref.shape[3], n), F32)
    for kc in range(nkc):
        s = _dot(k_ref[0, 0, kc * tk:(kc + 1) * tk, :], rhs)
        m_new = jnp.maximum(m, jnp.max(s, axis=0, keepdims=True))
        alpha = jnp.exp(m - m_new)
        p = jnp.exp(s - m_new)
        l = alpha * l + jnp.sum(p, axis=0, keepdims=True)
        acc = alpha * acc + _dot(v_ref[0, 0, kc], p.astype(BF16))
        m = m_new
    o = acc / l
    if diff:
        dl = dl_ref[...]
        lam = (jnp.exp(jnp.sum(dl[0:1] * dl[1:2], axis=1, keepdims=True))
               - jnp.exp(jnp.sum(dl[2:3] * dl[3:4], axis=1, keepdims=True)) + lam_init)
        od = o[:, :tq] - lam * o[:, tq:]
        od = od * lax.rsqrt(jnp.mean(od * od, axis=0, keepdims=True) + EPS)
        o_ref[0] = (od * sg_ref[...] * (1.0 - lam_init)).astype(o_ref.dtype)
    else:
        o_ref[0] = o.astype(o_ref.dtype)


def _attention(qT, k, vT, *, tq, q_blocks, q_block0, k_rows, k_block0, v_chunk, v_sub, nkc, tk,
               diff, extra=(), lam_init=0.0, prev_out=None):
    b, heads, _, nt = qT.shape
    dv = vT.shape[3]
    kdiv = 2 if diff else 1
    in_specs = [
        pl.BlockSpec((1, 1, LANES, tq), lambda i, h, j: (i, h, 0, q_block0 + j)),
        pl.BlockSpec((1, 1, k_rows, LANES), lambda i, h, j: (i, h // kdiv, k_block0, 0)),
        pl.BlockSpec((1, 1, nkc, dv, tk), lambda i, h, j: (i, h, v_chunk, 0, v_sub)),
    ]
    args = [qT, k, vT]
    if diff:
        in_specs += [pl.BlockSpec(extra[0].shape, lambda i, h, j: (0, 0)),
                     pl.BlockSpec(extra[1].shape, lambda i, h, j: (0, 0))]
        args += list(extra)
    aliases = {}
    if prev_out is not None:
        in_specs.append(pl.BlockSpec(memory_space=pl.ANY))
        args.append(prev_out)
        aliases = {len(args) - 1: 0}
    kern = functools.partial(_attn_kernel, nkc=nkc, tk=tk, diff=diff, lam_init=lam_init)
    if prev_out is not None:
        kern = _drop_alias_ref(kern)
    return pl.pallas_call(
        kern,
        grid=(b, heads, q_blocks),
        in_specs=in_specs,
        out_specs=pl.BlockSpec((1, dv, tq), lambda i, h, j: (i, h, q_block0 + j)),
        out_shape=jax.ShapeDtypeStruct((b, heads * dv, nt), BF16),
        input_output_aliases=aliases,
        compiler_params=_params(),
        name=("diff" if diff else "mla") + ("_ctx" if prev_out is not None else "_x"),
    )(*args)


def _drop_alias_ref(kern):
    def wrapped(*refs):
        return kern(*refs[:-2], refs[-1])
    return wrapped


def _mix_kernel(x_ref, mod_ref, oa_ref, od_ref, pc_ref, pp_ref, pn_ref, gt_ref, bg_ref,
                pproj_ref, pb_ref, ps_ref, wa_ref, wd_ref, wp_ref, wo_ref, o_ref, *, x_tiles):
    t = pl.program_id(1)
    x = x_ref[0]
    gate = mod_ref[0][5:6]
    tm, d = x.shape
    a = lax.dot_general(oa_ref[0], wa_ref[...], TN_DIMS, preferred_element_type=F32)
    dd = lax.dot_general(od_ref[0], wd_ref[...], TN_DIMS, preferred_element_type=F32)

    is_ctx = t == x_tiles
    has_prev = jnp.logical_and(t != 0, jnp.logical_not(is_ctx))
    has_next = jnp.logical_and(t != x_tiles - 1, jnp.logical_not(is_ctx))
    cur = pc_ref[0]
    prev = jnp.where(has_prev, pp_ref[0], 0.0)
    nxt = jnp.where(has_next, pn_ref[0], 0.0)
    ext = jnp.concatenate([prev, cur, nxt], axis=0)
    rows = ext.shape[0]
    seq_len = jnp.where(is_ctx, tm, x_tiles * tm)
    pos = jnp.where(is_ctx, 0, t * tm) + lax.broadcasted_iota(jnp.int32, (tm, 1), 0)
    outs = []
    for g, w in enumerate(POOL_WINDOWS):
        lanes = slice(g * POOL_G, (g + 1) * POOL_G)
        run = ext[:, lanes]
        span = 1
        while span < w:
            run = run + pltpu.roll(run, rows - span, axis=0)
            span *= 2
        win = pltpu.roll(run, rows - (POOL_HALO - w // 2), axis=0)[0:tm]
        cnt = (jnp.minimum(pos + w // 2, seq_len) - jnp.maximum(pos - w // 2, 0)).astype(F32)
        pooled = (win / cnt - cur[:, lanes]).astype(BF16)
        outs.append((_dot(pooled, pproj_ref[g]) + pb_ref[:, lanes]) * ps_ref[:, lanes])
    pooled_out = _dot(jnp.concatenate(outs, axis=1).astype(BF16), wp_ref[...])

    gs = jax.nn.sigmoid(gt_ref[0].astype(F32) + bg_ref[...])
    merged = gs[:, 0:d] * a + gs[:, d:2 * d] * dd + gs[:, 2 * d:3 * d] * pooled_out
    o_ref[0] = x + gate * _dot(merged.astype(BF16), wo_ref[...])


def _mix(xs, mod_l, oa, od, pool_in, gates, w, *, mod_map, x_tiles):
    b, nt, d = xs.shape
    hb = TM // POOL_HALO
    n_halo = nt // POOL_HALO
    tile = pl.BlockSpec((1, TM, d), lambda i, t: (i, t, 0))
    pw = 4 * POOL_G
    weights = [w["bgate"], w["pproj"], w["pb"], w["ps"], w["wa"], w["wd"], w["wp"], w["wo"]]
    return pl.pallas_call(
        functools.partial(_mix_kernel, x_tiles=x_tiles),
        grid=(b, nt // TM),
        in_specs=[tile, pl.BlockSpec((1, N_ADA, d), mod_map),
                  pl.BlockSpec((1, oa.shape[1], TM), lambda i, t: (i, 0, t)),
                  pl.BlockSpec((1, od.shape[1], TM), lambda i, t: (i, 0, t)),
                  pl.BlockSpec((1, TM, pw), lambda i, t: (i, t, 0)),
                  pl.BlockSpec((1, POOL_HALO, pw), lambda i, t: (i, jnp.maximum(t * hb - 1, 0), 0)),
                  pl.BlockSpec((1, POOL_HALO, pw), lambda i, t: (i, jnp.minimum((t + 1) * hb, n_halo - 1), 0)),
                  pl.BlockSpec((1, TM, 3 * d), lambda i, t: (i, t, 0))]
                 + [_const_spec(a.shape) for a in weights],
        out_specs=tile,
        out_shape=jax.ShapeDtypeStruct(xs.shape, F32),
        input_output_aliases={0: 0},
        compiler_params=_params(),
        name="mix",
    )(xs, mod_l, oa, od, pool_in, pool_in, pool_in, gates, *weights)


def _final_kernel(x_ref, g_ref, o_ref):
    o_ref[0] = _rms(x_ref[0]) * g_ref[...]


def _final_norm(xs, g, seq):
    b, _, d = xs.shape
    tile = pl.BlockSpec((1, TM, d), lambda i, t: (i, t, 0))
    return pl.pallas_call(
        _final_kernel,
        grid=(b, seq // TM),
        in_specs=[tile, _const_spec((1, d))],
        out_specs=tile,
        out_shape=jax.ShapeDtypeStruct((b, seq, d), F32),
        compiler_params=_params(),
        name="final_norm",
    )(xs, g)


def _rope_tables(seq, ctx):
    rows = seq // GRID_W
    row_ids = jnp.repeat(jnp.arange(rows), GRID_W).astype(F32)
    col_ids = jnp.tile(jnp.arange(GRID_W), rows).astype(F32)
    n_freq = ROPE_HALF // 2
    inv_freq = ROPE_THETA ** (-jnp.arange(n_freq, dtype=F32) / n_freq)
    ang = jnp.concatenate([row_ids[:, None] * inv_freq, col_ids[:, None] * inv_freq], axis=-1)
    cos = jnp.concatenate([jnp.cos(ang), jnp.ones((ctx, ROPE_HALF), F32)], axis=0)
    sin = jnp.concatenate([jnp.sin(ang), jnp.zeros((ctx, ROPE_HALF), F32)], axis=0)
    pad = jnp.zeros((seq + ctx, LANES - 2 * ROPE_HALF), F32)
    return {
        "cosT": jnp.tile(cos.T, (HEADS, 1)), "sinT": jnp.tile(sin.T, (HEADS, 1)),
        "cosN": jnp.tile(cos, (1, HEADS)), "sinN": jnp.tile(sin, (1, HEADS)),
        "c128": jnp.concatenate([cos, cos, pad], axis=1),
        "s128": jnp.concatenate([sin, sin, pad], axis=1),
    }


def _index_tables():
    hw = ROPE_HALF
    uq = np.zeros(HEADS * MLA_QK, np.int32)
    for h in range(HEADS):
        for j in range(MLA_NOPE):
            uq[h * MLA_NOPE + j] = h * MLA_QK + j
        for f in range(hw):
            uq[HEADS * MLA_NOPE + h * hw + f] = h * MLA_QK + MLA_NOPE + f
            uq[HEADS * MLA_NOPE + LANES + h * hw + f] = h * MLA_QK + MLA_NOPE + hw + f
    dqk = np.zeros(HEADS * 2 * DIFF_HD, np.int32)
    perm = np.zeros((HEADS * 2 * DIFF_HD, HEADS * 2 * DIFF_HD), np.float32)
    for h in range(HEADS):
        for c in range(2):
            for half in range(2):
                for f in range(hw):
                    src = h * 2 * DIFF_HD + c * DIFF_HD + half * hw + f
                    blk = (c * 2 + half) * LANES + h * hw + f
                    dqk[blk] = src
                    perm[blk, src] = 1.0
    v_cols = np.array([h * (MLA_NOPE + MLA_V) + MLA_NOPE + j for h in range(HEADS) for j in range(MLA_V)], np.int32)
    k_mask = ((np.arange(HEADS * (MLA_NOPE + MLA_V)) % (MLA_NOPE + MLA_V)) < MLA_NOPE).astype(np.float32)
    place = np.zeros((LANES, HEADS * LANES), np.float32)
    for h in range(HEADS):
        for f in range(MLA_ROPE):
            place[f, h * LANES + MLA_NOPE + f] = 1.0
    return uq, dqk, perm, v_cols, k_mask, place


def _layer_weights(l, p, idx):
    uq, dqk, perm, v_cols, k_mask, place = idx
    d = p["w_in"].shape[1]
    w_in = p["w_in"][l]
    o_cq, o_ckv, o_kr = 0, 384, 640
    o_dq, o_dk, o_dv, o_pool, o_gate = 672, 1184, 1696, 2208, 2720
    wkr = w_in[:, o_kr:o_kr + MLA_ROPE]
    wkr_rot = jnp.concatenate([-wkr[:, ROPE_HALF:], wkr[:, :ROPE_HALF]], axis=1)
    zpad = jnp.zeros((d, LANES - MLA_ROPE), F32)
    wkr2 = jnp.concatenate([wkr, zpad, wkr_rot, zpad], axis=1)
    w_ukv = p["mla_w_ukv"][l]
    bf = lambda a: a.astype(BF16)
    return {
        "wcq": bf(w_in[:, o_cq:o_ckv]), "wckv": bf(w_in[:, o_ckv:o_kr]), "wkr2": bf(wkr2),
        "wdqT": bf(w_in[:, o_dq + dqk].T), "wdk": bf(w_in[:, o_dk + dqk]),
        "wdvT": bf(w_in[:, o_dv:o_pool].T), "wpool": bf(w_in[:, o_pool:o_gate]), "wgates": bf(w_in[:, o_gate:]),
        "qng": p["mla_q_norm_g"][l][None], "kvng": p["mla_kv_norm_g"][l][None],
        "wuqT": bf(p["mla_w_uq"][l][:, uq].T), "wkpad": bf(w_ukv * k_mask[None]), "wvT": bf(w_ukv[:, v_cols].T),
        "place": jnp.asarray(place, BF16), "perm": jnp.asarray(perm, BF16),
        "bgate": p["b_gate"][l].reshape(1, -1), "pproj": bf(p["pool_proj"][l]),
        "pb": p["pool_b"][l].reshape(1, -1), "ps": p["pool_scale"][l][None],
        "wa": bf(p["w_br_mla"][l]), "wd": bf(p["w_br_diff"][l]), "wp": bf(p["w_br_pool"][l]), "wo": bf(p["w_out"][l]),
    }


def kernel(x, c, ctx, c_ctx, ada_w, ada_b, norm_g, ffa_w_gate, ffa_w_up, ffa_w_down, ffb_w_gate, ffb_w_up, ffb_w_down, w_in, b_gate, mla_q_norm_g, mla_kv_norm_g, mla_w_uq, mla_w_ukv, diff_lambda, diff_subln_g, pool_proj, pool_b, pool_scale, w_br_mla, w_br_diff, w_br_pool, w_out, final_g):
    b, seq, d = x.shape
    n_ctx = ctx.shape[1]
    depth = ada_w.shape[0]
    nt = seq + n_ctx
    assert n_ctx == TM and seq % TQ_MLA == 0 and nt % TK == 0 and seq % GRID_W == 0 and b + 1 <= 8
    x_tiles = seq // TM
    p = dict(w_in=w_in, b_gate=b_gate, mla_q_norm_g=mla_q_norm_g, mla_kv_norm_g=mla_kv_norm_g,
             mla_w_uq=mla_w_uq, mla_w_ukv=mla_w_ukv, pool_proj=pool_proj, pool_b=pool_b,
             pool_scale=pool_scale, w_br_mla=w_br_mla, w_br_diff=w_br_diff, w_br_pool=w_br_pool, w_out=w_out)

    c_rows = jnp.concatenate([c, c_ctx[None], jnp.zeros((8 - b - 1, d), F32)], axis=0)
    mod = _modulation(c_rows, ada_w, ada_b).reshape(depth, 8, N_ADA, d)
    mod_map = lambda i, t: (jnp.where(t == x_tiles, b, i), 0, 0)

    tables = _rope_tables(seq, n_ctx)
    idx = _index_tables()
    xs = jnp.concatenate([x, ctx], axis=1)
    nkc = nt // TK
    ctx_chunk, ctx_sub = seq // TK, (seq % TK) // TM
    bf = lambda a: a.astype(BF16)

    for l in range(depth):
        w = _layer_weights(l, p, idx)
        lam_init = 0.8 - 0.6 * math.exp(-0.3 * l)
        xs = _ffn(xs, mod[l], norm_g[l, 0][None], bf(ffa_w_gate[l]), bf(ffa_w_up[l]), bf(ffa_w_down[l]),
                  mi=0, mod_map=mod_map)
        q_mla, k_mla, v_mla, q_diff, k_diff, v_diff, pool_in, gates = _inproj(
            xs, mod[l], norm_g[l, 1][None], w, tables, mod_map=mod_map)
        extra = (diff_lambda[l], diff_subln_g[l][:, None])
        full = dict(k_rows=nt, k_block0=0, v_chunk=0, v_sub=0, nkc=nkc, tk=TK)
        ctx_only = dict(tq=TM, q_blocks=1, q_block0=x_tiles, k_rows=TM, k_block0=x_tiles,
                        v_chunk=ctx_chunk, v_sub=ctx_sub, nkc=1, tk=TM)
        oa = _attention(q_mla, k_mla, v_mla, tq=TQ_MLA, q_blocks=seq // TQ_MLA, q_block0=0, diff=False, **full)
        oa = _attention(q_mla, k_mla, v_mla, diff=False, prev_out=oa, **ctx_only)
        od = _attention(q_diff, k_diff, v_diff, tq=TQ_DIFF, q_blocks=seq // TQ_DIFF, q_block0=0, diff=True,
                        extra=extra, lam_init=lam_init, **full)
        od = _attention(q_diff, k_diff, v_diff, diff=True, extra=extra, lam_init=lam_init, prev_out=od, **ctx_only)
        xs = _mix(xs, mod[l], oa, od, pool_in, gates, w, mod_map=mod_map, x_tiles=x_tiles)
        xs = _ffn(xs, mod[l], norm_g[l, 2][None], bf(ffb_w_gate[l]), bf(ffb_w_up[l]), bf(ffb_w_down[l]),
                  mi=2, mod_map=mod_map)

    return _final_norm(xs, final_g[None], seq)
```

```python
import functools
import math

import numpy as np
import jax
import jax.numpy as jnp
from jax import lax
from jax.experimental import pallas as pl
from jax.experimental.pallas import tpu as pltpu

F32 = jnp.float32
BF16 = jnp.bfloat16

EPS = 1e-6
ROPE_THETA = 10000.0
GRID_W = 64
N_ADA = 9

HEADS = 8
MLA_NOPE = 64
MLA_ROPE = 32
MLA_V = 64
MLA_QK = MLA_NOPE + MLA_ROPE
LOG2E = math.log2(math.e)
MLA_SCALE = MLA_QK ** -0.5 * LOG2E
DIFF_HD = 32
DIFF_V = 64
DIFF_SCALE = DIFF_HD ** -0.5 * LOG2E
ROPE_HALF = 16
POOL_WINDOWS = (2, 4, 8, 16)
POOL_G = 128
POOL_HALO = 16

LANES = 128
TM = 256
TK = 768
ATT_AHEAD = 2
ATT_GROUPS = 1
TQ_MLA = 512
TQ_DIFF = 256
VMEM_LIMIT = 52 * 1024 * 1024

NT_DIMS = (((1,), (1,)), ((), ()))
TN_DIMS = (((0,), (0,)), ((), ()))


def _params():
    return pltpu.CompilerParams(vmem_limit_bytes=VMEM_LIMIT)


def _const_spec(shape):
    zeros = (0,) * len(shape)
    return pl.BlockSpec(shape, lambda *_: zeros, pipeline_mode=pl.Buffered(1))


def _dot(a, b):
    return jnp.dot(a, b, preferred_element_type=F32)


def _rms(x):
    return x * lax.rsqrt(jnp.mean(x * x, axis=-1, keepdims=True) + EPS)


def _mod_kernel(c_ref, w_ref, b_ref, o_ref):
    c = c_ref[...]
    a = c * jax.nn.sigmoid(c)
    o_ref[0] = jnp.dot(a, w_ref[0], preferred_element_type=F32,
                       precision=lax.Precision.HIGHEST) + b_ref[0]


def _modulation(c_rows, ada_w, ada_b):
    depth, d, n = ada_w.shape
    bn = n // 8
    return pl.pallas_call(
        _mod_kernel,
        grid=(depth, n // bn),
        in_specs=[pl.BlockSpec((8, d), lambda l, j: (0, 0)),
                  pl.BlockSpec((1, d, bn), lambda l, j: (l, 0, j)),
                  pl.BlockSpec((1, 1, bn), lambda l, j: (l, 0, j))],
        out_specs=pl.BlockSpec((1, 8, bn), lambda l, j: (l, 0, j)),
        out_shape=jax.ShapeDtypeStruct((depth, 8, n), F32),
        compiler_params=_params(),
        name="modulation",
    )(c_rows, ada_w, ada_b.reshape(depth, 1, n))


def _ffn_kernel(x_ref, mod_ref, g_ref, wg_ref, wu_ref, wd_ref, o_ref, *, mi, f_chunks):
    x = x_ref[0]
    m = mod_ref[0]
    shift, scale, gate = m[3 * mi:3 * mi + 1], m[3 * mi + 1:3 * mi + 2], m[3 * mi + 2:3 * mi + 3]
    u = (_rms(x) * g_ref[...]) * (1.0 + scale) + shift
    ub = u.astype(BF16)
    y = jnp.zeros(x.shape, F32)
    for lo, hi in f_chunks:
        a = _dot(ub, wg_ref[:, lo:hi])
        b = _dot(ub, wu_ref[:, lo:hi])
        hid = (a * jax.nn.sigmoid(a) * b).astype(BF16)
        y = y + _dot(hid, wd_ref[lo:hi, :])
    o_ref[0] = x + (0.5 * gate) * y


def _ffn(xs, mod_l, g, wg, wu, wd, *, mi, mod_map):
    b, nt, d = xs.shape
    f = wg.shape[1]
    chunk = 1024
    f_chunks = tuple((lo, min(lo + chunk, f)) for lo in range(0, f, chunk))
    tile = pl.BlockSpec((1, TM, d), lambda i, t: (i, t, 0))
    return pl.pallas_call(
        functools.partial(_ffn_kernel, mi=mi, f_chunks=f_chunks),
        grid=(b, nt // TM),
        in_specs=[tile, pl.BlockSpec((1, N_ADA, d), mod_map), _const_spec((1, d)),
                  _const_spec((d, f)), _const_spec((d, f)), _const_spec((f, d))],
        out_specs=tile,
        out_shape=jax.ShapeDtypeStruct(xs.shape, F32),
        input_output_aliases={0: 0},
        compiler_params=_params(),
        name=f"ffn{mi}",
    )(xs, mod_l, g, wg, wu, wd)


def _inproj_kernel(x_ref, mod_ref, g_ref, wcq, wckv, wkr2, wdqT, wdk, wdvT, wpool, wgates,
                   qng, kvng, wuqT, wkpad, wvT, place, perm,
                   cosT, sinT, cosN, sinN, c128, s128,
                   oq_mla, ok_mla, ov_mla, oq_diff, ok_diff, ov_diff, opool, ogates):
    x = x_ref[0]
    m = mod_ref[0]
    u = (_rms(x) * g_ref[...]) * (1.0 + m[4:5]) + m[3:4]
    ub = u.astype(BF16)
    tm = x.shape[0]
    cT, sT = cosT[...], sinT[...]
    hw = ROPE_HALF

    cqn = (_rms(_dot(ub, wcq[...])) * qng[...]).astype(BF16)
    qT = lax.dot_general(wuqT[...], cqn, NT_DIMS, preferred_element_type=F32) * MLA_SCALE
    nope_w = HEADS * MLA_NOPE
    x1, x2 = qT[nope_w:nope_w + LANES], qT[nope_w + LANES:nope_w + 2 * LANES]
    qn = qT[0:nope_w].astype(BF16)
    r1 = (x1 * cT - x2 * sT).astype(BF16)
    r2 = (x1 * sT + x2 * cT).astype(BF16)
    zpad = jnp.zeros((LANES - MLA_QK, tm), BF16)
    for h in range(HEADS):
        oq_mla[0, h, 0:MLA_NOPE, :] = qn[h * MLA_NOPE:(h + 1) * MLA_NOPE]
        oq_mla[0, h, MLA_NOPE:MLA_NOPE + hw, :] = r1[h * hw:(h + 1) * hw]
        oq_mla[0, h, MLA_NOPE + hw:MLA_QK, :] = r2[h * hw:(h + 1) * hw]
        oq_mla[0, h, MLA_QK:LANES, :] = zpad

    ckvn = (_rms(_dot(ub, wckv[...])) * kvng[...]).astype(BF16)
    kr2 = _dot(ub, wkr2[...])
    krr = (kr2[:, 0:LANES] * c128[...] + kr2[:, LANES:2 * LANES] * s128[...]).astype(BF16)
    kall = (_dot(ckvn, wkpad[...]) + _dot(krr, place[...])).astype(BF16)
    vT = lax.dot_general(wvT[...], ckvn, NT_DIMS, preferred_element_type=F32).astype(BF16)
    for h in range(HEADS):
        ok_mla[0, h] = kall[:, h * LANES:(h + 1) * LANES]
    ov_mla[0] = vT

    dqT = lax.dot_general(wdqT[...], ub, NT_DIMS, preferred_element_type=F32) * DIFF_SCALE
    a1, a2, b1, b2 = (dqT[i * LANES:(i + 1) * LANES] for i in range(4))
    parts = [(a1 * cT - a2 * sT).astype(BF16), (a1 * sT + a2 * cT).astype(BF16),
             (b1 * cT - b2 * sT).astype(BF16), (b1 * sT + b2 * cT).astype(BF16)]
    zhalf = jnp.zeros((2 * DIFF_HD, tm), BF16)
    for h in range(HEADS):
        base = (h % 2) * 2 * DIFF_HD
        for i, part in enumerate(parts):
            oq_diff[0, h, base + i * hw:base + (i + 1) * hw, :] = part[h * hw:(h + 1) * hw]
        other = 2 * DIFF_HD - base
        oq_diff[0, h, other:other + 2 * DIFF_HD, :] = zhalf

    dk = _dot(ub, wdk[...])
    cN, sN = cosN[...], sinN[...]
    k1a, k1b, k2a, k2b = (dk[:, i * LANES:(i + 1) * LANES] for i in range(4))
    rk = jnp.concatenate([k1a * cN - k1b * sN, k1a * sN + k1b * cN,
                          k2a * cN - k2b * sN, k2a * sN + k2b * cN], axis=1).astype(BF16)
    kd = _dot(rk, perm[...]).astype(BF16)
    for p in range(HEADS // 2):
        ok_diff[0, p] = kd[:, p * LANES:(p + 1) * LANES]
    ov_diff[0] = lax.dot_general(wdvT[...], ub, NT_DIMS, preferred_element_type=F32).astype(BF16)

    opool[0] = _dot(ub, wpool[...])
    ogates[0] = _dot(ub, wgates[...]).astype(BF16)


def _inproj(xs, mod_l, g, w, tables, *, mod_map):
    b, nt, d = xs.shape
    tile = pl.BlockSpec((1, TM, d), lambda i, t: (i, t, 0))
    tabT = pl.BlockSpec((LANES, TM), lambda i, t: (0, t))
    tabN = pl.BlockSpec((TM, LANES), lambda i, t: (t, 0))
    weights = [w["wcq"], w["wckv"], w["wkr2"], w["wdqT"], w["wdk"], w["wdvT"], w["wpool"], w["wgates"],
               w["qng"], w["kvng"], w["wuqT"], w["wkpad"], w["wvT"], w["place"], w["perm"]]
    vT_spec = pl.BlockSpec((1, HEADS * MLA_V, TM), lambda i, t: (i, 0, t))
    out_shapes = [
        jax.ShapeDtypeStruct((b, HEADS, LANES, nt), BF16),
        jax.ShapeDtypeStruct((b, HEADS, nt, LANES), BF16),
        jax.ShapeDtypeStruct((b, HEADS * MLA_V, nt), BF16),
        jax.ShapeDtypeStruct((b, HEADS, LANES, nt), BF16),
        jax.ShapeDtypeStruct((b, HEADS // 2, nt, LANES), BF16),
        jax.ShapeDtypeStruct((b, HEADS * DIFF_V, nt), BF16),
        jax.ShapeDtypeStruct((b, nt, 4 * POOL_G), F32),
        jax.ShapeDtypeStruct((b, nt, 3 * d), BF16),
    ]
    out_specs = [
        pl.BlockSpec((1, HEADS, LANES, TM), lambda i, t: (i, 0, 0, t)),
        pl.BlockSpec((1, HEADS, TM, LANES), lambda i, t: (i, 0, t, 0)),
        vT_spec,
        pl.BlockSpec((1, HEADS, LANES, TM), lambda i, t: (i, 0, 0, t)),
        pl.BlockSpec((1, HEADS // 2, TM, LANES), lambda i, t: (i, 0, t, 0)),
        vT_spec,
        pl.BlockSpec((1, TM, 4 * POOL_G), lambda i, t: (i, t, 0)),
        pl.BlockSpec((1, TM, 3 * d), lambda i, t: (i, t, 0)),
    ]
    return pl.pallas_call(
        _inproj_kernel,
        grid=(b, nt // TM),
        in_specs=([tile, pl.BlockSpec((1, N_ADA, d), mod_map), _const_spec((1, d))]
                  + [_const_spec(a.shape) for a in weights]
                  + [tabT, tabT, tabN, tabN, tabN, tabN]),
        out_specs=out_specs,
        out_shape=out_shapes,
        compiler_params=_params(),
        name="inproj",
    )(xs, mod_l, g, *weights, tables["cosT"], tables["sinT"], tables["cosN"], tables["sinN"],
      tables["c128"], tables["s128"])


def _attn_kernel(*refs, chunks, groups, diff, lam_init, aliased):
    refs = list(refs)
    q_ref, k_ref, v_ref = refs[:3]
    o_ref = refs[-1]
    assert len(refs) == 4 + 2 * diff + aliased
    q = q_ref[0, 0]
    tq = q.shape[1] // groups
    rhs = []
    for g in range(groups):
        qg = q[:, g * tq:(g + 1) * tq]
        if diff:
            first = (lax.broadcasted_iota(jnp.int32, qg.shape, 0) & DIFF_HD) == 0
            zero = jnp.zeros_like(qg)
            qg = jnp.concatenate([jnp.where(first, qg, zero), jnp.where(first, zero, qg)], axis=1)
        rhs.append(qg)
    n = rhs[0].shape[1]
    state = [(jnp.full((1, n), -1e30, F32), jnp.zeros((1, n), F32), jnp.zeros((v_ref.shape[1], n), F32))
             for _ in range(groups)]

    def scores(c, g):
        lo, hi = chunks[c]
        return _dot(k_ref[0, 0, lo:hi, :], rhs[g])

    order = [(c, g) for c in range(len(chunks)) for g in range(groups)]
    depth = min(ATT_AHEAD * groups, len(order))
    ahead = [scores(c, g) for c, g in order[:depth]]
    for i, (c, g) in enumerate(order):
        s = ahead.pop(0)
        if i + depth < len(order):
            ahead.append(scores(*order[i + depth]))
        m, l, acc = state[g]
        lo, hi = chunks[c]
        m_new = jnp.maximum(m, jnp.max(s, axis=0, keepdims=True))
        alpha = jnp.exp2(m - m_new)
        p = jnp.exp2(s - m_new)
        l = alpha * l + jnp.sum(p, axis=0, keepdims=True)
        acc = alpha * acc + _dot(v_ref[0, :, lo:hi], p.astype(BF16))
        state[g] = (m_new, l, acc)

    for g in range(groups):
        _, l, acc = state[g]
        o = acc / l
        if diff:
            dl_ref, sg_ref = refs[3:5]
            dl = dl_ref[...]
            lam = (jnp.exp(jnp.sum(dl[0:1] * dl[1:2], axis=1, keepdims=True))
                   - jnp.exp(jnp.sum(dl[2:3] * dl[3:4], axis=1, keepdims=True)) + lam_init)
            o = o[:, :tq] - lam * o[:, tq:]
            o = o * lax.rsqrt(jnp.mean(o * o, axis=0, keepdims=True) + EPS)
            o = o * sg_ref[...] * (1.0 - lam_init)
        o_ref[0, :, g * tq:(g + 1) * tq] = o.astype(o_ref.dtype)


def _attention(qT, k, vT, *, tq, groups, q_blocks, q_block0, k_rows, k_block0, chunks,
               diff, extra=(), lam_init=0.0, prev_out=None):
    b, heads, _, nt = qT.shape
    dv = vT.shape[1] // heads
    kdiv = 2 if diff else 1
    tqb = groups * tq
    in_specs = [
        pl.BlockSpec((1, 1, LANES, tqb), lambda i, h, j: (i, h, 0, q_block0 + j)),
        pl.BlockSpec((1, 1, k_rows, LANES), lambda i, h, j: (i, h // kdiv, k_block0, 0)),
        pl.BlockSpec((1, dv, k_rows), lambda i, h, j: (i, h, k_block0)),
    ]
    args = [qT, k, vT]
    if diff:
        in_specs += [pl.BlockSpec(extra[0].shape, lambda i, h, j: (0, 0)),
                     pl.BlockSpec(extra[1].shape, lambda i, h, j: (0, 0))]
        args += list(extra)
    aliases = {}
    if prev_out is not None:
        in_specs.append(pl.BlockSpec(memory_space=pl.ANY))
        args.append(prev_out)
        aliases = {len(args) - 1: 0}
    return pl.pallas_call(
        functools.partial(_attn_kernel, chunks=chunks, groups=groups, diff=diff, lam_init=lam_init,
                          aliased=prev_out is not None),
        grid=(b, heads, q_blocks),
        in_specs=in_specs,
        out_specs=pl.BlockSpec((1, dv, tqb), lambda i, h, j: (i, h, q_block0 + j)),
        out_shape=jax.ShapeDtypeStruct((b, heads * dv, nt), BF16),
        input_output_aliases=aliases,
        compiler_params=_params(),
        name=("diff" if diff else "mla") + ("_ctx" if prev_out is not None else "_x"),
    )(*args)


def _mix_kernel(x_ref, mod_ref, oa_ref, od_ref, pc_ref, pp_ref, pn_ref, gt_ref, bg_ref,
                pproj_ref, pb_ref, ps_ref, wa_ref, wd_ref, wp_ref, wo_ref, o_ref, *, x_tiles):
    t = pl.program_id(1)
    x = x_ref[0]
    gate = mod_ref[0][5:6]
    tm, d = x.shape
    a = lax.dot_general(oa_ref[0], wa_ref[...], TN_DIMS, preferred_element_type=F32)
    dd = lax.dot_general(od_ref[0], wd_ref[...], TN_DIMS, preferred_element_type=F32)

    is_ctx = t == x_tiles
    has_prev = jnp.logical_and(t != 0, jnp.logical_not(is_ctx))
    has_next = jnp.logical_and(t != x_tiles - 1, jnp.logical_not(is_ctx))
    cur = pc_ref[0]
    prev = jnp.where(has_prev, pp_ref[0], 0.0)
    nxt = jnp.where(has_next, pn_ref[0], 0.0)
    ext = jnp.concatenate([prev, cur, nxt], axis=0)
    rows = ext.shape[0]
    seq_len = jnp.where(is_ctx, tm, x_tiles * tm)
    pos = jnp.where(is_ctx, 0, t * tm) + lax.broadcasted_iota(jnp.int32, (tm, 1), 0)
    outs = []
    for g, w in enumerate(POOL_WINDOWS):
        lanes = slice(g * POOL_G, (g + 1) * POOL_G)
        run = ext[:, lanes]
        span = 1
        while span < w:
            run = run + pltpu.roll(run, rows - span, axis=0)
            span *= 2
        win = pltpu.roll(run, rows - (POOL_HALO - w // 2), axis=0)[0:tm]
        cnt = (jnp.minimum(pos + w // 2, seq_len) - jnp.maximum(pos - w // 2, 0)).astype(F32)
        pooled = (win / cnt - cur[:, lanes]).astype(BF16)
        outs.append((_dot(pooled, pproj_ref[g]) + pb_ref[:, lanes]) * ps_ref[:, lanes])
    pooled_out = _dot(jnp.concatenate(outs, axis=1).astype(BF16), wp_ref[...])

    gs = jax.nn.sigmoid(gt_ref[0].astype(F32) + bg_ref[...])
    merged = gs[:, 0:d] * a + gs[:, d:2 * d] * dd + gs[:, 2 * d:3 * d] * pooled_out
    o_ref[0] = x + gate * _dot(merged.astype(BF16), wo_ref[...])


def _mix(xs, mod_l, oa, od, pool_in, gates, w, *, mod_map, x_tiles):
    b, nt, d = xs.shape
    hb = TM // POOL_HALO
    n_halo = nt // POOL_HALO
    tile = pl.BlockSpec((1, TM, d), lambda i, t: (i, t, 0))
    pw = 4 * POOL_G
    weights = [w["bgate"], w["pproj"], w["pb"], w["ps"], w["wa"], w["wd"], w["wp"], w["wo"]]
    return pl.pallas_call(
        functools.partial(_mix_kernel, x_tiles=x_tiles),
        grid=(b, nt // TM),
        in_specs=[tile, pl.BlockSpec((1, N_ADA, d), mod_map),
                  pl.BlockSpec((1, oa.shape[1], TM), lambda i, t: (i, 0, t)),
                  pl.BlockSpec((1, od.shape[1], TM), lambda i, t: (i, 0, t)),
                  pl.BlockSpec((1, TM, pw), lambda i, t: (i, t, 0)),
                  pl.BlockSpec((1, POOL_HALO, pw), lambda i, t: (i, jnp.maximum(t * hb - 1, 0), 0)),
                  pl.BlockSpec((1, POOL_HALO, pw), lambda i, t: (i, jnp.minimum((t + 1) * hb, n_halo - 1), 0)),
                  pl.BlockSpec((1, TM, 3 * d), lambda i, t: (i, t, 0))]
                 + [_const_spec(a.shape) for a in weights],
        out_specs=tile,
        out_shape=jax.ShapeDtypeStruct(xs.shape, F32),
        input_output_aliases={0: 0},
        compiler_params=_params(),
        name="mix",
    )(xs, mod_l, oa, od, pool_in, pool_in, pool_in, gates, *weights)


def _final_kernel(x_ref, g_ref, o_ref):
    o_ref[0] = _rms(x_ref[0]) * g_ref[...]


def _final_norm(xs, g, seq):
    b, _, d = xs.shape
    tile = pl.BlockSpec((1, TM, d), lambda i, t: (i, t, 0))
    return pl.pallas_call(
        _final_kernel,
        grid=(b, seq // TM),
        in_specs=[tile, _const_spec((1, d))],
        out_specs=tile,
        out_shape=jax.ShapeDtypeStruct((b, seq, d), F32),
        compiler_params=_params(),
        name="final_norm",
    )(xs, g)


def _rope_tables(seq, ctx):
    rows = seq // GRID_W
    row_ids = jnp.repeat(jnp.arange(rows), GRID_W).astype(F32)
    col_ids = jnp.tile(jnp.arange(GRID_W), rows).astype(F32)
    n_freq = ROPE_HALF // 2
    inv_freq = ROPE_THETA ** (-jnp.arange(n_freq, dtype=F32) / n_freq)
    ang = jnp.concatenate([row_ids[:, None] * inv_freq, col_ids[:, None] * inv_freq], axis=-1)
    cos = jnp.concatenate([jnp.cos(ang), jnp.ones((ctx, ROPE_HALF), F32)], axis=0)
    sin = jnp.concatenate([jnp.sin(ang), jnp.zeros((ctx, ROPE_HALF), F32)], axis=0)
    pad = jnp.zeros((seq + ctx, LANES - 2 * ROPE_HALF), F32)
    return {
        "cosT": jnp.tile(cos.T, (HEADS, 1)), "sinT": jnp.tile(sin.T, (HEADS, 1)),
        "cosN": jnp.tile(cos, (1, HEADS)), "sinN": jnp.tile(sin, (1, HEADS)),
        "c128": jnp.concatenate([cos, cos, pad], axis=1),
        "s128": jnp.concatenate([sin, sin, pad], axis=1),
    }


def _index_tables():
    hw = ROPE_HALF
    uq = np.zeros(HEADS * MLA_QK, np.int32)
    for h in range(HEADS):
        for j in range(MLA_NOPE):
            uq[h * MLA_NOPE + j] = h * MLA_QK + j
        for f in range(hw):
            uq[HEADS * MLA_NOPE + h * hw + f] = h * MLA_QK + MLA_NOPE + f
            uq[HEADS * MLA_NOPE + LANES + h * hw + f] = h * MLA_QK + MLA_NOPE + hw + f
    dqk = np.zeros(HEADS * 2 * DIFF_HD, np.int32)
    perm = np.zeros((HEADS * 2 * DIFF_HD, HEADS * 2 * DIFF_HD), np.float32)
    for h in range(HEADS):
        for c in range(2):
            for half in range(2):
                for f in range(hw):
                    src = h * 2 * DIFF_HD + c * DIFF_HD + half * hw + f
                    blk = (c * 2 + half) * LANES + h * hw + f
                    dqk[blk] = src
                    perm[blk, src] = 1.0
    v_cols = np.array([h * (MLA_NOPE + MLA_V) + MLA_NOPE + j for h in range(HEADS) for j in range(MLA_V)], np.int32)
    k_mask = ((np.arange(HEADS * (MLA_NOPE + MLA_V)) % (MLA_NOPE + MLA_V)) < MLA_NOPE).astype(np.float32)
    place = np.zeros((LANES, HEADS * LANES), np.float32)
    for h in range(HEADS):
        for f in range(MLA_ROPE):
            place[f, h * LANES + MLA_NOPE + f] = 1.0
    return uq, dqk, perm, v_cols, k_mask, place


def _layer_weights(l, p, idx):
    uq, dqk, perm, v_cols, k_mask, place = idx
    d = p["w_in"].shape[1]
    w_in = p["w_in"][l]
    o_cq, o_ckv, o_kr = 0, 384, 640
    o_dq, o_dk, o_dv, o_pool, o_gate = 672, 1184, 1696, 2208, 2720
    wkr = w_in[:, o_kr:o_kr + MLA_ROPE]
    wkr_rot = jnp.concatenate([-wkr[:, ROPE_HALF:], wkr[:, :ROPE_HALF]], axis=1)
    zpad = jnp.zeros((d, LANES - MLA_ROPE), F32)
    wkr2 = jnp.concatenate([wkr, zpad, wkr_rot, zpad], axis=1)
    w_ukv = p["mla_w_ukv"][l]
    bf = lambda a: a.astype(BF16)
    return {
        "wcq": bf(w_in[:, o_cq:o_ckv]), "wckv": bf(w_in[:, o_ckv:o_kr]), "wkr2": bf(wkr2),
        "wdqT": bf(w_in[:, o_dq + dqk].T), "wdk": bf(w_in[:, o_dk + dqk]),
        "wdvT": bf(w_in[:, o_dv:o_pool].T), "wpool": bf(w_in[:, o_pool:o_gate]), "wgates": bf(w_in[:, o_gate:]),
        "qng": p["mla_q_norm_g"][l][None], "kvng": p["mla_kv_norm_g"][l][None],
        "wuqT": bf(p["mla_w_uq"][l][:, uq].T), "wkpad": bf(w_ukv * k_mask[None]), "wvT": bf(w_ukv[:, v_cols].T),
        "place": jnp.asarray(place, BF16), "perm": jnp.asarray(perm, BF16),
        "bgate": p["b_gate"][l].reshape(1, -1), "pproj": bf(p["pool_proj"][l]),
        "pb": p["pool_b"][l].reshape(1, -1), "ps": p["pool_scale"][l][None],
        "wa": bf(p["w_br_mla"][l]), "wd": bf(p["w_br_diff"][l]), "wp": bf(p["w_br_pool"][l]), "wo": bf(p["w_out"][l]),
    }


def kernel(x, c, ctx, c_ctx, ada_w, ada_b, norm_g, ffa_w_gate, ffa_w_up, ffa_w_down, ffb_w_gate, ffb_w_up, ffb_w_down, w_in, b_gate, mla_q_norm_g, mla_kv_norm_g, mla_w_uq, mla_w_ukv, diff_lambda, diff_subln_g, pool_proj, pool_b, pool_scale, w_br_mla, w_br_diff, w_br_pool, w_out, final_g):
    b, seq, d = x.shape
    n_ctx = ctx.shape[1]
    depth = ada_w.shape[0]
    nt = seq + n_ctx
    assert n_ctx == TM and seq % (ATT_GROUPS * TQ_MLA) == 0 and seq % GRID_W == 0 and b + 1 <= 8
    x_tiles = seq // TM
    p = dict(w_in=w_in, b_gate=b_gate, mla_q_norm_g=mla_q_norm_g, mla_kv_norm_g=mla_kv_norm_g,
             mla_w_uq=mla_w_uq, mla_w_ukv=mla_w_ukv, pool_proj=pool_proj, pool_b=pool_b,
             pool_scale=pool_scale, w_br_mla=w_br_mla, w_br_diff=w_br_diff, w_br_pool=w_br_pool, w_out=w_out)

    c_rows = jnp.concatenate([c, c_ctx[None], jnp.zeros((8 - b - 1, d), F32)], axis=0)
    mod = _modulation(c_rows, ada_w, ada_b).reshape(depth, 8, N_ADA, d)
    mod_map = lambda i, t: (jnp.where(t == x_tiles, b, i), 0, 0)

    tables = _rope_tables(seq, n_ctx)
    idx = _index_tables()
    xs = jnp.concatenate([x, ctx], axis=1)
    chunks = tuple((lo, min(lo + TK, nt)) for lo in range(0, nt, TK))
    bf = lambda a: a.astype(BF16)

    for l in range(depth):
        w = _layer_weights(l, p, idx)
        lam_init = 0.8 - 0.6 * math.exp(-0.3 * l)
        xs = _ffn(xs, mod[l], norm_g[l, 0][None], bf(ffa_w_gate[l]), bf(ffa_w_up[l]), bf(ffa_w_down[l]),
                  mi=0, mod_map=mod_map)
        q_mla, k_mla, v_mla, q_diff, k_diff, v_diff, pool_in, gates = _inproj(
            xs, mod[l], norm_g[l, 1][None], w, tables, mod_map=mod_map)
        extra = (diff_lambda[l], diff_subln_g[l][:, None])
        full = dict(groups=ATT_GROUPS, q_block0=0, k_rows=nt, k_block0=0, chunks=chunks)
        ctx_only = dict(tq=TM, groups=1, q_blocks=1, q_block0=x_tiles, k_rows=TM, k_block0=x_tiles,
                        chunks=((0, TM),))
        oa = _attention(q_mla, k_mla, v_mla, tq=TQ_MLA, q_blocks=seq // (ATT_GROUPS * TQ_MLA), diff=False, **full)
        oa = _attention(q_mla, k_mla, v_mla, diff=False, prev_out=oa, **ctx_only)
        od = _attention(q_diff, k_diff, v_diff, tq=TQ_DIFF, q_blocks=seq // (ATT_GROUPS * TQ_DIFF), diff=True,
                        extra=extra, lam_init=lam_init, **full)
        od = _attention(q_diff, k_diff, v_diff, diff=True, extra=extra, lam_init=lam_init, prev_out=od, **ctx_only)
        xs = _mix(xs, mod[l], oa, od, pool_in, gates, w, mod_map=mod_map, x_tiles=x_tiles)
        xs = _ffn(xs, mod[l], norm_g[l, 2][None], bf(ffb_w_gate[l]), bf(ffb_w_up[l]), bf(ffb_w_down[l]),
                  mi=2, mod_map=mod_map)

    return _final_norm(xs, final_g[None], seq)
```

```python
import functools
import math

import numpy as np
import jax
import jax.numpy as jnp
from jax import lax
from jax.experimental import pallas as pl
from jax.experimental.pallas import tpu as pltpu

F32 = jnp.float32
BF16 = jnp.bfloat16

EPS = 1e-6
ROPE_THETA = 10000.0
GRID_W = 64
N_ADA = 9

HEADS = 8
MLA_NOPE = 64
MLA_ROPE = 32
MLA_V = 64
MLA_QK = MLA_NOPE + MLA_ROPE
LOG2E = math.log2(math.e)
MLA_SCALE = MLA_QK ** -0.5 * LOG2E
DIFF_HD = 32
DIFF_V = 64
DIFF_SCALE = DIFF_HD ** -0.5 * LOG2E
ROPE_HALF = 16
POOL_WINDOWS = (2, 4, 8, 16)
POOL_G = 128
POOL_HALO = 16

LANES = 128
TM = 256
TK = 768
CHUNK_HEAD = (256, 512)
CHUNK_TAIL = (512, 256)
ATT_AHEAD = 2
ATT_ROWS = 256
ATT_NSUB = 4
TQ_MLA = 512
TQ_DIFF = 256
VMEM_LIMIT = 52 * 1024 * 1024

ATT_FLAGS = {}

NT_DIMS = (((1,), (1,)), ((), ()))
TN_DIMS = (((0,), (0,)), ((), ()))


def _params():
    return pltpu.CompilerParams(vmem_limit_bytes=VMEM_LIMIT)


def _const_spec(shape):
    zeros = (0,) * len(shape)
    return pl.BlockSpec(shape, lambda *_: zeros, pipeline_mode=pl.Buffered(1))


def _dot(a, b):
    return jnp.dot(a, b, preferred_element_type=F32)


def _rms(x):
    return x * lax.rsqrt(jnp.mean(x * x, axis=-1, keepdims=True) + EPS)


def _mod_kernel(c_ref, w_ref, b_ref, o_ref):
    c = c_ref[...]
    a = c * jax.nn.sigmoid(c)
    o_ref[0] = jnp.dot(a, w_ref[0], preferred_element_type=F32,
                       precision=lax.Precision.HIGHEST) + b_ref[0]


def _modulation(c_rows, ada_w, ada_b):
    depth, d, n = ada_w.shape
    bn = n // 8
    return pl.pallas_call(
        _mod_kernel,
        grid=(depth, n // bn),
        in_specs=[pl.BlockSpec((8, d), lambda l, j: (0, 0)),
                  pl.BlockSpec((1, d, bn), lambda l, j: (l, 0, j)),
                  pl.BlockSpec((1, 1, bn), lambda l, j: (l, 0, j))],
        out_specs=pl.BlockSpec((1, 8, bn), lambda l, j: (l, 0, j)),
        out_shape=jax.ShapeDtypeStruct((depth, 8, n), F32),
        compiler_params=_params(),
        name="modulation",
    )(c_rows, ada_w, ada_b.reshape(depth, 1, n))


def _ffn_kernel(x_ref, mod_ref, g_ref, wg_ref, wu_ref, wd_ref, o_ref, *, mi, f_chunks):
    x = x_ref[0]
    m = mod_ref[0]
    shift, scale, gate = m[3 * mi:3 * mi + 1], m[3 * mi + 1:3 * mi + 2], m[3 * mi + 2:3 * mi + 3]
    u = (_rms(x) * g_ref[...]) * (1.0 + scale) + shift
    ub = u.astype(BF16)
    y = jnp.zeros(x.shape, F32)
    for lo, hi in f_chunks:
        a = _dot(ub, wg_ref[:, lo:hi])
        b = _dot(ub, wu_ref[:, lo:hi])
        hid = (a * jax.nn.sigmoid(a) * b).astype(BF16)
        y = y + _dot(hid, wd_ref[lo:hi, :])
    o_ref[0] = x + (0.5 * gate) * y


def _ffn(xs, mod_l, g, wg, wu, wd, *, mi, mod_map):
    b, nt, d = xs.shape
    f = wg.shape[1]
    chunk = 1024
    f_chunks = tuple((lo, min(lo + chunk, f)) for lo in range(0, f, chunk))
    tile = pl.BlockSpec((1, TM, d), lambda i, t: (i, t, 0))
    return pl.pallas_call(
        functools.partial(_ffn_kernel, mi=mi, f_chunks=f_chunks),
        grid=(b, nt // TM),
        in_specs=[tile, pl.BlockSpec((1, N_ADA, d), mod_map), _const_spec((1, d)),
                  _const_spec((d, f)), _const_spec((d, f)), _const_spec((f, d))],
        out_specs=tile,
        out_shape=jax.ShapeDtypeStruct(xs.shape, F32),
        input_output_aliases={0: 0},
        compiler_params=_params(),
        name=f"ffn{mi}",
    )(xs, mod_l, g, wg, wu, wd)


def _inproj_kernel(x_ref, mod_ref, g_ref, wcq, wckv, wkr2, wdqT, wdk, wdvT, wpool, wgates,
                   qng, kvng, wuqT, wkpad, wvT, place, perm,
                   cosT, sinT, cosN, sinN, c128, s128,
                   oq_mla, ok_mla, ov_mla, oq_diff, ok_diff, ov_diff, opool, ogates):
    x = x_ref[0]
    m = mod_ref[0]
    u = (_rms(x) * g_ref[...]) * (1.0 + m[4:5]) + m[3:4]
    ub = u.astype(BF16)
    tm = x.shape[0]
    cT, sT = cosT[...], sinT[...]
    hw = ROPE_HALF

    cqn = (_rms(_dot(ub, wcq[...])) * qng[...]).astype(BF16)
    qT = lax.dot_general(wuqT[...], cqn, NT_DIMS, preferred_element_type=F32) * MLA_SCALE
    nope_w = HEADS * MLA_NOPE
    x1, x2 = qT[nope_w:nope_w + LANES], qT[nope_w + LANES:nope_w + 2 * LANES]
    qn = qT[0:nope_w].astype(BF16)
    r1 = (x1 * cT - x2 * sT).astype(BF16)
    r2 = (x1 * sT + x2 * cT).astype(BF16)
    zpad = jnp.zeros((LANES - MLA_QK, tm), BF16)
    for h in range(HEADS):
        oq_mla[0, h, 0, 0:MLA_NOPE, :] = qn[h * MLA_NOPE:(h + 1) * MLA_NOPE]
        oq_mla[0, h, 0, MLA_NOPE:MLA_NOPE + hw, :] = r1[h * hw:(h + 1) * hw]
        oq_mla[0, h, 0, MLA_NOPE + hw:MLA_QK, :] = r2[h * hw:(h + 1) * hw]
        oq_mla[0, h, 0, MLA_QK:LANES, :] = zpad

    ckvn = (_rms(_dot(ub, wckv[...])) * kvng[...]).astype(BF16)
    kr2 = _dot(ub, wkr2[...])
    krr = (kr2[:, 0:LANES] * c128[...] + kr2[:, LANES:2 * LANES] * s128[...]).astype(BF16)
    kall = (_dot(ckvn, wkpad[...]) + _dot(krr, place[...])).astype(BF16)
    vT = lax.dot_general(wvT[...], ckvn, NT_DIMS, preferred_element_type=F32).astype(BF16)
    for h in range(HEADS):
        ok_mla[0, h] = kall[:, h * LANES:(h + 1) * LANES]
    ov_mla[0] = vT

    dqT = lax.dot_general(wdqT[...], ub, NT_DIMS, preferred_element_type=F32) * DIFF_SCALE
    a1, a2, b1, b2 = (dqT[i * LANES:(i + 1) * LANES] for i in range(4))
    parts = [(a1 * cT - a2 * sT).astype(BF16), (a1 * sT + a2 * cT).astype(BF16),
             (b1 * cT - b2 * sT).astype(BF16), (b1 * sT + b2 * cT).astype(BF16)]
    zhalf = jnp.zeros((2 * DIFF_HD, tm), BF16)
    for h in range(HEADS):
        base = (h % 2) * 2 * DIFF_HD
        for i, part in enumerate(parts):
            oq_diff[0, h, 0, base + i * hw:base + (i + 1) * hw, :] = part[h * hw:(h + 1) * hw]
        other = 2 * DIFF_HD - base
        oq_diff[0, h, 0, other:other + 2 * DIFF_HD, :] = zhalf

    dk = _dot(ub, wdk[...])
    cN, sN = cosN[...], sinN[...]
    k1a, k1b, k2a, k2b = (dk[:, i * LANES:(i + 1) * LANES] for i in range(4))
    rk = jnp.concatenate([k1a * cN - k1b * sN, k1a * sN + k1b * cN,
                          k2a * cN - k2b * sN, k2a * sN + k2b * cN], axis=1).astype(BF16)
    kd = _dot(rk, perm[...]).astype(BF16)
    for p in range(HEADS // 2):
        ok_diff[0, p] = kd[:, p * LANES:(p + 1) * LANES]
    ov_diff[0] = lax.dot_general(wdvT[...], ub, NT_DIMS, preferred_element_type=F32).astype(BF16)

    opool[0] = _dot(ub, wpool[...])
    ogates[0] = _dot(ub, wgates[...]).astype(BF16)


def _inproj(xs, mod_l, g, w, tables, *, mod_map):
    b, nt, d = xs.shape
    tile = pl.BlockSpec((1, TM, d), lambda i, t: (i, t, 0))
    tabT = pl.BlockSpec((LANES, TM), lambda i, t: (0, t))
    tabN = pl.BlockSpec((TM, LANES), lambda i, t: (t, 0))
    weights = [w["wcq"], w["wckv"], w["wkr2"], w["wdqT"], w["wdk"], w["wdvT"], w["wpool"], w["wgates"],
               w["qng"], w["kvng"], w["wuqT"], w["wkpad"], w["wvT"], w["place"], w["perm"]]
    vT_spec = pl.BlockSpec((1, HEADS * MLA_V, TM), lambda i, t: (i, 0, t))

    def q_spec(tq):
        r = tq // TM
        return pl.BlockSpec((1, HEADS, 1, LANES, TM), lambda i, t: (i, 0, t // r, 0, t % r))

    out_shapes = [
        jax.ShapeDtypeStruct((b, HEADS, pl.cdiv(nt, TQ_MLA), LANES, TQ_MLA), BF16),
        jax.ShapeDtypeStruct((b, HEADS, nt, LANES), BF16),
        jax.ShapeDtypeStruct((b, HEADS * MLA_V, nt), BF16),
        jax.ShapeDtypeStruct((b, HEADS, pl.cdiv(nt, TQ_DIFF), LANES, TQ_DIFF), BF16),
        jax.ShapeDtypeStruct((b, HEADS // 2, nt, LANES), BF16),
        jax.ShapeDtypeStruct((b, HEADS * DIFF_V, nt), BF16),
        jax.ShapeDtypeStruct((b, nt, 4 * POOL_G), F32),
        jax.ShapeDtypeStruct((b, nt, 3 * d), BF16),
    ]
    out_specs = [
        q_spec(TQ_MLA),
        pl.BlockSpec((1, HEADS, TM, LANES), lambda i, t: (i, 0, t, 0)),
        vT_spec,
        q_spec(TQ_DIFF),
        pl.BlockSpec((1, HEADS // 2, TM, LANES), lambda i, t: (i, 0, t, 0)),
        vT_spec,
        pl.BlockSpec((1, TM, 4 * POOL_G), lambda i, t: (i, t, 0)),
        pl.BlockSpec((1, TM, 3 * d), lambda i, t: (i, t, 0)),
    ]
    return pl.pallas_call(
        _inproj_kernel,
        grid=(b, nt // TM),
        in_specs=([tile, pl.BlockSpec((1, N_ADA, d), mod_map), _const_spec((1, d))]
                  + [_const_spec(a.shape) for a in weights]
                  + [tabT, tabT, tabN, tabN, tabN, tabN]),
        out_specs=out_specs,
        out_shape=out_shapes,
        compiler_params=_params(),
        name="inproj",
    )(xs, mod_l, g, *weights, tables["cosT"], tables["sinT"], tables["cosN"], tables["sinN"],
      tables["c128"], tables["s128"])


def _attn_kernel(*refs, chunks, diff, lam_init, aliased):
    refs = list(refs)
    q_ref, k_ref, v_ref = refs[:3]
    n_buf = ATT_AHEAD + 1
    s_bufs = refs[-n_buf:]
    o_ref = refs[-n_buf - 1]
    assert len(refs) == 4 + 2 * diff + aliased + n_buf
    nsub, tq = q_ref.shape[2], q_ref.shape[4]
    nc = len(chunks)
    size = lambda c: chunks[c][1] - chunks[c][0]

    def sub_block(sub, zero_row):
        q = q_ref[0, 0, sub]
        if diff:
            first = (lax.broadcasted_iota(jnp.int32, q.shape, 0) & DIFF_HD) == 0
            zero = jnp.zeros_like(q)
            rhs = jnp.concatenate([jnp.where(first, q, zero), jnp.where(first, zero, q)], axis=1)
        else:
            rhs = q
        n = rhs.shape[1]

        def score_piece(c, off, cmax):
            lo, hi = chunks[c]
            rows = min(ATT_ROWS, hi - lo - off)
            s = _dot(k_ref[0, 0, lo + off:lo + off + rows, :], rhs)
            s_bufs[c % n_buf][pl.ds(zero_row + off, rows), :] = s
            pmax = jnp.max(s, axis=0, keepdims=True)
            return pmax if cmax is None else jnp.maximum(cmax, pmax)

        def softmax_piece(c, off, m_new, lsum, pv):
            lo, hi = chunks[c]
            rows = min(ATT_ROWS, hi - lo - off)
            p = jnp.exp2(s_bufs[c % n_buf][pl.ds(zero_row + off, rows), :] - m_new)
            psum = jnp.sum(p, axis=0, keepdims=True)
            part = _dot(v_ref[0, :, lo + off:lo + off + rows], p.astype(BF16))
            return (psum if lsum is None else lsum + psum), (part if pv is None else pv + part)

        m = jnp.full((1, n), -1e30, F32)
        l = jnp.zeros((1, n), F32)
        acc = jnp.zeros((v_ref.shape[1], n), F32)
        cmaxes = {}
        for c in range(min(ATT_AHEAD, nc)):
            cmaxes[c] = None
            for off in range(0, size(c), ATT_ROWS):
                cmaxes[c] = score_piece(c, off, cmaxes[c])
        for c in range(nc):
            m_new = jnp.maximum(m, cmaxes.pop(c))
            alpha = jnp.exp2(m - m_new)
            cn = c + ATT_AHEAD
            lsum, pv, nmax = None, None, None
            for off in range(0, max(size(c), size(cn) if cn < nc else 0), ATT_ROWS):
                if cn < nc and off < size(cn):
                    nmax = score_piece(cn, off, nmax)
                if off < size(c):
                    lsum, pv = softmax_piece(c, off, m_new, lsum, pv)
            if cn < nc:
                cmaxes[cn] = nmax
            m, l, acc = m_new, alpha * l + lsum, alpha * acc + pv

        o = acc / l
        if diff:
            dl_ref, sg_ref = refs[3:5]
            dl = dl_ref[...]
            lam = (jnp.exp(jnp.sum(dl[0:1] * dl[1:2], axis=1, keepdims=True))
                   - jnp.exp(jnp.sum(dl[2:3] * dl[3:4], axis=1, keepdims=True)) + lam_init)
            o = o[:, :tq] - lam * o[:, tq:]
            o = o * lax.rsqrt(jnp.mean(o * o, axis=0, keepdims=True) + EPS)
            o = o * sg_ref[...] * (1.0 - lam_init)
        o_ref[0, sub] = o.astype(o_ref.dtype)

    zero_row = pl.multiple_of(jnp.minimum(pl.program_id(2), 0), 16)
    if nsub == 1:
        sub_block(0, zero_row)
    else:
        def body(sub, carry):
            sub_block(sub, zero_row)
            return carry
        lax.fori_loop(0, nsub, body, 0)


def _key_chunks(nk):
    head = [c for c in CHUNK_HEAD if sum(CHUNK_HEAD) + sum(CHUNK_TAIL) + TK <= nk]
    tail = CHUNK_TAIL if head else ()
    sizes = list(head)
    body = nk - sum(head) - sum(tail)
    sizes += [TK] * (body // TK) + ([body % TK] if body % TK else [])
    sizes += list(tail)
    edges = np.cumsum([0] + sizes)
    assert edges[-1] == nk and all(s % LANES == 0 for s in sizes)
    return tuple((int(a), int(b)) for a, b in zip(edges[:-1], edges[1:]))


def _attention(qT, k, vT, *, tq, nsub, steps, q_block0, k_rows, k_block0, chunks,
               diff, extra=(), lam_init=0.0, prev_out=None):
    b, heads, nqb, _, tqw = qT.shape
    dv = vT.shape[1] // heads
    kdiv = 2 if diff else 1
    in_specs = [
        pl.BlockSpec((1, 1, nsub, LANES, tq), lambda i, h, j: (i, h, q_block0 + j, 0, 0)),
        pl.BlockSpec((1, 1, k_rows, LANES), lambda i, h, j: (i, h // kdiv, k_block0, 0)),
        pl.BlockSpec((1, dv, k_rows), lambda i, h, j: (i, h, k_block0)),
    ]
    args = [qT, k, vT]
    if diff:
        in_specs += [pl.BlockSpec(extra[0].shape, lambda i, h, j: (0, 0)),
                     pl.BlockSpec(extra[1].shape, lambda i, h, j: (0, 0))]
        args += list(extra)
    aliases = {}
    if prev_out is not None:
        in_specs.append(pl.BlockSpec(memory_space=pl.ANY))
        args.append(prev_out)
        aliases = {len(args) - 1: 0}
    return pl.pallas_call(
        functools.partial(_attn_kernel, chunks=chunks, diff=diff, lam_init=lam_init,
                          aliased=prev_out is not None),
        grid=(b, heads, steps),
        in_specs=in_specs,
        out_specs=pl.BlockSpec((1, nsub, dv, tq), lambda i, h, j: (i, q_block0 + j, h, 0)),
        out_shape=jax.ShapeDtypeStruct((b, nqb, heads * dv, tqw), BF16),
        scratch_shapes=[pltpu.VMEM((max(hi - lo for lo, hi in chunks), (2 * tq if diff else tq)), dt)
                        for dt in [F32] * (ATT_AHEAD + 1)],
        input_output_aliases=aliases,
        compiler_params=pltpu.CompilerParams(vmem_limit_bytes=VMEM_LIMIT, flags=ATT_FLAGS),
        name=("diff" if diff else "mla") + ("_ctx" if prev_out is not None else "_x"),
    )(*args)


def _mix_kernel(x_ref, mod_ref, oa_ref, od_ref, pc_ref, pp_ref, pn_ref, gt_ref, bg_ref,
                pproj_ref, pb_ref, ps_ref, wa_ref, wd_ref, wp_ref, wo_ref, o_ref, *, x_tiles):
    t = pl.program_id(1)
    x = x_ref[0]
    gate = mod_ref[0][5:6]
    tm, d = x.shape
    a = lax.dot_general(oa_ref[0, 0], wa_ref[...], TN_DIMS, preferred_element_type=F32)
    dd = lax.dot_general(od_ref[0, 0], wd_ref[...], TN_DIMS, preferred_element_type=F32)

    is_ctx = t == x_tiles
    has_prev = jnp.logical_and(t != 0, jnp.logical_not(is_ctx))
    has_next = jnp.logical_and(t != x_tiles - 1, jnp.logical_not(is_ctx))
    cur = pc_ref[0]
    prev = jnp.where(has_prev, pp_ref[0], 0.0)
    nxt = jnp.where(has_next, pn_ref[0], 0.0)
    ext = jnp.concatenate([prev, cur, nxt], axis=0)
    rows = ext.shape[0]
    seq_len = jnp.where(is_ctx, tm, x_tiles * tm)
    pos = jnp.where(is_ctx, 0, t * tm) + lax.broadcasted_iota(jnp.int32, (tm, 1), 0)
    outs = []
    for g, w in enumerate(POOL_WINDOWS):
        lanes = slice(g * POOL_G, (g + 1) * POOL_G)
        run = ext[:, lanes]
        span = 1
        while span < w:
            run = run + pltpu.roll(run, rows - span, axis=0)
            span *= 2
        win = pltpu.roll(run, rows - (POOL_HALO - w // 2), axis=0)[0:tm]
        cnt = (jnp.minimum(pos + w // 2, seq_len) - jnp.maximum(pos - w // 2, 0)).astype(F32)
        pooled = (win / cnt - cur[:, lanes]).astype(BF16)
        outs.append((_dot(pooled, pproj_ref[g]) + pb_ref[:, lanes]) * ps_ref[:, lanes])
    pooled_out = _dot(jnp.concatenate(outs, axis=1).astype(BF16), wp_ref[...])

    gs = jax.nn.sigmoid(gt_ref[0].astype(F32) + bg_ref[...])
    merged = gs[:, 0:d] * a + gs[:, d:2 * d] * dd + gs[:, 2 * d:3 * d] * pooled_out
    o_ref[0] = x + gate * _dot(merged.astype(BF16), wo_ref[...])


def _mix(xs, mod_l, oa, od, pool_in, gates, w, *, mod_map, x_tiles):
    b, nt, d = xs.shape
    hb = TM // POOL_HALO
    n_halo = nt // POOL_HALO
    tile = pl.BlockSpec((1, TM, d), lambda i, t: (i, t, 0))
    pw = 4 * POOL_G
    weights = [w["bgate"], w["pproj"], w["pb"], w["ps"], w["wa"], w["wd"], w["wp"], w["wo"]]

    def o_spec(o):
        r = o.shape[3] // TM
        return pl.BlockSpec((1, 1, o.shape[2], TM), lambda i, t: (i, t // r, 0, t % r))

    return pl.pallas_call(
        functools.partial(_mix_kernel, x_tiles=x_tiles),
        grid=(b, nt // TM),
        in_specs=[tile, pl.BlockSpec((1, N_ADA, d), mod_map),
                  o_spec(oa), o_spec(od),
                  pl.BlockSpec((1, TM, pw), lambda i, t: (i, t, 0)),
                  pl.BlockSpec((1, POOL_HALO, pw), lambda i, t: (i, jnp.maximum(t * hb - 1, 0), 0)),
                  pl.BlockSpec((1, POOL_HALO, pw), lambda i, t: (i, jnp.minimum((t + 1) * hb, n_halo - 1), 0)),
                  pl.BlockSpec((1, TM, 3 * d), lambda i, t: (i, t, 0))]
                 + [_const_spec(a.shape) for a in weights],
        out_specs=tile,
        out_shape=jax.ShapeDtypeStruct(xs.shape, F32),
        input_output_aliases={0: 0},
        compiler_params=_params(),
        name="mix",
    )(xs, mod_l, oa, od, pool_in, pool_in, pool_in, gates, *weights)


def _final_kernel(x_ref, g_ref, o_ref):
    o_ref[0] = _rms(x_ref[0]) * g_ref[...]


def _final_norm(xs, g, seq):
    b, _, d = xs.shape
    tile = pl.BlockSpec((1, TM, d), lambda i, t: (i, t, 0))
    return pl.pallas_call(
        _final_kernel,
        grid=(b, seq // TM),
        in_specs=[tile, _const_spec((1, d))],
        out_specs=tile,
        out_shape=jax.ShapeDtypeStruct((b, seq, d), F32),
        compiler_params=_params(),
        name="final_norm",
    )(xs, g)


def _rope_tables(seq, ctx):
    rows = seq // GRID_W
    row_ids = jnp.repeat(jnp.arange(rows), GRID_W).astype(F32)
    col_ids = jnp.tile(jnp.arange(GRID_W), rows).astype(F32)
    n_freq = ROPE_HALF // 2
    inv_freq = ROPE_THETA ** (-jnp.arange(n_freq, dtype=F32) / n_freq)
    ang = jnp.concatenate([row_ids[:, None] * inv_freq, col_ids[:, None] * inv_freq], axis=-1)
    cos = jnp.concatenate([jnp.cos(ang), jnp.ones((ctx, ROPE_HALF), F32)], axis=0)
    sin = jnp.concatenate([jnp.sin(ang), jnp.zeros((ctx, ROPE_HALF), F32)], axis=0)
    pad = jnp.zeros((seq + ctx, LANES - 2 * ROPE_HALF), F32)
    return {
        "cosT": jnp.tile(cos.T, (HEADS, 1)), "sinT": jnp.tile(sin.T, (HEADS, 1)),
        "cosN": jnp.tile(cos, (1, HEADS)), "sinN": jnp.tile(sin, (1, HEADS)),
        "c128": jnp.concatenate([cos, cos, pad], axis=1),
        "s128": jnp.concatenate([sin, sin, pad], axis=1),
    }


def _index_tables():
    hw = ROPE_HALF
    uq = np.zeros(HEADS * MLA_QK, np.int32)
    for h in range(HEADS):
        for j in range(MLA_NOPE):
            uq[h * MLA_NOPE + j] = h * MLA_QK + j
        for f in range(hw):
            uq[HEADS * MLA_NOPE + h * hw + f] = h * MLA_QK + MLA_NOPE + f
            uq[HEADS * MLA_NOPE + LANES + h * hw + f] = h * MLA_QK + MLA_NOPE + hw + f
    dqk = np.zeros(HEADS * 2 * DIFF_HD, np.int32)
    perm = np.zeros((HEADS * 2 * DIFF_HD, HEADS * 2 * DIFF_HD), np.float32)
    for h in range(HEADS):
        for c in range(2):
            for half in range(2):
                for f in range(hw):
                    src = h * 2 * DIFF_HD + c * DIFF_HD + half * hw + f
                    blk = (c * 2 + half) * LANES + h * hw + f
                    dqk[blk] = src
                    perm[blk, src] = 1.0
    v_cols = np.array([h * (MLA_NOPE + MLA_V) + MLA_NOPE + j for h in range(HEADS) for j in range(MLA_V)], np.int32)
    k_mask = ((np.arange(HEADS * (MLA_NOPE + MLA_V)) % (MLA_NOPE + MLA_V)) < MLA_NOPE).astype(np.float32)
    place = np.zeros((LANES, HEADS * LANES), np.float32)
    for h in range(HEADS):
        for f in range(MLA_ROPE):
            place[f, h * LANES + MLA_NOPE + f] = 1.0
    return uq, dqk, perm, v_cols, k_mask, place


def _layer_weights(l, p, idx):
    uq, dqk, perm, v_cols, k_mask, place = idx
    d = p["w_in"].shape[1]
    w_in = p["w_in"][l]
    o_cq, o_ckv, o_kr = 0, 384, 640
    o_dq, o_dk, o_dv, o_pool, o_gate = 672, 1184, 1696, 2208, 2720
    wkr = w_in[:, o_kr:o_kr + MLA_ROPE]
    wkr_rot = jnp.concatenate([-wkr[:, ROPE_HALF:], wkr[:, :ROPE_HALF]], axis=1)
    zpad = jnp.zeros((d, LANES - MLA_ROPE), F32)
    wkr2 = jnp.concatenate([wkr, zpad, wkr_rot, zpad], axis=1)
    w_ukv = p["mla_w_ukv"][l]
    bf = lambda a: a.astype(BF16)
    return {
        "wcq": bf(w_in[:, o_cq:o_ckv]), "wckv": bf(w_in[:, o_ckv:o_kr]), "wkr2": bf(wkr2),
        "wdqT": bf(w_in[:, o_dq + dqk].T), "wdk": bf(w_in[:, o_dk + dqk]),
        "wdvT": bf(w_in[:, o_dv:o_pool].T), "wpool": bf(w_in[:, o_pool:o_gate]), "wgates": bf(w_in[:, o_gate:]),
        "qng": p["mla_q_norm_g"][l][None], "kvng": p["mla_kv_norm_g"][l][None],
        "wuqT": bf(p["mla_w_uq"][l][:, uq].T), "wkpad": bf(w_ukv * k_mask[None]), "wvT": bf(w_ukv[:, v_cols].T),
        "place": jnp.asarray(place, BF16), "perm": jnp.asarray(perm, BF16),
        "bgate": p["b_gate"][l].reshape(1, -1), "pproj": bf(p["pool_proj"][l]),
        "pb": p["pool_b"][l].reshape(1, -1), "ps": p["pool_scale"][l][None],
        "wa": bf(p["w_br_mla"][l]), "wd": bf(p["w_br_diff"][l]), "wp": bf(p["w_br_pool"][l]), "wo": bf(p["w_out"][l]),
    }


def kernel(x, c, ctx, c_ctx, ada_w, ada_b, norm_g, ffa_w_gate, ffa_w_up, ffa_w_down, ffb_w_gate, ffb_w_up, ffb_w_down, w_in, b_gate, mla_q_norm_g, mla_kv_norm_g, mla_w_uq, mla_w_ukv, diff_lambda, diff_subln_g, pool_proj, pool_b, pool_scale, w_br_mla, w_br_diff, w_br_pool, w_out, final_g):
    b, seq, d = x.shape
    n_ctx = ctx.shape[1]
    depth = ada_w.shape[0]
    nt = seq + n_ctx
    nsub = ATT_NSUB if seq % (ATT_NSUB * TQ_MLA) == 0 else 1
    assert n_ctx == TM and seq % (nsub * TQ_MLA) == 0 and seq % GRID_W == 0 and b + 1 <= 8
    x_tiles = seq // TM
    p = dict(w_in=w_in, b_gate=b_gate, mla_q_norm_g=mla_q_norm_g, mla_kv_norm_g=mla_kv_norm_g,
             mla_w_uq=mla_w_uq, mla_w_ukv=mla_w_ukv, pool_proj=pool_proj, pool_b=pool_b,
             pool_scale=pool_scale, w_br_mla=w_br_mla, w_br_diff=w_br_diff, w_br_pool=w_br_pool, w_out=w_out)

    c_rows = jnp.concatenate([c, c_ctx[None], jnp.zeros((8 - b - 1, d), F32)], axis=0)
    mod = _modulation(c_rows, ada_w, ada_b).reshape(depth, 8, N_ADA, d)
    mod_map = lambda i, t: (jnp.where(t == x_tiles, b, i), 0, 0)

    tables = _rope_tables(seq, n_ctx)
    idx = _index_tables()
    xs = jnp.concatenate([x, ctx], axis=1)
    chunks = _key_chunks(nt)
    bf = lambda a: a.astype(BF16)

    for l in range(depth):
        w = _layer_weights(l, p, idx)
        lam_init = 0.8 - 0.6 * math.exp(-0.3 * l)
        xs = _ffn(xs, mod[l], norm_g[l, 0][None], bf(ffa_w_gate[l]), bf(ffa_w_up[l]), bf(ffa_w_down[l]),
                  mi=0, mod_map=mod_map)
        q_mla, k_mla, v_mla, q_diff, k_diff, v_diff, pool_in, gates = _inproj(
            xs, mod[l], norm_g[l, 1][None], w, tables, mod_map=mod_map)
        extra = (diff_lambda[l], diff_subln_g[l][:, None])
        full = dict(nsub=nsub, q_block0=0, k_rows=nt, k_block0=0, chunks=chunks)
        ctx_only = dict(tq=TM, nsub=1, steps=1, k_rows=TM, k_block0=x_tiles, chunks=((0, TM),))
        oa = _attention(q_mla, k_mla, v_mla, tq=TQ_MLA, steps=seq // (nsub * TQ_MLA), diff=False, **full)
        oa = _attention(q_mla, k_mla, v_mla, q_block0=seq // TQ_MLA, diff=False, prev_out=oa, **ctx_only)
        od = _attention(q_diff, k_diff, v_diff, tq=TQ_DIFF, steps=seq // (nsub * TQ_DIFF), diff=True,
                        extra=extra, lam_init=lam_init, **full)
        od = _attention(q_diff, k_diff, v_diff, q_block0=seq // TQ_DIFF, diff=True, extra=extra,
                        lam_init=lam_init, prev_out=od, **ctx_only)
        xs = _mix(xs, mod[l], oa, od, pool_in, gates, w, mod_map=mod_map, x_tiles=x_tiles)
        xs = _ffn(xs, mod[l], norm_g[l, 2][None], bf(ffb_w_gate[l]), bf(ffb_w_up[l]), bf(ffb_w_down[l]),
                  mi=2, mod_map=mod_map)

    return _final_norm(xs, final_g[None], seq)
```

```python
import functools
import math

import numpy as np
import jax
import jax.numpy as jnp
from jax import lax
from jax.experimental import pallas as pl
from jax.experimental.pallas import tpu as pltpu

F32 = jnp.float32
BF16 = jnp.bfloat16

EPS = 1e-6
ROPE_THETA = 10000.0
GRID_W = 64
N_ADA = 9

HEADS = 8
MLA_NOPE = 64
MLA_ROPE = 32
MLA_V = 64
V_ROWS = MLA_V + 16
MLA_QK = MLA_NOPE + MLA_ROPE
LOG2E = math.log2(math.e)
MLA_SCALE = MLA_QK ** -0.5 * LOG2E
DIFF_HD = 32
DIFF_V = 64
DIFF_SCALE = DIFF_HD ** -0.5 * LOG2E
ROPE_HALF = 16
POOL_WINDOWS = (2, 4, 8, 16)
POOL_G = 128
POOL_HALO = 16

LANES = 128
TM = 256
TK = 768
CHUNK_HEAD = (256, 512)
CHUNK_TAIL = (512, 256)
ATT_AHEAD = 2
ATT_BOUND_MAX = 40.0
ATT_NSUB = 4
TQ_MLA = 512
TQ_DIFF = 256
VMEM_LIMIT = 52 * 1024 * 1024

ATT_FLAGS = {}

NT_DIMS = (((1,), (1,)), ((), ()))
TN_DIMS = (((0,), (0,)), ((), ()))


def _params():
    return pltpu.CompilerParams(vmem_limit_bytes=VMEM_LIMIT)


def _const_spec(shape):
    zeros = (0,) * len(shape)
    return pl.BlockSpec(shape, lambda *_: zeros, pipeline_mode=pl.Buffered(1))


def _dot(a, b):
    return jnp.dot(a, b, preferred_element_type=F32)


def _rms(x):
    return x * lax.rsqrt(jnp.mean(x * x, axis=-1, keepdims=True) + EPS)


def _mod_kernel(c_ref, w_ref, b_ref, o_ref):
    c = c_ref[...]
    a = c * jax.nn.sigmoid(c)
    o_ref[0] = jnp.dot(a, w_ref[0], preferred_element_type=F32,
                       precision=lax.Precision.HIGHEST) + b_ref[0]


def _modulation(c_rows, ada_w, ada_b):
    depth, d, n = ada_w.shape
    bn = n // 8
    return pl.pallas_call(
        _mod_kernel,
        grid=(depth, n // bn),
        in_specs=[pl.BlockSpec((8, d), lambda l, j: (0, 0)),
                  pl.BlockSpec((1, d, bn), lambda l, j: (l, 0, j)),
                  pl.BlockSpec((1, 1, bn), lambda l, j: (l, 0, j))],
        out_specs=pl.BlockSpec((1, 8, bn), lambda l, j: (l, 0, j)),
        out_shape=jax.ShapeDtypeStruct((depth, 8, n), F32),
        compiler_params=_params(),
        name="modulation",
    )(c_rows, ada_w, ada_b.reshape(depth, 1, n))


def _ffn_kernel(x_ref, mod_ref, g_ref, wg_ref, wu_ref, wd_ref, o_ref, *, mi, f_chunks):
    x = x_ref[0]
    m = mod_ref[0]
    shift, scale, gate = m[3 * mi:3 * mi + 1], m[3 * mi + 1:3 * mi + 2], m[3 * mi + 2:3 * mi + 3]
    u = (_rms(x) * g_ref[...]) * (1.0 + scale) + shift
    ub = u.astype(BF16)
    y = jnp.zeros(x.shape, F32)
    for lo, hi in f_chunks:
        a = _dot(ub, wg_ref[:, lo:hi])
        b = _dot(ub, wu_ref[:, lo:hi])
        hid = (a * jax.nn.sigmoid(a) * b).astype(BF16)
        y = y + _dot(hid, wd_ref[lo:hi, :])
    o_ref[0] = x + (0.5 * gate) * y


def _ffn(xs, mod_l, g, wg, wu, wd, *, mi, mod_map):
    b, nt, d = xs.shape
    f = wg.shape[1]
    chunk = 1024
    f_chunks = tuple((lo, min(lo + chunk, f)) for lo in range(0, f, chunk))
    tile = pl.BlockSpec((1, TM, d), lambda i, t: (i, t, 0))
    return pl.pallas_call(
        functools.partial(_ffn_kernel, mi=mi, f_chunks=f_chunks),
        grid=(b, nt // TM),
        in_specs=[tile, pl.BlockSpec((1, N_ADA, d), mod_map), _const_spec((1, d)),
                  _const_spec((d, f)), _const_spec((d, f)), _const_spec((f, d))],
        out_specs=tile,
        out_shape=jax.ShapeDtypeStruct(xs.shape, F32),
        input_output_aliases={0: 0},
        compiler_params=_params(),
        name=f"ffn{mi}",
    )(xs, mod_l, g, wg, wu, wd)


def _inproj_kernel(x_ref, mod_ref, g_ref, wcq, wckv, wkr2, wdqT, wdk, wdvT, wpool, wgates,
                   qng, kvng, wuqT, wkpad, wvT, place, perm,
                   cosT, sinT, cosN, sinN, c128, s128,
                   oq_mla, ok_mla, ov_mla, oq_diff, ok_diff, ov_diff, opool, ogates, okn_mla, okn_diff):
    x = x_ref[0]
    m = mod_ref[0]
    u = (_rms(x) * g_ref[...]) * (1.0 + m[4:5]) + m[3:4]
    ub = u.astype(BF16)
    tm = x.shape[0]
    cT, sT = cosT[...], sinT[...]
    hw = ROPE_HALF

    cqn = (_rms(_dot(ub, wcq[...])) * qng[...]).astype(BF16)
    qT = lax.dot_general(wuqT[...], cqn, NT_DIMS, preferred_element_type=F32) * MLA_SCALE
    nope_w = HEADS * MLA_NOPE
    x1, x2 = qT[nope_w:nope_w + LANES], qT[nope_w + LANES:nope_w + 2 * LANES]
    qn = qT[0:nope_w].astype(BF16)
    r1 = (x1 * cT - x2 * sT).astype(BF16)
    r2 = (x1 * sT + x2 * cT).astype(BF16)
    zpad = jnp.zeros((LANES - MLA_QK, tm), BF16)
    for h in range(HEADS):
        oq_mla[0, h, 0, 0:MLA_NOPE, :] = qn[h * MLA_NOPE:(h + 1) * MLA_NOPE]
        oq_mla[0, h, 0, MLA_NOPE:MLA_NOPE + hw, :] = r1[h * hw:(h + 1) * hw]
        oq_mla[0, h, 0, MLA_NOPE + hw:MLA_QK, :] = r2[h * hw:(h + 1) * hw]
        oq_mla[0, h, 0, MLA_QK:LANES, :] = zpad

    ckvn = (_rms(_dot(ub, wckv[...])) * kvng[...]).astype(BF16)
    kr2 = _dot(ub, wkr2[...])
    krr = (kr2[:, 0:LANES] * c128[...] + kr2[:, LANES:2 * LANES] * s128[...]).astype(BF16)
    kall = (_dot(ckvn, wkpad[...]) + _dot(krr, place[...])).astype(BF16)
    vT = lax.dot_general(wvT[...], ckvn, NT_DIMS, preferred_element_type=F32).astype(BF16)
    head_row = lax.broadcasted_iota(jnp.int32, (HEADS, LANES), 0)
    left_half = lax.broadcasted_iota(jnp.int32, (tm, LANES), 1) < 2 * DIFF_HD

    def max_sq_norm(sq):
        return jnp.max(jnp.sum(sq, axis=1, keepdims=True), axis=0, keepdims=True)

    kn = jnp.zeros((HEADS, LANES), F32)
    for h in range(HEADS):
        kh = kall[:, h * LANES:(h + 1) * LANES]
        ok_mla[0, h] = kh
        khf = kh.astype(F32)
        kn = jnp.where(head_row == h, max_sq_norm(khf * khf), kn)
    okn_mla[0, 0] = kn
    ones_rows = (lax.broadcasted_iota(jnp.int32, (V_ROWS - MLA_V, tm), 0) == 0).astype(BF16)

    def store_values(ov, vals):
        for h in range(HEADS):
            ov[0, h * V_ROWS:h * V_ROWS + MLA_V, :] = vals[h * MLA_V:(h + 1) * MLA_V]
            ov[0, h * V_ROWS + MLA_V:(h + 1) * V_ROWS, :] = ones_rows

    store_values(ov_mla, vT)

    dqT = lax.dot_general(wdqT[...], ub, NT_DIMS, preferred_element_type=F32) * DIFF_SCALE
    a1, a2, b1, b2 = (dqT[i * LANES:(i + 1) * LANES] for i in range(4))
    parts = [(a1 * cT - a2 * sT).astype(BF16), (a1 * sT + a2 * cT).astype(BF16),
             (b1 * cT - b2 * sT).astype(BF16), (b1 * sT + b2 * cT).astype(BF16)]
    zhalf = jnp.zeros((2 * DIFF_HD, tm), BF16)
    for h in range(HEADS):
        base = (h % 2) * 2 * DIFF_HD
        for i, part in enumerate(parts):
            oq_diff[0, h, 0, base + i * hw:base + (i + 1) * hw, :] = part[h * hw:(h + 1) * hw]
        other = 2 * DIFF_HD - base
        oq_diff[0, h, 0, other:other + 2 * DIFF_HD, :] = zhalf

    dk = _dot(ub, wdk[...])
    cN, sN = cosN[...], sinN[...]
    k1a, k1b, k2a, k2b = (dk[:, i * LANES:(i + 1) * LANES] for i in range(4))
    rk = jnp.concatenate([k1a * cN - k1b * sN, k1a * sN + k1b * cN,
                          k2a * cN - k2b * sN, k2a * sN + k2b * cN], axis=1).astype(BF16)
    kd = _dot(rk, perm[...]).astype(BF16)
    kn = jnp.zeros((HEADS, LANES), F32)
    for p in range(HEADS // 2):
        kp = kd[:, p * LANES:(p + 1) * LANES]
        ok_diff[0, p] = kp
        kpf = kp.astype(F32)
        sq = kpf * kpf
        kn = jnp.where(head_row == 2 * p, max_sq_norm(jnp.where(left_half, sq, 0.0)), kn)
        kn = jnp.where(head_row == 2 * p + 1, max_sq_norm(jnp.where(left_half, 0.0, sq)), kn)
    okn_diff[0, 0] = kn
    store_values(ov_diff, lax.dot_general(wdvT[...], ub, NT_DIMS, preferred_element_type=F32).astype(BF16))

    opool[0] = _dot(ub, wpool[...])
    ogates[0] = _dot(ub, wgates[...]).astype(BF16)


def _inproj(xs, mod_l, g, w, tables, *, mod_map):
    b, nt, d = xs.shape
    tile = pl.BlockSpec((1, TM, d), lambda i, t: (i, t, 0))
    tabT = pl.BlockSpec((LANES, TM), lambda i, t: (0, t))
    tabN = pl.BlockSpec((TM, LANES), lambda i, t: (t, 0))
    weights = [w["wcq"], w["wckv"], w["wkr2"], w["wdqT"], w["wdk"], w["wdvT"], w["wpool"], w["wgates"],
               w["qng"], w["kvng"], w["wuqT"], w["wkpad"], w["wvT"], w["place"], w["perm"]]
    vT_spec = pl.BlockSpec((1, HEADS * V_ROWS, TM), lambda i, t: (i, 0, t))

    def q_spec(tq):
        r = tq // TM
        return pl.BlockSpec((1, HEADS, 1, LANES, TM), lambda i, t: (i, 0, t // r, 0, t % r))

    out_shapes = [
        jax.ShapeDtypeStruct((b, HEADS, pl.cdiv(nt, TQ_MLA), LANES, TQ_MLA), BF16),
        jax.ShapeDtypeStruct((b, HEADS, nt, LANES), BF16),
        jax.ShapeDtypeStruct((b, HEADS * V_ROWS, nt), BF16),
        jax.ShapeDtypeStruct((b, HEADS, pl.cdiv(nt, TQ_DIFF), LANES, TQ_DIFF), BF16),
        jax.ShapeDtypeStruct((b, HEADS // 2, nt, LANES), BF16),
        jax.ShapeDtypeStruct((b, HEADS * V_ROWS, nt), BF16),
        jax.ShapeDtypeStruct((b, nt, 4 * POOL_G), F32),
        jax.ShapeDtypeStruct((b, nt, 3 * d), BF16),
        jax.ShapeDtypeStruct((b, nt // TM, HEADS, LANES), F32),
        jax.ShapeDtypeStruct((b, nt // TM, HEADS, LANES), F32),
    ]
    kn_spec = pl.BlockSpec((1, 1, HEADS, LANES), lambda i, t: (i, t, 0, 0))
    out_specs = [
        q_spec(TQ_MLA),
        pl.BlockSpec((1, HEADS, TM, LANES), lambda i, t: (i, 0, t, 0)),
        vT_spec,
        q_spec(TQ_DIFF),
        pl.BlockSpec((1, HEADS // 2, TM, LANES), lambda i, t: (i, 0, t, 0)),
        vT_spec,
        pl.BlockSpec((1, TM, 4 * POOL_G), lambda i, t: (i, t, 0)),
        pl.BlockSpec((1, TM, 3 * d), lambda i, t: (i, t, 0)),
        kn_spec, kn_spec,
    ]
    return pl.pallas_call(
        _inproj_kernel,
        grid=(b, nt // TM),
        in_specs=([tile, pl.BlockSpec((1, N_ADA, d), mod_map), _const_spec((1, d))]
                  + [_const_spec(a.shape) for a in weights]
                  + [tabT, tabT, tabN, tabN, tabN, tabN]),
        out_specs=out_specs,
        out_shape=out_shapes,
        compiler_params=_params(),
        name="inproj",
    )(xs, mod_l, g, *weights, tables["cosT"], tables["sinT"], tables["cosN"], tables["sinN"],
      tables["c128"], tables["s128"])


def _attn_kernel(*refs, chunks, diff, lam_init, aliased):
    refs = list(refs)
    q_ref, k_ref, v_ref, kn_ref = refs[:4]
    n_buf = ATT_AHEAD + 1
    s_bufs = refs[-n_buf:]
    o_ref = refs[-n_buf - 1]
    assert len(refs) == 5 + 2 * diff + aliased + n_buf
    nsub, tq = q_ref.shape[2], q_ref.shape[4]
    nc = len(chunks)
    head = pl.program_id(1)

    tile_max = jnp.max(kn_ref[0], axis=0)
    rows = lax.broadcasted_iota(jnp.int32, tile_max.shape, 0)
    kmax2 = jnp.max(jnp.where(rows == head, tile_max, 0.0), axis=0, keepdims=True)[:, 0:1]

    def finish(num, den, sub):
        o = num / den
        if diff:
            dl_ref, sg_ref = refs[4:6]
            dl = dl_ref[...]
            lam = (jnp.exp(jnp.sum(dl[0:1] * dl[1:2], axis=1, keepdims=True))
                   - jnp.exp(jnp.sum(dl[2:3] * dl[3:4], axis=1, keepdims=True)) + lam_init)
            o = o[:, :tq] - lam * o[:, tq:]
            o = o * lax.rsqrt(jnp.mean(o * o, axis=0, keepdims=True) + EPS)
            o = o * sg_ref[...] * (1.0 - lam_init)
        o_ref[0, sub] = o.astype(o_ref.dtype)

    def bounded_shift(rhs, bound, sub):
        acc = jnp.zeros((MLA_V, rhs.shape[1]), F32)
        den = jnp.zeros((1, rhs.shape[1]), F32)
        logits = lambda c: _dot(k_ref[0, 0, chunks[c][0]:chunks[c][1], :], rhs)
        s_next = logits(0)
        for c, (lo, hi) in enumerate(chunks):
            s = s_next
            if c + 1 < nc:
                s_next = logits(c + 1)
            p = jnp.exp2(s - bound)
            den = den + jnp.sum(p, axis=0, keepdims=True)
            acc = acc + _dot(v_ref[0, 0:MLA_V, lo:hi], p.astype(BF16))
        finish(acc, den, sub)

    def running_max(rhs, sub, zero_row):
        n = rhs.shape[1]

        def scores(c):
            lo, hi = chunks[c]
            s = _dot(k_ref[0, 0, lo:hi, :], rhs)
            s_bufs[c % n_buf][pl.ds(zero_row, hi - lo), :] = s
            return jnp.max(s, axis=0, keepdims=True)

        m = jnp.full((1, n), -1e30, F32)
        acc = jnp.zeros((V_ROWS, n), F32)
        cmaxes = {c: scores(c) for c in range(min(ATT_AHEAD, nc))}
        for c in range(nc):
            lo, hi = chunks[c]
            m_new = jnp.maximum(m, cmaxes.pop(c))
            alpha = jnp.exp2(m - m_new)
            if c + ATT_AHEAD < nc:
                cmaxes[c + ATT_AHEAD] = scores(c + ATT_AHEAD)
            p = jnp.exp2(s_bufs[c % n_buf][pl.ds(zero_row, hi - lo), :] - m_new)
            m, acc = m_new, alpha * acc + _dot(v_ref[0, :, lo:hi], p.astype(BF16))
        finish(acc[0:MLA_V], acc[MLA_V:MLA_V + 1], sub)

    def sub_block(sub, zero_row):
        q = q_ref[0, 0, sub]
        if diff:
            first = (lax.broadcasted_iota(jnp.int32, q.shape, 0) & DIFF_HD) == 0
            zero = jnp.zeros_like(q)
            rhs = jnp.concatenate([jnp.where(first, q, zero), jnp.where(first, zero, q)], axis=1)
        else:
            rhs = q
        qf = rhs.astype(F32)
        bound = jnp.sqrt(jnp.sum(qf * qf, axis=0, keepdims=True) * kmax2)
        small = jnp.max(bound) <= ATT_BOUND_MAX

        @pl.when(small)
        def _():
            bounded_shift(rhs, bound, sub)

        @pl.when(jnp.logical_not(small))
        def _():
            running_max(rhs, sub, zero_row)

    zero_row = pl.multiple_of(jnp.minimum(pl.program_id(2), 0), 16)
    if nsub == 1:
        sub_block(0, zero_row)
    else:
        def body(sub, carry):
            sub_block(sub, zero_row)
            return carry
        lax.fori_loop(0, nsub, body, 0)


def _key_chunks(nk):
    head = [c for c in CHUNK_HEAD if sum(CHUNK_HEAD) + sum(CHUNK_TAIL) + TK <= nk]
    tail = CHUNK_TAIL if head else ()
    sizes = list(head)
    body = nk - sum(head) - sum(tail)
    sizes += [TK] * (body // TK) + ([body % TK] if body % TK else [])
    sizes += list(tail)
    edges = np.cumsum([0] + sizes)
    assert edges[-1] == nk and all(s % LANES == 0 for s in sizes)
    return tuple((int(a), int(b)) for a, b in zip(edges[:-1], edges[1:]))


def _attention(qT, k, vT, kn, *, tq, nsub, steps, q_block0, k_rows, k_block0, chunks,
               diff, extra=(), lam_init=0.0, prev_out=None):
    b, heads, nqb, _, tqw = qT.shape
    dv = MLA_V
    kdiv = 2 if diff else 1
    in_specs = [
        pl.BlockSpec((1, 1, nsub, LANES, tq), lambda i, h, j: (i, h, q_block0 + j, 0, 0)),
        pl.BlockSpec((1, 1, k_rows, LANES), lambda i, h, j: (i, h // kdiv, k_block0, 0)),
        pl.BlockSpec((1, V_ROWS, k_rows), lambda i, h, j: (i, h, k_block0)),
        pl.BlockSpec((1,) + kn.shape[1:], lambda i, h, j: (i, 0, 0, 0)),
    ]
    args = [qT, k, vT, kn]
    if diff:
        in_specs += [pl.BlockSpec(extra[0].shape, lambda i, h, j: (0, 0)),
                     pl.BlockSpec(extra[1].shape, lambda i, h, j: (0, 0))]
        args += list(extra)
    aliases = {}
    if prev_out is not None:
        in_specs.append(pl.BlockSpec(memory_space=pl.ANY))
        args.append(prev_out)
        aliases = {len(args) - 1: 0}
    return pl.pallas_call(
        functools.partial(_attn_kernel, chunks=chunks, diff=diff, lam_init=lam_init,
                          aliased=prev_out is not None),
        grid=(b, heads, steps),
        in_specs=in_specs,
        out_specs=pl.BlockSpec((1, nsub, dv, tq), lambda i, h, j: (i, q_block0 + j, h, 0)),
        out_shape=jax.ShapeDtypeStruct((b, nqb, heads * dv, tqw), BF16),
        scratch_shapes=[pltpu.VMEM((max(hi - lo for lo, hi in chunks), (2 * tq if diff else tq)), dt)
                        for dt in [F32] * (ATT_AHEAD + 1)],
        input_output_aliases=aliases,
        compiler_params=pltpu.CompilerParams(vmem_limit_bytes=VMEM_LIMIT, flags=ATT_FLAGS),
        name=("diff" if diff else "mla") + ("_ctx" if prev_out is not None else "_x"),
    )(*args)


def _mix_kernel(x_ref, mod_ref, oa_ref, od_ref, pc_ref, pp_ref, pn_ref, gt_ref, bg_ref,
                pproj_ref, pb_ref, ps_ref, wa_ref, wd_ref, wp_ref, wo_ref, o_ref, *, x_tiles):
    t = pl.program_id(1)
    x = x_ref[0]
    gate = mod_ref[0][5:6]
    tm, d = x.shape
    a = lax.dot_general(oa_ref[0, 0], wa_ref[...], TN_DIMS, preferred_element_type=F32)
    dd = lax.dot_general(od_ref[0, 0], wd_ref[...], TN_DIMS, preferred_element_type=F32)

    is_ctx = t == x_tiles
    has_prev = jnp.logical_and(t != 0, jnp.logical_not(is_ctx))
    has_next = jnp.logical_and(t != x_tiles - 1, jnp.logical_not(is_ctx))
    cur = pc_ref[0]
    prev = jnp.where(has_prev, pp_ref[0], 0.0)
    nxt = jnp.where(has_next, pn_ref[0], 0.0)
    ext = jnp.concatenate([prev, cur, nxt], axis=0)
    rows = ext.shape[0]
    seq_len = jnp.where(is_ctx, tm, x_tiles * tm)
    pos = jnp.where(is_ctx, 0, t * tm) + lax.broadcasted_iota(jnp.int32, (tm, 1), 0)
    outs = []
    for g, w in enumerate(POOL_WINDOWS):
        lanes = slice(g * POOL_G, (g + 1) * POOL_G)
        run = ext[:, lanes]
        span = 1
        while span < w:
            run = run + pltpu.roll(run, rows - span, axis=0)
            span *= 2
        win = pltpu.roll(run, rows - (POOL_HALO - w // 2), axis=0)[0:tm]
        cnt = (jnp.minimum(pos + w // 2, seq_len) - jnp.maximum(pos - w // 2, 0)).astype(F32)
        pooled = (win / cnt - cur[:, lanes]).astype(BF16)
        outs.append((_dot(pooled, pproj_ref[g]) + pb_ref[:, lanes]) * ps_ref[:, lanes])
    pooled_out = _dot(jnp.concatenate(outs, axis=1).astype(BF16), wp_ref[...])

    gs = jax.nn.sigmoid(gt_ref[0].astype(F32) + bg_ref[...])
    merged = gs[:, 0:d] * a + gs[:, d:2 * d] * dd + gs[:, 2 * d:3 * d] * pooled_out
    o_ref[0] = x + gate * _dot(merged.astype(BF16), wo_ref[...])


def _mix(xs, mod_l, oa, od, pool_in, gates, w, *, mod_map, x_tiles):
    b, nt, d = xs.shape
    hb = TM // POOL_HALO
    n_halo = nt // POOL_HALO
    tile = pl.BlockSpec((1, TM, d), lambda i, t: (i, t, 0))
    pw = 4 * POOL_G
    weights = [w["bgate"], w["pproj"], w["pb"], w["ps"], w["wa"], w["wd"], w["wp"], w["wo"]]

    def o_spec(o):
        r = o.shape[3] // TM
        return pl.BlockSpec((1, 1, o.shape[2], TM), lambda i, t: (i, t // r, 0, t % r))

    return pl.pallas_call(
        functools.partial(_mix_kernel, x_tiles=x_tiles),
        grid=(b, nt // TM),
        in_specs=[tile, pl.BlockSpec((1, N_ADA, d), mod_map),
                  o_spec(oa), o_spec(od),
                  pl.BlockSpec((1, TM, pw), lambda i, t: (i, t, 0)),
                  pl.BlockSpec((1, POOL_HALO, pw), lambda i, t: (i, jnp.maximum(t * hb - 1, 0), 0)),
                  pl.BlockSpec((1, POOL_HALO, pw), lambda i, t: (i, jnp.minimum((t + 1) * hb, n_halo - 1), 0)),
                  pl.BlockSpec((1, TM, 3 * d), lambda i, t: (i, t, 0))]
                 + [_const_spec(a.shape) for a in weights],
        out_specs=tile,
        out_shape=jax.ShapeDtypeStruct(xs.shape, F32),
        input_output_aliases={0: 0},
        compiler_params=_params(),
        name="mix",
    )(xs, mod_l, oa, od, pool_in, pool_in, pool_in, gates, *weights)


def _final_kernel(x_ref, g_ref, o_ref):
    o_ref[0] = _rms(x_ref[0]) * g_ref[...]


def _final_norm(xs, g, seq):
    b, _, d = xs.shape
    tile = pl.BlockSpec((1, TM, d), lambda i, t: (i, t, 0))
    return pl.pallas_call(
        _final_kernel,
        grid=(b, seq // TM),
        in_specs=[tile, _const_spec((1, d))],
        out_specs=tile,
        out_shape=jax.ShapeDtypeStruct((b, seq, d), F32),
        compiler_params=_params(),
        name="final_norm",
    )(xs, g)


def _rope_tables(seq, ctx):
    rows = seq // GRID_W
    row_ids = jnp.repeat(jnp.arange(rows), GRID_W).astype(F32)
    col_ids = jnp.tile(jnp.arange(GRID_W), rows).astype(F32)
    n_freq = ROPE_HALF // 2
    inv_freq = ROPE_THETA ** (-jnp.arange(n_freq, dtype=F32) / n_freq)
    ang = jnp.concatenate([row_ids[:, None] * inv_freq, col_ids[:, None] * inv_freq], axis=-1)
    cos = jnp.concatenate([jnp.cos(ang), jnp.ones((ctx, ROPE_HALF), F32)], axis=0)
    sin = jnp.concatenate([jnp.sin(ang), jnp.zeros((ctx, ROPE_HALF), F32)], axis=0)
    pad = jnp.zeros((seq + ctx, LANES - 2 * ROPE_HALF), F32)
    return {
        "cosT": jnp.tile(cos.T, (HEADS, 1)), "sinT": jnp.tile(sin.T, (HEADS, 1)),
        "cosN": jnp.tile(cos, (1, HEADS)), "sinN": jnp.tile(sin, (1, HEADS)),
        "c128": jnp.concatenate([cos, cos, pad], axis=1),
        "s128": jnp.concatenate([sin, sin, pad], axis=1),
    }


def _index_tables():
    hw = ROPE_HALF
    uq = np.zeros(HEADS * MLA_QK, np.int32)
    for h in range(HEADS):
        for j in range(MLA_NOPE):
            uq[h * MLA_NOPE + j] = h * MLA_QK + j
        for f in range(hw):
            uq[HEADS * MLA_NOPE + h * hw + f] = h * MLA_QK + MLA_NOPE + f
            uq[HEADS * MLA_NOPE + LANES + h * hw + f] = h * MLA_QK + MLA_NOPE + hw + f
    dqk = np.zeros(HEADS * 2 * DIFF_HD, np.int32)
    perm = np.zeros((HEADS * 2 * DIFF_HD, HEADS * 2 * DIFF_HD), np.float32)
    for h in range(HEADS):
        for c in range(2):
            for half in range(2):
                for f in range(hw):
                    src = h * 2 * DIFF_HD + c * DIFF_HD + half * hw + f
                    blk = (c * 2 + half) * LANES + h * hw + f
                    dqk[blk] = src
                    perm[blk, src] = 1.0
    v_cols = np.array([h * (MLA_NOPE + MLA_V) + MLA_NOPE + j for h in range(HEADS) for j in range(MLA_V)], np.int32)
    k_mask = ((np.arange(HEADS * (MLA_NOPE + MLA_V)) % (MLA_NOPE + MLA_V)) < MLA_NOPE).astype(np.float32)
    place = np.zeros((LANES, HEADS * LANES), np.float32)
    for h in range(HEADS):
        for f in range(MLA_ROPE):
            place[f, h * LANES + MLA_NOPE + f] = 1.0
    return uq, dqk, perm, v_cols, k_mask, place


def _layer_weights(l, p, idx):
    uq, dqk, perm, v_cols, k_mask, place = idx
    d = p["w_in"].shape[1]
    w_in = p["w_in"][l]
    o_cq, o_ckv, o_kr = 0, 384, 640
    o_dq, o_dk, o_dv, o_pool, o_gate = 672, 1184, 1696, 2208, 2720
    wkr = w_in[:, o_kr:o_kr + MLA_ROPE]
    wkr_rot = jnp.concatenate([-wkr[:, ROPE_HALF:], wkr[:, :ROPE_HALF]], axis=1)
    zpad = jnp.zeros((d, LANES - MLA_ROPE), F32)
    wkr2 = jnp.concatenate([wkr, zpad, wkr_rot, zpad], axis=1)
    w_ukv = p["mla_w_ukv"][l]
    bf = lambda a: a.astype(BF16)
    return {
        "wcq": bf(w_in[:, o_cq:o_ckv]), "wckv": bf(w_in[:, o_ckv:o_kr]), "wkr2": bf(wkr2),
        "wdqT": bf(w_in[:, o_dq + dqk].T), "wdk": bf(w_in[:, o_dk + dqk]),
        "wdvT": bf(w_in[:, o_dv:o_pool].T), "wpool": bf(w_in[:, o_pool:o_gate]), "wgates": bf(w_in[:, o_gate:]),
        "qng": p["mla_q_norm_g"][l][None], "kvng": p["mla_kv_norm_g"][l][None],
        "wuqT": bf(p["mla_w_uq"][l][:, uq].T), "wkpad": bf(w_ukv * k_mask[None]), "wvT": bf(w_ukv[:, v_cols].T),
        "place": jnp.asarray(place, BF16), "perm": jnp.asarray(perm, BF16),
        "bgate": p["b_gate"][l].reshape(1, -1), "pproj": bf(p["pool_proj"][l]),
        "pb": p["pool_b"][l].reshape(1, -1), "ps": p["pool_scale"][l][None],
        "wa": bf(p["w_br_mla"][l]), "wd": bf(p["w_br_diff"][l]), "wp": bf(p["w_br_pool"][l]), "wo": bf(p["w_out"][l]),
    }


def kernel(x, c, ctx, c_ctx, ada_w, ada_b, norm_g, ffa_w_gate, ffa_w_up, ffa_w_down, ffb_w_gate, ffb_w_up, ffb_w_down, w_in, b_gate, mla_q_norm_g, mla_kv_norm_g, mla_w_uq, mla_w_ukv, diff_lambda, diff_subln_g, pool_proj, pool_b, pool_scale, w_br_mla, w_br_diff, w_br_pool, w_out, final_g):
    b, seq, d = x.shape
    n_ctx = ctx.shape[1]
    depth = ada_w.shape[0]
    nt = seq + n_ctx
    nsub = ATT_NSUB if seq % (ATT_NSUB * TQ_MLA) == 0 else 1
    assert n_ctx == TM and seq % (nsub * TQ_MLA) == 0 and seq % GRID_W == 0 and b + 1 <= 8
    x_tiles = seq // TM
    p = dict(w_in=w_in, b_gate=b_gate, mla_q_norm_g=mla_q_norm_g, mla_kv_norm_g=mla_kv_norm_g,
             mla_w_uq=mla_w_uq, mla_w_ukv=mla_w_ukv, pool_proj=pool_proj, pool_b=pool_b,
             pool_scale=pool_scale, w_br_mla=w_br_mla, w_br_diff=w_br_diff, w_br_pool=w_br_pool, w_out=w_out)

    c_rows = jnp.concatenate([c, c_ctx[None], jnp.zeros((8 - b - 1, d), F32)], axis=0)
    mod = _modulation(c_rows, ada_w, ada_b).reshape(depth, 8, N_ADA, d)
    mod_map = lambda i, t: (jnp.where(t == x_tiles, b, i), 0, 0)

    tables = _rope_tables(seq, n_ctx)
    idx = _index_tables()
    xs = jnp.concatenate([x, ctx], axis=1)
    chunks = _key_chunks(nt)
    bf = lambda a: a.astype(BF16)

    for l in range(depth):
        w = _layer_weights(l, p, idx)
        lam_init = 0.8 - 0.6 * math.exp(-0.3 * l)
        xs = _ffn(xs, mod[l], norm_g[l, 0][None], bf(ffa_w_gate[l]), bf(ffa_w_up[l]), bf(ffa_w_down[l]),
                  mi=0, mod_map=mod_map)
        q_mla, k_mla, v_mla, q_diff, k_diff, v_diff, pool_in, gates, kn_mla, kn_diff = _inproj(
            xs, mod[l], norm_g[l, 1][None], w, tables, mod_map=mod_map)
        mla = (q_mla, k_mla, v_mla, kn_mla)
        dif = (q_diff, k_diff, v_diff, kn_diff)
        extra = (diff_lambda[l], diff_subln_g[l][:, None])
        full = dict(nsub=nsub, q_block0=0, k_rows=nt, k_block0=0, chunks=chunks)
        ctx_only = dict(tq=TM, nsub=1, steps=1, k_rows=TM, k_block0=x_tiles, chunks=((0, TM),))
        oa = _attention(*mla, tq=TQ_MLA, steps=seq // (nsub * TQ_MLA), diff=False, **full)
        oa = _attention(*mla, q_block0=seq // TQ_MLA, diff=False, prev_out=oa, **ctx_only)
        od = _attention(*dif, tq=TQ_DIFF, steps=seq // (nsub * TQ_DIFF), diff=True,
                        extra=extra, lam_init=lam_init, **full)
        od = _attention(*dif, q_block0=seq // TQ_DIFF, diff=True, extra=extra,
                        lam_init=lam_init, prev_out=od, **ctx_only)
        xs = _mix(xs, mod[l], oa, od, pool_in, gates, w, mod_map=mod_map, x_tiles=x_tiles)
        xs = _ffn(xs, mod[l], norm_g[l, 2][None], bf(ffb_w_gate[l]), bf(ffb_w_up[l]), bf(ffb_w_down[l]),
                  mi=2, mod_map=mod_map)

    return _final_norm(xs, final_g[None], seq)
```

```python
import functools
import math

import numpy as np
import jax
import jax.numpy as jnp
from jax import lax
from jax.experimental import pallas as pl
from jax.experimental.pallas import tpu as pltpu

F32 = jnp.float32
BF16 = jnp.bfloat16

EPS = 1e-6
ROPE_THETA = 10000.0
GRID_W = 64
N_ADA = 9

HEADS = 8
MLA_NOPE = 64
MLA_ROPE = 32
MLA_V = 64
V_ROWS = MLA_V + 16
MLA_QK = MLA_NOPE + MLA_ROPE
LOG2E = math.log2(math.e)
MLA_SCALE = MLA_QK ** -0.5 * LOG2E
DIFF_HD = 32
DIFF_V = 64
DIFF_SCALE = DIFF_HD ** -0.5 * LOG2E
ROPE_HALF = 16
POOL_WINDOWS = (2, 4, 8, 16)
POOL_G = 128
POOL_HALO = 16

LANES = 128
TM = 256
TK = 768
CHUNK_HEAD = (256, 512)
CHUNK_TAIL = (512, 256)
ATT_AHEAD = 2
ATT_BOUND_MAX = 40.0
ATT_NSUB = 4
TQ_MLA = 1024
TQ_DIFF = 512
VMEM_LIMIT = 52 * 1024 * 1024

ATT_FLAGS = {}

NT_DIMS = (((1,), (1,)), ((), ()))
TN_DIMS = (((0,), (0,)), ((), ()))


def _params():
    return pltpu.CompilerParams(vmem_limit_bytes=VMEM_LIMIT)


def _const_spec(shape):
    zeros = (0,) * len(shape)
    return pl.BlockSpec(shape, lambda *_: zeros, pipeline_mode=pl.Buffered(1))


def _dot(a, b):
    return jnp.dot(a, b, preferred_element_type=F32)


def _rms(x):
    return x * lax.rsqrt(jnp.mean(x * x, axis=-1, keepdims=True) + EPS)


def _mod_kernel(c_ref, w_ref, b_ref, o_ref):
    c = c_ref[...]
    a = c * jax.nn.sigmoid(c)
    o_ref[0] = jnp.dot(a, w_ref[0], preferred_element_type=F32,
                       precision=lax.Precision.HIGHEST) + b_ref[0]


def _modulation(c_rows, ada_w, ada_b):
    depth, d, n = ada_w.shape
    bn = n // 8
    return pl.pallas_call(
        _mod_kernel,
        grid=(depth, n // bn),
        in_specs=[pl.BlockSpec((8, d), lambda l, j: (0, 0)),
                  pl.BlockSpec((1, d, bn), lambda l, j: (l, 0, j)),
                  pl.BlockSpec((1, 1, bn), lambda l, j: (l, 0, j))],
        out_specs=pl.BlockSpec((1, 8, bn), lambda l, j: (l, 0, j)),
        out_shape=jax.ShapeDtypeStruct((depth, 8, n), F32),
        compiler_params=_params(),
        name="modulation",
    )(c_rows, ada_w, ada_b.reshape(depth, 1, n))


def _ffn_kernel(x_ref, mod_ref, g_ref, wg_ref, wu_ref, wd_ref, o_ref, *, mi, f_chunks):
    x = x_ref[0]
    m = mod_ref[0]
    shift, scale, gate = m[3 * mi:3 * mi + 1], m[3 * mi + 1:3 * mi + 2], m[3 * mi + 2:3 * mi + 3]
    u = (_rms(x) * g_ref[...]) * (1.0 + scale) + shift
    ub = u.astype(BF16)
    y = jnp.zeros(x.shape, F32)
    for lo, hi in f_chunks:
        a = _dot(ub, wg_ref[:, lo:hi])
        b = _dot(ub, wu_ref[:, lo:hi])
        hid = (a * jax.nn.sigmoid(a) * b).astype(BF16)
        y = y + _dot(hid, wd_ref[lo:hi, :])
    o_ref[0] = x + (0.5 * gate) * y


def _ffn(xs, mod_l, g, wg, wu, wd, *, mi, mod_map):
    b, nt, d = xs.shape
    f = wg.shape[1]
    chunk = 1024
    f_chunks = tuple((lo, min(lo + chunk, f)) for lo in range(0, f, chunk))
    tile = pl.BlockSpec((1, TM, d), lambda i, t: (i, t, 0))
    return pl.pallas_call(
        functools.partial(_ffn_kernel, mi=mi, f_chunks=f_chunks),
        grid=(b, nt // TM),
        in_specs=[tile, pl.BlockSpec((1, N_ADA, d), mod_map), _const_spec((1, d)),
                  _const_spec((d, f)), _const_spec((d, f)), _const_spec((f, d))],
        out_specs=tile,
        out_shape=jax.ShapeDtypeStruct(xs.shape, F32),
        input_output_aliases={0: 0},
        compiler_params=_params(),
        name=f"ffn{mi}",
    )(xs, mod_l, g, wg, wu, wd)


def _inproj_kernel(x_ref, mod_ref, g_ref, wcq, wckv, wkr2, wdqT, wdk, wdvT, wpool, wgates,
                   qng, kvng, wuqT, wkpad, wvT, place, perm,
                   cosT, sinT, cosN, sinN, c128, s128,
                   oq_mla, ok_mla, ov_mla, oq_diff, ok_diff, ov_diff, opool, ogates, okn_mla, okn_diff):
    x = x_ref[0]
    m = mod_ref[0]
    u = (_rms(x) * g_ref[...]) * (1.0 + m[4:5]) + m[3:4]
    ub = u.astype(BF16)
    tm = x.shape[0]
    cT, sT = cosT[...], sinT[...]
    hw = ROPE_HALF

    cqn = (_rms(_dot(ub, wcq[...])) * qng[...]).astype(BF16)
    qT = lax.dot_general(wuqT[...], cqn, NT_DIMS, preferred_element_type=F32) * MLA_SCALE
    nope_w = HEADS * MLA_NOPE
    x1, x2 = qT[nope_w:nope_w + LANES], qT[nope_w + LANES:nope_w + 2 * LANES]
    qn = qT[0:nope_w].astype(BF16)
    r1 = (x1 * cT - x2 * sT).astype(BF16)
    r2 = (x1 * sT + x2 * cT).astype(BF16)
    zpad = jnp.zeros((LANES - MLA_QK, tm), BF16)
    for h in range(HEADS):
        oq_mla[0, h, 0, 0:MLA_NOPE, :] = qn[h * MLA_NOPE:(h + 1) * MLA_NOPE]
        oq_mla[0, h, 0, MLA_NOPE:MLA_NOPE + hw, :] = r1[h * hw:(h + 1) * hw]
        oq_mla[0, h, 0, MLA_NOPE + hw:MLA_QK, :] = r2[h * hw:(h + 1) * hw]
        oq_mla[0, h, 0, MLA_QK:LANES, :] = zpad

    ckvn = (_rms(_dot(ub, wckv[...])) * kvng[...]).astype(BF16)
    kr2 = _dot(ub, wkr2[...])
    krr = (kr2[:, 0:LANES] * c128[...] + kr2[:, LANES:2 * LANES] * s128[...]).astype(BF16)
    kall = (_dot(ckvn, wkpad[...]) + _dot(krr, place[...])).astype(BF16)
    vT = lax.dot_general(wvT[...], ckvn, NT_DIMS, preferred_element_type=F32).astype(BF16)
    head_row = lax.broadcasted_iota(jnp.int32, (HEADS, LANES), 0)
    left_half = lax.broadcasted_iota(jnp.int32, (tm, LANES), 1) < 2 * DIFF_HD

    def max_sq_norm(sq):
        return jnp.max(jnp.sum(sq, axis=1, keepdims=True), axis=0, keepdims=True)

    kn = jnp.zeros((HEADS, LANES), F32)
    for h in range(HEADS):
        kh = kall[:, h * LANES:(h + 1) * LANES]
        ok_mla[0, h] = kh
        khf = kh.astype(F32)
        kn = jnp.where(head_row == h, max_sq_norm(khf * khf), kn)
    okn_mla[0, 0] = kn
    ones_rows = (lax.broadcasted_iota(jnp.int32, (V_ROWS - MLA_V, tm), 0) == 0).astype(BF16)

    def store_values(ov, vals):
        for h in range(HEADS):
            ov[0, h * V_ROWS:h * V_ROWS + MLA_V, :] = vals[h * MLA_V:(h + 1) * MLA_V]
            ov[0, h * V_ROWS + MLA_V:(h + 1) * V_ROWS, :] = ones_rows

    store_values(ov_mla, vT)

    dqT = lax.dot_general(wdqT[...], ub, NT_DIMS, preferred_element_type=F32) * DIFF_SCALE
    a1, a2, b1, b2 = (dqT[i * LANES:(i + 1) * LANES] for i in range(4))
    parts = [(a1 * cT - a2 * sT).astype(BF16), (a1 * sT + a2 * cT).astype(BF16),
             (b1 * cT - b2 * sT).astype(BF16), (b1 * sT + b2 * cT).astype(BF16)]
    zhalf = jnp.zeros((2 * DIFF_HD, tm), BF16)
    for h in range(HEADS):
        base = (h % 2) * 2 * DIFF_HD
        for i, part in enumerate(parts):
            oq_diff[0, h, 0, base + i * hw:base + (i + 1) * hw, :] = part[h * hw:(h + 1) * hw]
        other = 2 * DIFF_HD - base
        oq_diff[0, h, 0, other:other + 2 * DIFF_HD, :] = zhalf

    dk = _dot(ub, wdk[...])
    cN, sN = cosN[...], sinN[...]
    k1a, k1b, k2a, k2b = (dk[:, i * LANES:(i + 1) * LANES] for i in range(4))
    rk = jnp.concatenate([k1a * cN - k1b * sN, k1a * sN + k1b * cN,
                          k2a * cN - k2b * sN, k2a * sN + k2b * cN], axis=1).astype(BF16)
    kd = _dot(rk, perm[...]).astype(BF16)
    kn = jnp.zeros((HEADS, LANES), F32)
    for p in range(HEADS // 2):
        kp = kd[:, p * LANES:(p + 1) * LANES]
        ok_diff[0, p] = kp
        kpf = kp.astype(F32)
        sq = kpf * kpf
        kn = jnp.where(head_row == 2 * p, max_sq_norm(jnp.where(left_half, sq, 0.0)), kn)
        kn = jnp.where(head_row == 2 * p + 1, max_sq_norm(jnp.where(left_half, 0.0, sq)), kn)
    okn_diff[0, 0] = kn
    store_values(ov_diff, lax.dot_general(wdvT[...], ub, NT_DIMS, preferred_element_type=F32).astype(BF16))

    opool[0] = _dot(ub, wpool[...])
    ogates[0] = _dot(ub, wgates[...]).astype(BF16)


def _inproj(xs, mod_l, g, w, tables, *, mod_map):
    b, nt, d = xs.shape
    tile = pl.BlockSpec((1, TM, d), lambda i, t: (i, t, 0))
    tabT = pl.BlockSpec((LANES, TM), lambda i, t: (0, t))
    tabN = pl.BlockSpec((TM, LANES), lambda i, t: (t, 0))
    weights = [w["wcq"], w["wckv"], w["wkr2"], w["wdqT"], w["wdk"], w["wdvT"], w["wpool"], w["wgates"],
               w["qng"], w["kvng"], w["wuqT"], w["wkpad"], w["wvT"], w["place"], w["perm"]]
    vT_spec = pl.BlockSpec((1, HEADS * V_ROWS, TM), lambda i, t: (i, 0, t))

    def q_spec(tq):
        r = tq // TM
        return pl.BlockSpec((1, HEADS, 1, LANES, TM), lambda i, t: (i, 0, t // r, 0, t % r))

    out_shapes = [
        jax.ShapeDtypeStruct((b, HEADS, pl.cdiv(nt, TQ_MLA), LANES, TQ_MLA), BF16),
        jax.ShapeDtypeStruct((b, HEADS, nt, LANES), BF16),
        jax.ShapeDtypeStruct((b, HEADS * V_ROWS, nt), BF16),
        jax.ShapeDtypeStruct((b, HEADS, pl.cdiv(nt, TQ_DIFF), LANES, TQ_DIFF), BF16),
        jax.ShapeDtypeStruct((b, HEADS // 2, nt, LANES), BF16),
        jax.ShapeDtypeStruct((b, HEADS * V_ROWS, nt), BF16),
        jax.ShapeDtypeStruct((b, nt, 4 * POOL_G), F32),
        jax.ShapeDtypeStruct((b, nt, 3 * d), BF16),
        jax.ShapeDtypeStruct((b, nt // TM, HEADS, LANES), F32),
        jax.ShapeDtypeStruct((b, nt // TM, HEADS, LANES), F32),
    ]
    kn_spec = pl.BlockSpec((1, 1, HEADS, LANES), lambda i, t: (i, t, 0, 0))
    out_specs = [
        q_spec(TQ_MLA),
        pl.BlockSpec((1, HEADS, TM, LANES), lambda i, t: (i, 0, t, 0)),
        vT_spec,
        q_spec(TQ_DIFF),
        pl.BlockSpec((1, HEADS // 2, TM, LANES), lambda i, t: (i, 0, t, 0)),
        vT_spec,
        pl.BlockSpec((1, TM, 4 * POOL_G), lambda i, t: (i, t, 0)),
        pl.BlockSpec((1, TM, 3 * d), lambda i, t: (i, t, 0)),
        kn_spec, kn_spec,
    ]
    return pl.pallas_call(
        _inproj_kernel,
        grid=(b, nt // TM),
        in_specs=([tile, pl.BlockSpec((1, N_ADA, d), mod_map), _const_spec((1, d))]
                  + [_const_spec(a.shape) for a in weights]
                  + [tabT, tabT, tabN, tabN, tabN, tabN]),
        out_specs=out_specs,
        out_shape=out_shapes,
        compiler_params=_params(),
        name="inproj",
    )(xs, mod_l, g, *weights, tables["cosT"], tables["sinT"], tables["cosN"], tables["sinN"],
      tables["c128"], tables["s128"])


def _attn_kernel(*refs, chunks, diff, lam_init, aliased):
    refs = list(refs)
    q_ref, k_ref, v_ref, kn_ref = refs[:4]
    n_buf = ATT_AHEAD + 1
    s_bufs = refs[-n_buf:]
    o_ref = refs[-n_buf - 1]
    assert len(refs) == 5 + 2 * diff + aliased + n_buf
    nsub, tq = q_ref.shape[2], q_ref.shape[4]
    nc = len(chunks)
    head = pl.program_id(1)

    tile_max = jnp.max(kn_ref[0], axis=0)
    rows = lax.broadcasted_iota(jnp.int32, tile_max.shape, 0)
    kmax2 = jnp.max(jnp.where(rows == head, tile_max, 0.0), axis=0, keepdims=True)[:, 0:1]

    def finish(num, den, sub):
        o = num / den
        if diff:
            dl_ref, sg_ref = refs[4:6]
            dl = dl_ref[...]
            lam = (jnp.exp(jnp.sum(dl[0:1] * dl[1:2], axis=1, keepdims=True))
                   - jnp.exp(jnp.sum(dl[2:3] * dl[3:4], axis=1, keepdims=True)) + lam_init)
            o = o[:, :tq] - lam * o[:, tq:]
            o = o * lax.rsqrt(jnp.mean(o * o, axis=0, keepdims=True) + EPS)
            o = o * sg_ref[...] * (1.0 - lam_init)
        o_ref[0, sub] = o.astype(o_ref.dtype)

    def bounded_shift(rhs, bound, sub):
        acc = jnp.zeros((MLA_V, rhs.shape[1]), F32)
        den = jnp.zeros((1, rhs.shape[1]), F32)
        logits = lambda c: _dot(k_ref[0, 0, chunks[c][0]:chunks[c][1], :], rhs)
        s_next = logits(0)
        for c, (lo, hi) in enumerate(chunks):
            s = s_next
            if c + 1 < nc:
                s_next = logits(c + 1)
            p = jnp.exp2(s - bound)
            den = den + jnp.sum(p, axis=0, keepdims=True)
            acc = acc + _dot(v_ref[0, 0:MLA_V, lo:hi], p.astype(BF16))
        finish(acc, den, sub)

    def running_max(rhs, sub, zero_row):
        n = rhs.shape[1]

        def scores(c):
            lo, hi = chunks[c]
            s = _dot(k_ref[0, 0, lo:hi, :], rhs)
            s_bufs[c % n_buf][pl.ds(zero_row, hi - lo), :] = s
            return jnp.max(s, axis=0, keepdims=True)

        m = jnp.full((1, n), -1e30, F32)
        acc = jnp.zeros((V_ROWS, n), F32)
        cmaxes = {c: scores(c) for c in range(min(ATT_AHEAD, nc))}
        for c in range(nc):
            lo, hi = chunks[c]
            m_new = jnp.maximum(m, cmaxes.pop(c))
            alpha = jnp.exp2(m - m_new)
            if c + ATT_AHEAD < nc:
                cmaxes[c + ATT_AHEAD] = scores(c + ATT_AHEAD)
            p = jnp.exp2(s_bufs[c % n_buf][pl.ds(zero_row, hi - lo), :] - m_new)
            m, acc = m_new, alpha * acc + _dot(v_ref[0, :, lo:hi], p.astype(BF16))
        finish(acc[0:MLA_V], acc[MLA_V:MLA_V + 1], sub)

    def sub_block(sub, zero_row):
        q = q_ref[0, 0, sub]
        if diff:
            first = (lax.broadcasted_iota(jnp.int32, q.shape, 0) & DIFF_HD) == 0
            zero = jnp.zeros_like(q)
            rhs = jnp.concatenate([jnp.where(first, q, zero), jnp.where(first, zero, q)], axis=1)
        else:
            rhs = q
        qf = rhs.astype(F32)
        bound = jnp.sqrt(jnp.sum(qf * qf, axis=0, keepdims=True) * kmax2)
        small = jnp.max(bound) <= ATT_BOUND_MAX

        @pl.when(small)
        def _():
            bounded_shift(rhs, bound, sub)

        @pl.when(jnp.logical_not(small))
        def _():
            running_max(rhs, sub, zero_row)

    zero_row = pl.multiple_of(jnp.minimum(pl.program_id(2), 0), 16)
    if nsub == 1:
        sub_block(0, zero_row)
    else:
        def body(sub, carry):
            sub_block(sub, zero_row)
            return carry
        lax.fori_loop(0, nsub, body, 0)


def _key_chunks(nk):
    head = [c for c in CHUNK_HEAD if sum(CHUNK_HEAD) + sum(CHUNK_TAIL) + TK <= nk]
    tail = CHUNK_TAIL if head else ()
    sizes = list(head)
    body = nk - sum(head) - sum(tail)
    sizes += [TK] * (body // TK) + ([body % TK] if body % TK else [])
    sizes += list(tail)
    edges = np.cumsum([0] + sizes)
    assert edges[-1] == nk and all(s % LANES == 0 for s in sizes)
    return tuple((int(a), int(b)) for a, b in zip(edges[:-1], edges[1:]))


def _attention(qT, k, vT, kn, *, tq, nsub, steps, q_block0, k_rows, k_block0, chunks,
               diff, extra=(), lam_init=0.0, prev_out=None):
    b, heads, nqb, _, tqw = qT.shape
    dv = MLA_V
    kdiv = 2 if diff else 1
    in_specs = [
        pl.BlockSpec((1, 1, nsub, LANES, tq), lambda i, h, j: (i, h, q_block0 + j, 0, 0)),
        pl.BlockSpec((1, 1, k_rows, LANES), lambda i, h, j: (i, h // kdiv, k_block0, 0)),
        pl.BlockSpec((1, V_ROWS, k_rows), lambda i, h, j: (i, h, k_block0)),
        pl.BlockSpec((1,) + kn.shape[1:], lambda i, h, j: (i, 0, 0, 0)),
    ]
    args = [qT, k, vT, kn]
    if diff:
        in_specs += [pl.BlockSpec(extra[0].shape, lambda i, h, j: (0, 0)),
                     pl.BlockSpec(extra[1].shape, lambda i, h, j: (0, 0))]
        args += list(extra)
    aliases = {}
    if prev_out is not None:
        in_specs.append(pl.BlockSpec(memory_space=pl.ANY))
        args.append(prev_out)
        aliases = {len(args) - 1: 0}
    return pl.pallas_call(
        functools.partial(_attn_kernel, chunks=chunks, diff=diff, lam_init=lam_init,
                          aliased=prev_out is not None),
        grid=(b, heads, steps),
        in_specs=in_specs,
        out_specs=pl.BlockSpec((1, nsub, dv, tq), lambda i, h, j: (i, q_block0 + j, h, 0)),
        out_shape=jax.ShapeDtypeStruct((b, nqb, heads * dv, tqw), BF16),
        scratch_shapes=[pltpu.VMEM((max(hi - lo for lo, hi in chunks), (2 * tq if diff else tq)), dt)
                        for dt in [F32] * (ATT_AHEAD + 1)],
        input_output_aliases=aliases,
        compiler_params=pltpu.CompilerParams(vmem_limit_bytes=VMEM_LIMIT, flags=ATT_FLAGS),
        name=("diff" if diff else "mla") + ("_ctx" if prev_out is not None else "_x"),
    )(*args)


def _mix_kernel(x_ref, mod_ref, oa_ref, od_ref, pc_ref, pp_ref, pn_ref, gt_ref, bg_ref,
                pproj_ref, pb_ref, ps_ref, wa_ref, wd_ref, wp_ref, wo_ref, o_ref, *, x_tiles):
    t = pl.program_id(1)
    x = x_ref[0]
    gate = mod_ref[0][5:6]
    tm, d = x.shape
    a = lax.dot_general(oa_ref[0, 0], wa_ref[...], TN_DIMS, preferred_element_type=F32)
    dd = lax.dot_general(od_ref[0, 0], wd_ref[...], TN_DIMS, preferred_element_type=F32)

    is_ctx = t == x_tiles
    has_prev = jnp.logical_and(t != 0, jnp.logical_not(is_ctx))
    has_next = jnp.logical_and(t != x_tiles - 1, jnp.logical_not(is_ctx))
    cur = pc_ref[0]
    prev = jnp.where(has_prev, pp_ref[0], 0.0)
    nxt = jnp.where(has_next, pn_ref[0], 0.0)
    ext = jnp.concatenate([prev, cur, nxt], axis=0)
    rows = ext.shape[0]
    seq_len = jnp.where(is_ctx, tm, x_tiles * tm)
    pos = jnp.where(is_ctx, 0, t * tm) + lax.broadcasted_iota(jnp.int32, (tm, 1), 0)
    outs = []
    for g, w in enumerate(POOL_WINDOWS):
        lanes = slice(g * POOL_G, (g + 1) * POOL_G)
        run = ext[:, lanes]
        span = 1
        while span < w:
            run = run + pltpu.roll(run, rows - span, axis=0)
            span *= 2
        win = pltpu.roll(run, rows - (POOL_HALO - w // 2), axis=0)[0:tm]
        cnt = (jnp.minimum(pos + w // 2, seq_len) - jnp.maximum(pos - w // 2, 0)).astype(F32)
        pooled = (win / cnt - cur[:, lanes]).astype(BF16)
        outs.append((_dot(pooled, pproj_ref[g]) + pb_ref[:, lanes]) * ps_ref[:, lanes])
    pooled_out = _dot(jnp.concatenate(outs, axis=1).astype(BF16), wp_ref[...])

    gs = jax.nn.sigmoid(gt_ref[0].astype(F32) + bg_ref[...])
    merged = gs[:, 0:d] * a + gs[:, d:2 * d] * dd + gs[:, 2 * d:3 * d] * pooled_out
    o_ref[0] = x + gate * _dot(merged.astype(BF16), wo_ref[...])


def _mix(xs, mod_l, oa, od, pool_in, gates, w, *, mod_map, x_tiles):
    b, nt, d = xs.shape
    hb = TM // POOL_HALO
    n_halo = nt // POOL_HALO
    tile = pl.BlockSpec((1, TM, d), lambda i, t: (i, t, 0))
    pw = 4 * POOL_G
    weights = [w["bgate"], w["pproj"], w["pb"], w["ps"], w["wa"], w["wd"], w["wp"], w["wo"]]

    def o_spec(o):
        r = o.shape[3] // TM
        return pl.BlockSpec((1, 1, o.shape[2], TM), lambda i, t: (i, t // r, 0, t % r))

    return pl.pallas_call(
        functools.partial(_mix_kernel, x_tiles=x_tiles),
        grid=(b, nt // TM),
        in_specs=[tile, pl.BlockSpec((1, N_ADA, d), mod_map),
                  o_spec(oa), o_spec(od),
                  pl.BlockSpec((1, TM, pw), lambda i, t: (i, t, 0)),
                  pl.BlockSpec((1, POOL_HALO, pw), lambda i, t: (i, jnp.maximum(t * hb - 1, 0), 0)),
                  pl.BlockSpec((1, POOL_HALO, pw), lambda i, t: (i, jnp.minimum((t + 1) * hb, n_halo - 1), 0)),
                  pl.BlockSpec((1, TM, 3 * d), lambda i, t: (i, t, 0))]
                 + [_const_spec(a.shape) for a in weights],
        out_specs=tile,
        out_shape=jax.ShapeDtypeStruct(xs.shape, F32),
        input_output_aliases={0: 0},
        compiler_params=_params(),
        name="mix",
    )(xs, mod_l, oa, od, pool_in, pool_in, pool_in, gates, *weights)


def _final_kernel(x_ref, g_ref, o_ref):
    o_ref[0] = _rms(x_ref[0]) * g_ref[...]


def _final_norm(xs, g, seq):
    b, _, d = xs.shape
    tile = pl.BlockSpec((1, TM, d), lambda i, t: (i, t, 0))
    return pl.pallas_call(
        _final_kernel,
        grid=(b, seq // TM),
        in_specs=[tile, _const_spec((1, d))],
        out_specs=tile,
        out_shape=jax.ShapeDtypeStruct((b, seq, d), F32),
        compiler_params=_params(),
        name="final_norm",
    )(xs, g)


def _rope_tables(seq, ctx):
    rows = seq // GRID_W
    row_ids = jnp.repeat(jnp.arange(rows), GRID_W).astype(F32)
    col_ids = jnp.tile(jnp.arange(GRID_W), rows).astype(F32)
    n_freq = ROPE_HALF // 2
    inv_freq = ROPE_THETA ** (-jnp.arange(n_freq, dtype=F32) / n_freq)
    ang = jnp.concatenate([row_ids[:, None] * inv_freq, col_ids[:, None] * inv_freq], axis=-1)
    cos = jnp.concatenate([jnp.cos(ang), jnp.ones((ctx, ROPE_HALF), F32)], axis=0)
    sin = jnp.concatenate([jnp.sin(ang), jnp.zeros((ctx, ROPE_HALF), F32)], axis=0)
    pad = jnp.zeros((seq + ctx, LANES - 2 * ROPE_HALF), F32)
    return {
        "cosT": jnp.tile(cos.T, (HEADS, 1)), "sinT": jnp.tile(sin.T, (HEADS, 1)),
        "cosN": jnp.tile(cos, (1, HEADS)), "sinN": jnp.tile(sin, (1, HEADS)),
        "c128": jnp.concatenate([cos, cos, pad], axis=1),
        "s128": jnp.concatenate([sin, sin, pad], axis=1),
    }


def _index_tables():
    hw = ROPE_HALF
    uq = np.zeros(HEADS * MLA_QK, np.int32)
    for h in range(HEADS):
        for j in range(MLA_NOPE):
            uq[h * MLA_NOPE + j] = h * MLA_QK + j
        for f in range(hw):
            uq[HEADS * MLA_NOPE + h * hw + f] = h * MLA_QK + MLA_NOPE + f
            uq[HEADS * MLA_NOPE + LANES + h * hw + f] = h * MLA_QK + MLA_NOPE + hw + f
    dqk = np.zeros(HEADS * 2 * DIFF_HD, np.int32)
    perm = np.zeros((HEADS * 2 * DIFF_HD, HEADS * 2 * DIFF_HD), np.float32)
    for h in range(HEADS):
        for c in range(2):
            for half in range(2):
                for f in range(hw):
                    src = h * 2 * DIFF_HD + c * DIFF_HD + half * hw + f
                    blk = (c * 2 + half) * LANES + h * hw + f
                    dqk[blk] = src
                    perm[blk, src] = 1.0
    v_cols = np.array([h * (MLA_NOPE + MLA_V) + MLA_NOPE + j for h in range(HEADS) for j in range(MLA_V)], np.int32)
    k_mask = ((np.arange(HEADS * (MLA_NOPE + MLA_V)) % (MLA_NOPE + MLA_V)) < MLA_NOPE).astype(np.float32)
    place = np.zeros((LANES, HEADS * LANES), np.float32)
    for h in range(HEADS):
        for f in range(MLA_ROPE):
            place[f, h * LANES + MLA_NOPE + f] = 1.0
    return uq, dqk, perm, v_cols, k_mask, place


def _layer_weights(l, p, idx):
    uq, dqk, perm, v_cols, k_mask, place = idx
    d = p["w_in"].shape[1]
    w_in = p["w_in"][l]
    o_cq, o_ckv, o_kr = 0, 384, 640
    o_dq, o_dk, o_dv, o_pool, o_gate = 672, 1184, 1696, 2208, 2720
    wkr = w_in[:, o_kr:o_kr + MLA_ROPE]
    wkr_rot = jnp.concatenate([-wkr[:, ROPE_HALF:], wkr[:, :ROPE_HALF]], axis=1)
    zpad = jnp.zeros((d, LANES - MLA_ROPE), F32)
    wkr2 = jnp.concatenate([wkr, zpad, wkr_rot, zpad], axis=1)
    w_ukv = p["mla_w_ukv"][l]
    bf = lambda a: a.astype(BF16)
    return {
        "wcq": bf(w_in[:, o_cq:o_ckv]), "wckv": bf(w_in[:, o_ckv:o_kr]), "wkr2": bf(wkr2),
        "wdqT": bf(w_in[:, o_dq + dqk].T), "wdk": bf(w_in[:, o_dk + dqk]),
        "wdvT": bf(w_in[:, o_dv:o_pool].T), "wpool": bf(w_in[:, o_pool:o_gate]), "wgates": bf(w_in[:, o_gate:]),
        "qng": p["mla_q_norm_g"][l][None], "kvng": p["mla_kv_norm_g"][l][None],
        "wuqT": bf(p["mla_w_uq"][l][:, uq].T), "wkpad": bf(w_ukv * k_mask[None]), "wvT": bf(w_ukv[:, v_cols].T),
        "place": jnp.asarray(place, BF16), "perm": jnp.asarray(perm, BF16),
        "bgate": p["b_gate"][l].reshape(1, -1), "pproj": bf(p["pool_proj"][l]),
        "pb": p["pool_b"][l].reshape(1, -1), "ps": p["pool_scale"][l][None],
        "wa": bf(p["w_br_mla"][l]), "wd": bf(p["w_br_diff"][l]), "wp": bf(p["w_br_pool"][l]), "wo": bf(p["w_out"][l]),
    }


def kernel(x, c, ctx, c_ctx, ada_w, ada_b, norm_g, ffa_w_gate, ffa_w_up, ffa_w_down, ffb_w_gate, ffb_w_up, ffb_w_down, w_in, b_gate, mla_q_norm_g, mla_kv_norm_g, mla_w_uq, mla_w_ukv, diff_lambda, diff_subln_g, pool_proj, pool_b, pool_scale, w_br_mla, w_br_diff, w_br_pool, w_out, final_g):
    b, seq, d = x.shape
    n_ctx = ctx.shape[1]
    depth = ada_w.shape[0]
    nt = seq + n_ctx
    nsub = ATT_NSUB if seq % (ATT_NSUB * TQ_MLA) == 0 else 1
    assert n_ctx == TM and seq % (nsub * TQ_MLA) == 0 and seq % GRID_W == 0 and b + 1 <= 8
    x_tiles = seq // TM
    p = dict(w_in=w_in, b_gate=b_gate, mla_q_norm_g=mla_q_norm_g, mla_kv_norm_g=mla_kv_norm_g,
             mla_w_uq=mla_w_uq, mla_w_ukv=mla_w_ukv, pool_proj=pool_proj, pool_b=pool_b,
             pool_scale=pool_scale, w_br_mla=w_br_mla, w_br_diff=w_br_diff, w_br_pool=w_br_pool, w_out=w_out)

    c_rows = jnp.concatenate([c, c_ctx[None], jnp.zeros((8 - b - 1, d), F32)], axis=0)
    mod = _modulation(c_rows, ada_w, ada_b).reshape(depth, 8, N_ADA, d)
    mod_map = lambda i, t: (jnp.where(t == x_tiles, b, i), 0, 0)

    tables = _rope_tables(seq, n_ctx)
    idx = _index_tables()
    xs = jnp.concatenate([x, ctx], axis=1)
    chunks = _key_chunks(nt)
    bf = lambda a: a.astype(BF16)

    for l in range(depth):
        w = _layer_weights(l, p, idx)
        lam_init = 0.8 - 0.6 * math.exp(-0.3 * l)
        xs = _ffn(xs, mod[l], norm_g[l, 0][None], bf(ffa_w_gate[l]), bf(ffa_w_up[l]), bf(ffa_w_down[l]),
                  mi=0, mod_map=mod_map)
        q_mla, k_mla, v_mla, q_diff, k_diff, v_diff, pool_in, gates, kn_mla, kn_diff = _inproj(
            xs, mod[l], norm_g[l, 1][None], w, tables, mod_map=mod_map)
        mla = (q_mla, k_mla, v_mla, kn_mla)
        dif = (q_diff, k_diff, v_diff, kn_diff)
        extra = (diff_lambda[l], diff_subln_g[l][:, None])
        full = dict(nsub=nsub, q_block0=0, k_rows=nt, k_block0=0, chunks=chunks)
        ctx_only = dict(tq=TM, nsub=1, steps=1, k_rows=TM, k_block0=x_tiles, chunks=((0, TM),))
        oa = _attention(*mla, tq=TQ_MLA, steps=seq // (nsub * TQ_MLA), diff=False, **full)
        oa = _attention(*mla, q_block0=seq // TQ_MLA, diff=False, prev_out=oa, **ctx_only)
        od = _attention(*dif, tq=TQ_DIFF, steps=seq // (nsub * TQ_DIFF), diff=True,
                        extra=extra, lam_init=lam_init, **full)
        od = _attention(*dif, q_block0=seq // TQ_DIFF, diff=True, extra=extra,
                        lam_init=lam_init, prev_out=od, **ctx_only)
        xs = _mix(xs, mod[l], oa, od, pool_in, gates, w, mod_map=mod_map, x_tiles=x_tiles)
        xs = _ffn(xs, mod[l], norm_g[l, 2][None], bf(ffb_w_gate[l]), bf(ffb_w_up[l]), bf(ffb_w_down[l]),
                  mi=2, mod_map=mod_map)

    return _final_norm(xs, final_g[None], seq)
```

```python
import functools
import math

import numpy as np
import jax
import jax.numpy as jnp
from jax import lax
from jax.experimental import pallas as pl
from jax.experimental.pallas import tpu as pltpu

F32 = jnp.float32
BF16 = jnp.bfloat16

EPS = 1e-6
ROPE_THETA = 10000.0
GRID_W = 64
N_ADA = 9

HEADS = 8
MLA_NOPE = 64
MLA_ROPE = 32
MLA_V = 64
V_ROWS = MLA_V + 16
MLA_QK = MLA_NOPE + MLA_ROPE
LOG2E = math.log2(math.e)
MLA_SCALE = MLA_QK ** -0.5 * LOG2E
DIFF_HD = 32
DIFF_V = 64
DIFF_SCALE = DIFF_HD ** -0.5 * LOG2E
ROPE_HALF = 16
POOL_WINDOWS = (2, 4, 8, 16)
POOL_G = 128
POOL_HALO = 16

LANES = 128
TM = 256
TK = 512
CHUNK_HEAD = (256,)
CHUNK_TAIL = ()
ATT_AHEAD = 2
ATT_ROWS = 256
ATT_BOUND_MAX = 40.0
ATT_NSUB = 4
TQ_MLA = 1024
TQ_DIFF = 512
VMEM_LIMIT = 52 * 1024 * 1024

ATT_FLAGS = {}

NT_DIMS = (((1,), (1,)), ((), ()))
TN_DIMS = (((0,), (0,)), ((), ()))


def _params():
    return pltpu.CompilerParams(vmem_limit_bytes=VMEM_LIMIT)


def _const_spec(shape):
    zeros = (0,) * len(shape)
    return pl.BlockSpec(shape, lambda *_: zeros, pipeline_mode=pl.Buffered(1))


def _dot(a, b):
    return jnp.dot(a, b, preferred_element_type=F32)


def _rms(x):
    return x * lax.rsqrt(jnp.mean(x * x, axis=-1, keepdims=True) + EPS)


def _mod_kernel(c_ref, w_ref, b_ref, o_ref):
    c = c_ref[...]
    a = c * jax.nn.sigmoid(c)
    o_ref[0] = jnp.dot(a, w_ref[0], preferred_element_type=F32,
                       precision=lax.Precision.HIGHEST) + b_ref[0]


def _modulation(c_rows, ada_w, ada_b):
    depth, d, n = ada_w.shape
    bn = n // 8
    return pl.pallas_call(
        _mod_kernel,
        grid=(depth, n // bn),
        in_specs=[pl.BlockSpec((8, d), lambda l, j: (0, 0)),
                  pl.BlockSpec((1, d, bn), lambda l, j: (l, 0, j)),
                  pl.BlockSpec((1, 1, bn), lambda l, j: (l, 0, j))],
        out_specs=pl.BlockSpec((1, 8, bn), lambda l, j: (l, 0, j)),
        out_shape=jax.ShapeDtypeStruct((depth, 8, n), F32),
        compiler_params=_params(),
        name="modulation",
    )(c_rows, ada_w, ada_b.reshape(depth, 1, n))


def _ffn_kernel(x_ref, mod_ref, g_ref, wg_ref, wu_ref, wd_ref, o_ref, *, mi, f_chunks):
    x = x_ref[0]
    m = mod_ref[0]
    shift, scale, gate = m[3 * mi:3 * mi + 1], m[3 * mi + 1:3 * mi + 2], m[3 * mi + 2:3 * mi + 3]
    u = (_rms(x) * g_ref[...]) * (1.0 + scale) + shift
    ub = u.astype(BF16)
    y = jnp.zeros(x.shape, F32)
    for lo, hi in f_chunks:
        a = _dot(ub, wg_ref[:, lo:hi])
        b = _dot(ub, wu_ref[:, lo:hi])
        hid = (a * jax.nn.sigmoid(a) * b).astype(BF16)
        y = y + _dot(hid, wd_ref[lo:hi, :])
    o_ref[0] = x + (0.5 * gate) * y


def _ffn(xs, mod_l, g, wg, wu, wd, *, mi, mod_map):
    b, nt, d = xs.shape
    f = wg.shape[1]
    chunk = 1024
    f_chunks = tuple((lo, min(lo + chunk, f)) for lo in range(0, f, chunk))
    tile = pl.BlockSpec((1, TM, d), lambda i, t: (i, t, 0))
    return pl.pallas_call(
        functools.partial(_ffn_kernel, mi=mi, f_chunks=f_chunks),
        grid=(b, nt // TM),
        in_specs=[tile, pl.BlockSpec((1, N_ADA, d), mod_map), _const_spec((1, d)),
                  _const_spec((d, f)), _const_spec((d, f)), _const_spec((f, d))],
        out_specs=tile,
        out_shape=jax.ShapeDtypeStruct(xs.shape, F32),
        input_output_aliases={0: 0},
        compiler_params=_params(),
        name=f"ffn{mi}",
    )(xs, mod_l, g, wg, wu, wd)


def _inproj_kernel(x_ref, mod_ref, g_ref, wcq, wckv, wkr2, wdqT, wdk, wdvT, wpool, wgates,
                   qng, kvng, wuqT, wkpad, wvT, place, perm,
                   cosT, sinT, cosN, sinN, c128, s128,
                   oq_mla, ok_mla, ov_mla, oq_diff, ok_diff, ov_diff, opool, ogates, okn_mla, okn_diff):
    x = x_ref[0]
    m = mod_ref[0]
    u = (_rms(x) * g_ref[...]) * (1.0 + m[4:5]) + m[3:4]
    ub = u.astype(BF16)
    tm = x.shape[0]
    cT, sT = cosT[...], sinT[...]
    hw = ROPE_HALF

    cqn = (_rms(_dot(ub, wcq[...])) * qng[...]).astype(BF16)
    qT = lax.dot_general(wuqT[...], cqn, NT_DIMS, preferred_element_type=F32) * MLA_SCALE
    nope_w = HEADS * MLA_NOPE
    x1, x2 = qT[nope_w:nope_w + LANES], qT[nope_w + LANES:nope_w + 2 * LANES]
    qn = qT[0:nope_w].astype(BF16)
    r1 = (x1 * cT - x2 * sT).astype(BF16)
    r2 = (x1 * sT + x2 * cT).astype(BF16)
    zpad = jnp.zeros((LANES - MLA_QK, tm), BF16)
    for h in range(HEADS):
        oq_mla[0, h, 0, 0:MLA_NOPE, :] = qn[h * MLA_NOPE:(h + 1) * MLA_NOPE]
        oq_mla[0, h, 0, MLA_NOPE:MLA_NOPE + hw, :] = r1[h * hw:(h + 1) * hw]
        oq_mla[0, h, 0, MLA_NOPE + hw:MLA_QK, :] = r2[h * hw:(h + 1) * hw]
        oq_mla[0, h, 0, MLA_QK:LANES, :] = zpad

    ckvn = (_rms(_dot(ub, wckv[...])) * kvng[...]).astype(BF16)
    kr2 = _dot(ub, wkr2[...])
    krr = (kr2[:, 0:LANES] * c128[...] + kr2[:, LANES:2 * LANES] * s128[...]).astype(BF16)
    kall = (_dot(ckvn, wkpad[...]) + _dot(krr, place[...])).astype(BF16)
    vT = lax.dot_general(wvT[...], ckvn, NT_DIMS, preferred_element_type=F32).astype(BF16)
    head_row = lax.broadcasted_iota(jnp.int32, (HEADS, LANES), 0)
    left_half = lax.broadcasted_iota(jnp.int32, (tm, LANES), 1) < 2 * DIFF_HD

    def max_sq_norm(sq):
        return jnp.max(jnp.sum(sq, axis=1, keepdims=True), axis=0, keepdims=True)

    kn = jnp.zeros((HEADS, LANES), F32)
    for h in range(HEADS):
        kh = kall[:, h * LANES:(h + 1) * LANES]
        ok_mla[0, h] = kh
        khf = kh.astype(F32)
        kn = jnp.where(head_row == h, max_sq_norm(khf * khf), kn)
    okn_mla[0, 0] = kn
    ones_rows = (lax.broadcasted_iota(jnp.int32, (V_ROWS - MLA_V, tm), 0) == 0).astype(BF16)

    def store_values(ov, vals):
        for h in range(HEADS):
            ov[0, h * V_ROWS:h * V_ROWS + MLA_V, :] = vals[h * MLA_V:(h + 1) * MLA_V]
            ov[0, h * V_ROWS + MLA_V:(h + 1) * V_ROWS, :] = ones_rows

    store_values(ov_mla, vT)

    dqT = lax.dot_general(wdqT[...], ub, NT_DIMS, preferred_element_type=F32) * DIFF_SCALE
    a1, a2, b1, b2 = (dqT[i * LANES:(i + 1) * LANES] for i in range(4))
    parts = [(a1 * cT - a2 * sT).astype(BF16), (a1 * sT + a2 * cT).astype(BF16),
             (b1 * cT - b2 * sT).astype(BF16), (b1 * sT + b2 * cT).astype(BF16)]
    zhalf = jnp.zeros((2 * DIFF_HD, tm), BF16)
    for h in range(HEADS):
        base = (h % 2) * 2 * DIFF_HD
        for i, part in enumerate(parts):
            oq_diff[0, h, 0, base + i * hw:base + (i + 1) * hw, :] = part[h * hw:(h + 1) * hw]
        other = 2 * DIFF_HD - base
        oq_diff[0, h, 0, other:other + 2 * DIFF_HD, :] = zhalf

    dk = _dot(ub, wdk[...])
    cN, sN = cosN[...], sinN[...]
    k1a, k1b, k2a, k2b = (dk[:, i * LANES:(i + 1) * LANES] for i in range(4))
    rk = jnp.concatenate([k1a * cN - k1b * sN, k1a * sN + k1b * cN,
                          k2a * cN - k2b * sN, k2a * sN + k2b * cN], axis=1).astype(BF16)
    kd = _dot(rk, perm[...]).astype(BF16)
    kn = jnp.zeros((HEADS, LANES), F32)
    for p in range(HEADS // 2):
        kp = kd[:, p * LANES:(p + 1) * LANES]
        ok_diff[0, p] = kp
        kpf = kp.astype(F32)
        sq = kpf * kpf
        kn = jnp.where(head_row == 2 * p, max_sq_norm(jnp.where(left_half, sq, 0.0)), kn)
        kn = jnp.where(head_row == 2 * p + 1, max_sq_norm(jnp.where(left_half, 0.0, sq)), kn)
    okn_diff[0, 0] = kn
    store_values(ov_diff, lax.dot_general(wdvT[...], ub, NT_DIMS, preferred_element_type=F32).astype(BF16))

    opool[0] = _dot(ub, wpool[...])
    ogates[0] = _dot(ub, wgates[...]).astype(BF16)


def _inproj(xs, mod_l, g, w, tables, *, mod_map):
    b, nt, d = xs.shape
    tile = pl.BlockSpec((1, TM, d), lambda i, t: (i, t, 0))
    tabT = pl.BlockSpec((LANES, TM), lambda i, t: (0, t))
    tabN = pl.BlockSpec((TM, LANES), lambda i, t: (t, 0))
    weights = [w["wcq"], w["wckv"], w["wkr2"], w["wdqT"], w["wdk"], w["wdvT"], w["wpool"], w["wgates"],
               w["qng"], w["kvng"], w["wuqT"], w["wkpad"], w["wvT"], w["place"], w["perm"]]
    vT_spec = pl.BlockSpec((1, HEADS * V_ROWS, TM), lambda i, t: (i, 0, t))

    def q_spec(tq):
        r = tq // TM
        return pl.BlockSpec((1, HEADS, 1, LANES, TM), lambda i, t: (i, 0, t // r, 0, t % r))

    out_shapes = [
        jax.ShapeDtypeStruct((b, HEADS, pl.cdiv(nt, TQ_MLA), LANES, TQ_MLA), BF16),
        jax.ShapeDtypeStruct((b, HEADS, nt, LANES), BF16),
        jax.ShapeDtypeStruct((b, HEADS * V_ROWS, nt), BF16),
        jax.ShapeDtypeStruct((b, HEADS, pl.cdiv(nt, TQ_DIFF), LANES, TQ_DIFF), BF16),
        jax.ShapeDtypeStruct((b, HEADS // 2, nt, LANES), BF16),
        jax.ShapeDtypeStruct((b, HEADS * V_ROWS, nt), BF16),
        jax.ShapeDtypeStruct((b, nt, 4 * POOL_G), F32),
        jax.ShapeDtypeStruct((b, nt, 3 * d), BF16),
        jax.ShapeDtypeStruct((b, nt // TM, HEADS, LANES), F32),
        jax.ShapeDtypeStruct((b, nt // TM, HEADS, LANES), F32),
    ]
    kn_spec = pl.BlockSpec((1, 1, HEADS, LANES), lambda i, t: (i, t, 0, 0))
    out_specs = [
        q_spec(TQ_MLA),
        pl.BlockSpec((1, HEADS, TM, LANES), lambda i, t: (i, 0, t, 0)),
        vT_spec,
        q_spec(TQ_DIFF),
        pl.BlockSpec((1, HEADS // 2, TM, LANES), lambda i, t: (i, 0, t, 0)),
        vT_spec,
        pl.BlockSpec((1, TM, 4 * POOL_G), lambda i, t: (i, t, 0)),
        pl.BlockSpec((1, TM, 3 * d), lambda i, t: (i, t, 0)),
        kn_spec, kn_spec,
    ]
    return pl.pallas_call(
        _inproj_kernel,
        grid=(b, nt // TM),
        in_specs=([tile, pl.BlockSpec((1, N_ADA, d), mod_map), _const_spec((1, d))]
                  + [_const_spec(a.shape) for a in weights]
                  + [tabT, tabT, tabN, tabN, tabN, tabN]),
        out_specs=out_specs,
        out_shape=out_shapes,
        compiler_params=_params(),
        name="inproj",
    )(xs, mod_l, g, *weights, tables["cosT"], tables["sinT"], tables["cosN"], tables["sinN"],
      tables["c128"], tables["s128"])


def _attn_kernel(*refs, chunks, diff, lam_init, aliased):
    refs = list(refs)
    q_ref, k_ref, v_ref, kn_ref = refs[:4]
    n_buf = ATT_AHEAD + 1
    s_bufs = refs[-n_buf:]
    o_ref = refs[-n_buf - 1]
    assert len(refs) == 5 + 2 * diff + aliased + n_buf
    nsub, tq = q_ref.shape[2], q_ref.shape[4]
    nc = len(chunks)
    head = pl.program_id(1)

    tile_max = jnp.max(kn_ref[0], axis=0)
    rows = lax.broadcasted_iota(jnp.int32, tile_max.shape, 0)
    kmax2 = jnp.max(jnp.where(rows == head, tile_max, 0.0), axis=0, keepdims=True)[:, 0:1]

    def finish(num, den, sub):
        o = num / den
        if diff:
            dl_ref, sg_ref = refs[4:6]
            dl = dl_ref[...]
            lam = (jnp.exp(jnp.sum(dl[0:1] * dl[1:2], axis=1, keepdims=True))
                   - jnp.exp(jnp.sum(dl[2:3] * dl[3:4], axis=1, keepdims=True)) + lam_init)
            o = o[:, :tq] - lam * o[:, tq:]
            o = o * lax.rsqrt(jnp.mean(o * o, axis=0, keepdims=True) + EPS)
            o = o * sg_ref[...] * (1.0 - lam_init)
        o_ref[0, sub] = o.astype(o_ref.dtype)

    def bounded_shift(rhs, bound, sub):
        acc = jnp.zeros((MLA_V, rhs.shape[1]), F32)
        den = jnp.zeros((1, rhs.shape[1]), F32)
        pieces = lambda c: [(lo, min(lo + ATT_ROWS, chunks[c][1])) for lo in range(*chunks[c], ATT_ROWS)]
        logits = lambda lo, hi: _dot(k_ref[0, 0, lo:hi, :], rhs)
        s_next = [logits(lo, hi) for lo, hi in pieces(0)]
        for c in range(nc):
            s_cur, s_next = s_next, []
            nxt = pieces(c + 1) if c + 1 < nc else []
            for i, (lo, hi) in enumerate(pieces(c)):
                if i < len(nxt):
                    s_next.append(logits(*nxt[i]))
                p = jnp.exp2(s_cur[i] - bound)
                den = den + jnp.sum(p, axis=0, keepdims=True)
                acc = acc + _dot(v_ref[0, 0:MLA_V, lo:hi], p.astype(BF16))
            s_next += [logits(lo, hi) for lo, hi in nxt[len(pieces(c)):]]
        finish(acc, den, sub)

    def running_max(rhs, sub, zero_row):
        n = rhs.shape[1]

        def scores(c):
            lo, hi = chunks[c]
            s = _dot(k_ref[0, 0, lo:hi, :], rhs)
            s_bufs[c % n_buf][pl.ds(zero_row, hi - lo), :] = s
            return jnp.max(s, axis=0, keepdims=True)

        m = jnp.full((1, n), -1e30, F32)
        acc = jnp.zeros((V_ROWS, n), F32)
        cmaxes = {c: scores(c) for c in range(min(ATT_AHEAD, nc))}
        for c in range(nc):
            lo, hi = chunks[c]
            m_new = jnp.maximum(m, cmaxes.pop(c))
            alpha = jnp.exp2(m - m_new)
            if c + ATT_AHEAD < nc:
                cmaxes[c + ATT_AHEAD] = scores(c + ATT_AHEAD)
            p = jnp.exp2(s_bufs[c % n_buf][pl.ds(zero_row, hi - lo), :] - m_new)
            m, acc = m_new, alpha * acc + _dot(v_ref[0, :, lo:hi], p.astype(BF16))
        finish(acc[0:MLA_V], acc[MLA_V:MLA_V + 1], sub)

    def sub_block(sub, zero_row):
        q = q_ref[0, 0, sub]
        if diff:
            first = (lax.broadcasted_iota(jnp.int32, q.shape, 0) & DIFF_HD) == 0
            zero = jnp.zeros_like(q)
            rhs = jnp.concatenate([jnp.where(first, q, zero), jnp.where(first, zero, q)], axis=1)
        else:
            rhs = q
        qf = rhs.astype(F32)
        bound = jnp.sqrt(jnp.sum(qf * qf, axis=0, keepdims=True) * kmax2)
        small = jnp.max(bound) <= ATT_BOUND_MAX

        @pl.when(small)
        def _():
            bounded_shift(rhs, bound, sub)

        @pl.when(jnp.logical_not(small))
        def _():
            running_max(rhs, sub, zero_row)

    zero_row = pl.multiple_of(jnp.minimum(pl.program_id(2), 0), 16)
    if nsub == 1:
        sub_block(0, zero_row)
    else:
        def body(sub, carry):
            sub_block(sub, zero_row)
            return carry
        lax.fori_loop(0, nsub, body, 0)


def _key_chunks(nk):
    head = [c for c in CHUNK_HEAD if sum(CHUNK_HEAD) + sum(CHUNK_TAIL) + TK <= nk]
    tail = CHUNK_TAIL if head else ()
    sizes = list(head)
    body = nk - sum(head) - sum(tail)
    sizes += [TK] * (body // TK) + ([body % TK] if body % TK else [])
    sizes += list(tail)
    edges = np.cumsum([0] + sizes)
    assert edges[-1] == nk and all(s % LANES == 0 for s in sizes)
    return tuple((int(a), int(b)) for a, b in zip(edges[:-1], edges[1:]))


def _attention(qT, k, vT, kn, *, tq, nsub, steps, q_block0, k_rows, k_block0, chunks,
               diff, extra=(), lam_init=0.0, prev_out=None):
    b, heads, nqb, _, tqw = qT.shape
    dv = MLA_V
    kdiv = 2 if diff else 1
    in_specs = [
        pl.BlockSpec((1, 1, nsub, LANES, tq), lambda i, h, j: (i, h, q_block0 + j, 0, 0)),
        pl.BlockSpec((1, 1, k_rows, LANES), lambda i, h, j: (i, h // kdiv, k_block0, 0)),
        pl.BlockSpec((1, V_ROWS, k_rows), lambda i, h, j: (i, h, k_block0)),
        pl.BlockSpec((1,) + kn.shape[1:], lambda i, h, j: (i, 0, 0, 0)),
    ]
    args = [qT, k, vT, kn]
    if diff:
        in_specs += [pl.BlockSpec(extra[0].shape, lambda i, h, j: (0, 0)),
                     pl.BlockSpec(extra[1].shape, lambda i, h, j: (0, 0))]
        args += list(extra)
    aliases = {}
    if prev_out is not None:
        in_specs.append(pl.BlockSpec(memory_space=pl.ANY))
        args.append(prev_out)
        aliases = {len(args) - 1: 0}
    return pl.pallas_call(
        functools.partial(_attn_kernel, chunks=chunks, diff=diff, lam_init=lam_init,
                          aliased=prev_out is not None),
        grid=(b, heads, steps),
        in_specs=in_specs,
        out_specs=pl.BlockSpec((1, nsub, dv, tq), lambda i, h, j: (i, q_block0 + j, h, 0)),
        out_shape=jax.ShapeDtypeStruct((b, nqb, heads * dv, tqw), BF16),
        scratch_shapes=[pltpu.VMEM((max(hi - lo for lo, hi in chunks), (2 * tq if diff else tq)), dt)
                        for dt in [F32] * (ATT_AHEAD + 1)],
        input_output_aliases=aliases,
        compiler_params=pltpu.CompilerParams(vmem_limit_bytes=VMEM_LIMIT, flags=ATT_FLAGS),
        name=("diff" if diff else "mla") + ("_ctx" if prev_out is not None else "_x"),
    )(*args)


def _mix_kernel(x_ref, mod_ref, oa_ref, od_ref, pc_ref, pp_ref, pn_ref, gt_ref, bg_ref,
                pproj_ref, pb_ref, ps_ref, wa_ref, wd_ref, wp_ref, wo_ref, o_ref, *, x_tiles):
    t = pl.program_id(1)
    x = x_ref[0]
    gate = mod_ref[0][5:6]
    tm, d = x.shape
    a = lax.dot_general(oa_ref[0, 0], wa_ref[...], TN_DIMS, preferred_element_type=F32)
    dd = lax.dot_general(od_ref[0, 0], wd_ref[...], TN_DIMS, preferred_element_type=F32)

    is_ctx = t == x_tiles
    has_prev = jnp.logical_and(t != 0, jnp.logical_not(is_ctx))
    has_next = jnp.logical_and(t != x_tiles - 1, jnp.logical_not(is_ctx))
    cur = pc_ref[0]
    prev = jnp.where(has_prev, pp_ref[0], 0.0)
    nxt = jnp.where(has_next, pn_ref[0], 0.0)
    ext = jnp.concatenate([prev, cur, nxt], axis=0)
    rows = ext.shape[0]
    seq_len = jnp.where(is_ctx, tm, x_tiles * tm)
    pos = jnp.where(is_ctx, 0, t * tm) + lax.broadcasted_iota(jnp.int32, (tm, 1), 0)
    outs = []
    for g, w in enumerate(POOL_WINDOWS):
        lanes = slice(g * POOL_G, (g + 1) * POOL_G)
        run = ext[:, lanes]
        span = 1
        while span < w:
            run = run + pltpu.roll(run, rows - span, axis=0)
            span *= 2
        win = pltpu.roll(run, rows - (POOL_HALO - w // 2), axis=0)[0:tm]
        cnt = (jnp.minimum(pos + w // 2, seq_len) - jnp.maximum(pos - w // 2, 0)).astype(F32)
        pooled = (win / cnt - cur[:, lanes]).astype(BF16)
        outs.append((_dot(pooled, pproj_ref[g]) + pb_ref[:, lanes]) * ps_ref[:, lanes])
    pooled_out = _dot(jnp.concatenate(outs, axis=1).astype(BF16), wp_ref[...])

    gs = jax.nn.sigmoid(gt_ref[0].astype(F32) + bg_ref[...])
    merged = gs[:, 0:d] * a + gs[:, d:2 * d] * dd + gs[:, 2 * d:3 * d] * pooled_out
    o_ref[0] = x + gate * _dot(merged.astype(BF16), wo_ref[...])


def _mix(xs, mod_l, oa, od, pool_in, gates, w, *, mod_map, x_tiles):
    b, nt, d = xs.shape
    hb = TM // POOL_HALO
    n_halo = nt // POOL_HALO
    tile = pl.BlockSpec((1, TM, d), lambda i, t: (i, t, 0))
    pw = 4 * POOL_G
    weights = [w["bgate"], w["pproj"], w["pb"], w["ps"], w["wa"], w["wd"], w["wp"], w["wo"]]

    def o_spec(o):
        r = o.shape[3] // TM
        return pl.BlockSpec((1, 1, o.shape[2], TM), lambda i, t: (i, t // r, 0, t % r))

    return pl.pallas_call(
        functools.partial(_mix_kernel, x_tiles=x_tiles),
        grid=(b, nt // TM),
        in_specs=[tile, pl.BlockSpec((1, N_ADA, d), mod_map),
                  o_spec(oa), o_spec(od),
                  pl.BlockSpec((1, TM, pw), lambda i, t: (i, t, 0)),
                  pl.BlockSpec((1, POOL_HALO, pw), lambda i, t: (i, jnp.maximum(t * hb - 1, 0), 0)),
                  pl.BlockSpec((1, POOL_HALO, pw), lambda i, t: (i, jnp.minimum((t + 1) * hb, n_halo - 1), 0)),
                  pl.BlockSpec((1, TM, 3 * d), lambda i, t: (i, t, 0))]
                 + [_const_spec(a.shape) for a in weights],
        out_specs=tile,
        out_shape=jax.ShapeDtypeStruct(xs.shape, F32),
        input_output_aliases={0: 0},
        compiler_params=_params(),
        name="mix",
    )(xs, mod_l, oa, od, pool_in, pool_in, pool_in, gates, *weights)


def _final_kernel(x_ref, g_ref, o_ref):
    o_ref[0] = _rms(x_ref[0]) * g_ref[...]


def _final_norm(xs, g, seq):
    b, _, d = xs.shape
    tile = pl.BlockSpec((1, TM, d), lambda i, t: (i, t, 0))
    return pl.pallas_call(
        _final_kernel,
        grid=(b, seq // TM),
        in_specs=[tile, _const_spec((1, d))],
        out_specs=tile,
        out_shape=jax.ShapeDtypeStruct((b, seq, d), F32),
        compiler_params=_params(),
        name="final_norm",
    )(xs, g)


def _rope_tables(seq, ctx):
    rows = seq // GRID_W
    row_ids = jnp.repeat(jnp.arange(rows), GRID_W).astype(F32)
    col_ids = jnp.tile(jnp.arange(GRID_W), rows).astype(F32)
    n_freq = ROPE_HALF // 2
    inv_freq = ROPE_THETA ** (-jnp.arange(n_freq, dtype=F32) / n_freq)
    ang = jnp.concatenate([row_ids[:, None] * inv_freq, col_ids[:, None] * inv_freq], axis=-1)
    cos = jnp.concatenate([jnp.cos(ang), jnp.ones((ctx, ROPE_HALF), F32)], axis=0)
    sin = jnp.concatenate([jnp.sin(ang), jnp.zeros((ctx, ROPE_HALF), F32)], axis=0)
    pad = jnp.zeros((seq + ctx, LANES - 2 * ROPE_HALF), F32)
    return {
        "cosT": jnp.tile(cos.T, (HEADS, 1)), "sinT": jnp.tile(sin.T, (HEADS, 1)),
        "cosN": jnp.tile(cos, (1, HEADS)), "sinN": jnp.tile(sin, (1, HEADS)),
        "c128": jnp.concatenate([cos, cos, pad], axis=1),
        "s128": jnp.concatenate([sin, sin, pad], axis=1),
    }


def _index_tables():
    hw = ROPE_HALF
    uq = np.zeros(HEADS * MLA_QK, np.int32)
    for h in range(HEADS):
        for j in range(MLA_NOPE):
            uq[h * MLA_NOPE + j] = h * MLA_QK + j
        for f in range(hw):
            uq[HEADS * MLA_NOPE + h * hw + f] = h * MLA_QK + MLA_NOPE + f
            uq[HEADS * MLA_NOPE + LANES + h * hw + f] = h * MLA_QK + MLA_NOPE + hw + f
    dqk = np.zeros(HEADS * 2 * DIFF_HD, np.int32)
    perm = np.zeros((HEADS * 2 * DIFF_HD, HEADS * 2 * DIFF_HD), np.float32)
    for h in range(HEADS):
        for c in range(2):
            for half in range(2):
                for f in range(hw):
                    src = h * 2 * DIFF_HD + c * DIFF_HD + half * hw + f
                    blk = (c * 2 + half) * LANES + h * hw + f
                    dqk[blk] = src
                    perm[blk, src] = 1.0
    v_cols = np.array([h * (MLA_NOPE + MLA_V) + MLA_NOPE + j for h in range(HEADS) for j in range(MLA_V)], np.int32)
    k_mask = ((np.arange(HEADS * (MLA_NOPE + MLA_V)) % (MLA_NOPE + MLA_V)) < MLA_NOPE).astype(np.float32)
    place = np.zeros((LANES, HEADS * LANES), np.float32)
    for h in range(HEADS):
        for f in range(MLA_ROPE):
            place[f, h * LANES + MLA_NOPE + f] = 1.0
    return uq, dqk, perm, v_cols, k_mask, place


def _layer_weights(l, p, idx):
    uq, dqk, perm, v_cols, k_mask, place = idx
    d = p["w_in"].shape[1]
    w_in = p["w_in"][l]
    o_cq, o_ckv, o_kr = 0, 384, 640
    o_dq, o_dk, o_dv, o_pool, o_gate = 672, 1184, 1696, 2208, 2720
    wkr = w_in[:, o_kr:o_kr + MLA_ROPE]
    wkr_rot = jnp.concatenate([-wkr[:, ROPE_HALF:], wkr[:, :ROPE_HALF]], axis=1)
    zpad = jnp.zeros((d, LANES - MLA_ROPE), F32)
    wkr2 = jnp.concatenate([wkr, zpad, wkr_rot, zpad], axis=1)
    w_ukv = p["mla_w_ukv"][l]
    bf = lambda a: a.astype(BF16)
    return {
        "wcq": bf(w_in[:, o_cq:o_ckv]), "wckv": bf(w_in[:, o_ckv:o_kr]), "wkr2": bf(wkr2),
        "wdqT": bf(w_in[:, o_dq + dqk].T), "wdk": bf(w_in[:, o_dk + dqk]),
        "wdvT": bf(w_in[:, o_dv:o_pool].T), "wpool": bf(w_in[:, o_pool:o_gate]), "wgates": bf(w_in[:, o_gate:]),
        "qng": p["mla_q_norm_g"][l][None], "kvng": p["mla_kv_norm_g"][l][None],
        "wuqT": bf(p["mla_w_uq"][l][:, uq].T), "wkpad": bf(w_ukv * k_mask[None]), "wvT": bf(w_ukv[:, v_cols].T),
        "place": jnp.asarray(place, BF16), "perm": jnp.asarray(perm, BF16),
        "bgate": p["b_gate"][l].reshape(1, -1), "pproj": bf(p["pool_proj"][l]),
        "pb": p["pool_b"][l].reshape(1, -1), "ps": p["pool_scale"][l][None],
        "wa": bf(p["w_br_mla"][l]), "wd": bf(p["w_br_diff"][l]), "wp": bf(p["w_br_pool"][l]), "wo": bf(p["w_out"][l]),
    }


def kernel(x, c, ctx, c_ctx, ada_w, ada_b, norm_g, ffa_w_gate, ffa_w_up, ffa_w_down, ffb_w_gate, ffb_w_up, ffb_w_down, w_in, b_gate, mla_q_norm_g, mla_kv_norm_g, mla_w_uq, mla_w_ukv, diff_lambda, diff_subln_g, pool_proj, pool_b, pool_scale, w_br_mla, w_br_diff, w_br_pool, w_out, final_g):
    b, seq, d = x.shape
    n_ctx = ctx.shape[1]
    depth = ada_w.shape[0]
    nt = seq + n_ctx
    nsub = ATT_NSUB if seq % (ATT_NSUB * TQ_MLA) == 0 else 1
    assert n_ctx == TM and seq % (nsub * TQ_MLA) == 0 and seq % GRID_W == 0 and b + 1 <= 8
    x_tiles = seq // TM
    p = dict(w_in=w_in, b_gate=b_gate, mla_q_norm_g=mla_q_norm_g, mla_kv_norm_g=mla_kv_norm_g,
             mla_w_uq=mla_w_uq, mla_w_ukv=mla_w_ukv, pool_proj=pool_proj, pool_b=pool_b,
             pool_scale=pool_scale, w_br_mla=w_br_mla, w_br_diff=w_br_diff, w_br_pool=w_br_pool, w_out=w_out)

    c_rows = jnp.concatenate([c, c_ctx[None], jnp.zeros((8 - b - 1, d), F32)], axis=0)
    mod = _modulation(c_rows, ada_w, ada_b).reshape(depth, 8, N_ADA, d)
    mod_map = lambda i, t: (jnp.where(t == x_tiles, b, i), 0, 0)

    tables = _rope_tables(seq, n_ctx)
    idx = _index_tables()
    xs = jnp.concatenate([x, ctx], axis=1)
    chunks = _key_chunks(nt)
    bf = lambda a: a.astype(BF16)

    for l in range(depth):
        w = _layer_weights(l, p, idx)
        lam_init = 0.8 - 0.6 * math.exp(-0.3 * l)
        xs = _ffn(xs, mod[l], norm_g[l, 0][None], bf(ffa_w_gate[l]), bf(ffa_w_up[l]), bf(ffa_w_down[l]),
                  mi=0, mod_map=mod_map)
        q_mla, k_mla, v_mla, q_diff, k_diff, v_diff, pool_in, gates, kn_mla, kn_diff = _inproj(
            xs, mod[l], norm_g[l, 1][None], w, tables, mod_map=mod_map)
        mla = (q_mla, k_mla, v_mla, kn_mla)
        dif = (q_diff, k_diff, v_diff, kn_diff)
        extra = (diff_lambda[l], diff_subln_g[l][:, None])
        full = dict(nsub=nsub, q_block0=0, k_rows=nt, k_block0=0, chunks=chunks)
        ctx_only = dict(tq=TM, nsub=1, steps=1, k_rows=TM, k_block0=x_tiles, chunks=((0, TM),))
        oa = _attention(*mla, tq=TQ_MLA, steps=seq // (nsub * TQ_MLA), diff=False, **full)
        oa = _attention(*mla, q_block0=seq // TQ_MLA, diff=False, prev_out=oa, **ctx_only)
        od = _attention(*dif, tq=TQ_DIFF, steps=seq // (nsub * TQ_DIFF), diff=True,
                        extra=extra, lam_init=lam_init, **full)
        od = _attention(*dif, q_block0=seq // TQ_DIFF, diff=True, extra=extra,
                        lam_init=lam_init, prev_out=od, **ctx_only)
        xs = _mix(xs, mod[l], oa, od, pool_in, gates, w, mod_map=mod_map, x_tiles=x_tiles)
        xs = _ffn(xs, mod[l], norm_g[l, 2][None], bf(ffb_w_gate[l]), bf(ffb_w_up[l]), bf(ffb_w_down[l]),
                  mi=2, mod_map=mod_map)

    return _final_norm(xs, final_g[None], seq)
```

```python
import functools
import math

import numpy as np
import jax
import jax.numpy as jnp
from jax import lax
from jax.experimental import pallas as pl
from jax.experimental.pallas import tpu as pltpu

F32 = jnp.float32
BF16 = jnp.bfloat16

EPS = 1e-6
ROPE_THETA = 10000.0
GRID_W = 64
N_ADA = 9

HEADS = 8
MLA_NOPE = 64
MLA_ROPE = 32
MLA_V = 64
V_ROWS = MLA_V + 16
MLA_QK = MLA_NOPE + MLA_ROPE
LOG2E = math.log2(math.e)
MLA_SCALE = MLA_QK ** -0.5 * LOG2E
DIFF_HD = 32
DIFF_V = 64
DIFF_SCALE = DIFF_HD ** -0.5 * LOG2E
ROPE_HALF = 16
POOL_WINDOWS = (2, 4, 8, 16)
POOL_G = 128
POOL_HALO = 16

LANES = 128
TM = 256
TK = 512
CHUNK_HEAD = (256,)
CHUNK_TAIL = ()
ATT_AHEAD = 2
ATT_ROWS = 256
ATT_BOUND_MAX = 0.0
ATT_NSUB = 4
TQ_MLA = 1024
TQ_DIFF = 512
VMEM_LIMIT = 52 * 1024 * 1024

ATT_FLAGS = {}

NT_DIMS = (((1,), (1,)), ((), ()))
TN_DIMS = (((0,), (0,)), ((), ()))


def _params():
    return pltpu.CompilerParams(vmem_limit_bytes=VMEM_LIMIT)


def _const_spec(shape):
    zeros = (0,) * len(shape)
    return pl.BlockSpec(shape, lambda *_: zeros, pipeline_mode=pl.Buffered(1))


def _dot(a, b):
    return jnp.dot(a, b, preferred_element_type=F32)


def _rms(x):
    return x * lax.rsqrt(jnp.mean(x * x, axis=-1, keepdims=True) + EPS)


def _mod_kernel(c_ref, w_ref, b_ref, o_ref):
    c = c_ref[...]
    a = c * jax.nn.sigmoid(c)
    o_ref[0] = jnp.dot(a, w_ref[0], preferred_element_type=F32,
                       precision=lax.Precision.HIGHEST) + b_ref[0]


def _modulation(c_rows, ada_w, ada_b):
    depth, d, n = ada_w.shape
    bn = n // 8
    return pl.pallas_call(
        _mod_kernel,
        grid=(depth, n // bn),
        in_specs=[pl.BlockSpec((8, d), lambda l, j: (0, 0)),
                  pl.BlockSpec((1, d, bn), lambda l, j: (l, 0, j)),
                  pl.BlockSpec((1, 1, bn), lambda l, j: (l, 0, j))],
        out_specs=pl.BlockSpec((1, 8, bn), lambda l, j: (l, 0, j)),
        out_shape=jax.ShapeDtypeStruct((depth, 8, n), F32),
        compiler_params=_params(),
        name="modulation",
    )(c_rows, ada_w, ada_b.reshape(depth, 1, n))


def _ffn_kernel(x_ref, mod_ref, g_ref, wg_ref, wu_ref, wd_ref, o_ref, *, mi, f_chunks):
    x = x_ref[0]
    m = mod_ref[0]
    shift, scale, gate = m[3 * mi:3 * mi + 1], m[3 * mi + 1:3 * mi + 2], m[3 * mi + 2:3 * mi + 3]
    u = (_rms(x) * g_ref[...]) * (1.0 + scale) + shift
    ub = u.astype(BF16)
    y = jnp.zeros(x.shape, F32)
    for lo, hi in f_chunks:
        a = _dot(ub, wg_ref[:, lo:hi])
        b = _dot(ub, wu_ref[:, lo:hi])
        hid = (a * jax.nn.sigmoid(a) * b).astype(BF16)
        y = y + _dot(hid, wd_ref[lo:hi, :])
    o_ref[0] = x + (0.5 * gate) * y


def _ffn(xs, mod_l, g, wg, wu, wd, *, mi, mod_map):
    b, nt, d = xs.shape
    f = wg.shape[1]
    chunk = 1024
    f_chunks = tuple((lo, min(lo + chunk, f)) for lo in range(0, f, chunk))
    tile = pl.BlockSpec((1, TM, d), lambda i, t: (i, t, 0))
    return pl.pallas_call(
        functools.partial(_ffn_kernel, mi=mi, f_chunks=f_chunks),
        grid=(b, nt // TM),
        in_specs=[tile, pl.BlockSpec((1, N_ADA, d), mod_map), _const_spec((1, d)),
                  _const_spec((d, f)), _const_spec((d, f)), _const_spec((f, d))],
        out_specs=tile,
        out_shape=jax.ShapeDtypeStruct(xs.shape, F32),
        input_output_aliases={0: 0},
        compiler_params=_params(),
        name=f"ffn{mi}",
    )(xs, mod_l, g, wg, wu, wd)


def _inproj_kernel(x_ref, mod_ref, g_ref, wcq, wckv, wkr2, wdqT, wdk, wdvT, wpool, wgates,
                   qng, kvng, wuqT, wkpad, wvT, place, perm,
                   cosT, sinT, cosN, sinN, c128, s128,
                   oq_mla, ok_mla, ov_mla, oq_diff, ok_diff, ov_diff, opool, ogates, okn_mla, okn_diff):
    x = x_ref[0]
    m = mod_ref[0]
    u = (_rms(x) * g_ref[...]) * (1.0 + m[4:5]) + m[3:4]
    ub = u.astype(BF16)
    tm = x.shape[0]
    cT, sT = cosT[...], sinT[...]
    hw = ROPE_HALF

    cqn = (_rms(_dot(ub, wcq[...])) * qng[...]).astype(BF16)
    qT = lax.dot_general(wuqT[...], cqn, NT_DIMS, preferred_element_type=F32) * MLA_SCALE
    nope_w = HEADS * MLA_NOPE
    x1, x2 = qT[nope_w:nope_w + LANES], qT[nope_w + LANES:nope_w + 2 * LANES]
    qn = qT[0:nope_w].astype(BF16)
    r1 = (x1 * cT - x2 * sT).astype(BF16)
    r2 = (x1 * sT + x2 * cT).astype(BF16)
    zpad = jnp.zeros((LANES - MLA_QK, tm), BF16)
    for h in range(HEADS):
        oq_mla[0, h, 0, 0:MLA_NOPE, :] = qn[h * MLA_NOPE:(h + 1) * MLA_NOPE]
        oq_mla[0, h, 0, MLA_NOPE:MLA_NOPE + hw, :] = r1[h * hw:(h + 1) * hw]
        oq_mla[0, h, 0, MLA_NOPE + hw:MLA_QK, :] = r2[h * hw:(h + 1) * hw]
        oq_mla[0, h, 0, MLA_QK:LANES, :] = zpad

    ckvn = (_rms(_dot(ub, wckv[...])) * kvng[...]).astype(BF16)
    kr2 = _dot(ub, wkr2[...])
    krr = (kr2[:, 0:LANES] * c128[...] + kr2[:, LANES:2 * LANES] * s128[...]).astype(BF16)
    kall = (_dot(ckvn, wkpad[...]) + _dot(krr, place[...])).astype(BF16)
    vT = lax.dot_general(wvT[...], ckvn, NT_DIMS, preferred_element_type=F32).astype(BF16)
    head_row = lax.broadcasted_iota(jnp.int32, (HEADS, LANES), 0)
    left_half = lax.broadcasted_iota(jnp.int32, (tm, LANES), 1) < 2 * DIFF_HD

    def max_sq_norm(sq):
        return jnp.max(jnp.sum(sq, axis=1, keepdims=True), axis=0, keepdims=True)

    kn = jnp.zeros((HEADS, LANES), F32)
    for h in range(HEADS):
        kh = kall[:, h * LANES:(h + 1) * LANES]
        ok_mla[0, h] = kh
        khf = kh.astype(F32)
        kn = jnp.where(head_row == h, max_sq_norm(khf * khf), kn)
    okn_mla[0, 0] = kn
    ones_rows = (lax.broadcasted_iota(jnp.int32, (V_ROWS - MLA_V, tm), 0) == 0).astype(BF16)

    def store_values(ov, vals):
        for h in range(HEADS):
            ov[0, h * V_ROWS:h * V_ROWS + MLA_V, :] = vals[h * MLA_V:(h + 1) * MLA_V]
            ov[0, h * V_ROWS + MLA_V:(h + 1) * V_ROWS, :] = ones_rows

    store_values(ov_mla, vT)

    dqT = lax.dot_general(wdqT[...], ub, NT_DIMS, preferred_element_type=F32) * DIFF_SCALE
    a1, a2, b1, b2 = (dqT[i * LANES:(i + 1) * LANES] for i in range(4))
    parts = [(a1 * cT - a2 * sT).astype(BF16), (a1 * sT + a2 * cT).astype(BF16),
             (b1 * cT - b2 * sT).astype(BF16), (b1 * sT + b2 * cT).astype(BF16)]
    zhalf = jnp.zeros((2 * DIFF_HD, tm), BF16)
    for h in range(HEADS):
        base = (h % 2) * 2 * DIFF_HD
        for i, part in enumerate(parts):
            oq_diff[0, h, 0, base + i * hw:base + (i + 1) * hw, :] = part[h * hw:(h + 1) * hw]
        other = 2 * DIFF_HD - base
        oq_diff[0, h, 0, other:other + 2 * DIFF_HD, :] = zhalf

    dk = _dot(ub, wdk[...])
    cN, sN = cosN[...], sinN[...]
    k1a, k1b, k2a, k2b = (dk[:, i * LANES:(i + 1) * LANES] for i in range(4))
    rk = jnp.concatenate([k1a * cN - k1b * sN, k1a * sN + k1b * cN,
                          k2a * cN - k2b * sN, k2a * sN + k2b * cN], axis=1).astype(BF16)
    kd = _dot(rk, perm[...]).astype(BF16)
    kn = jnp.zeros((HEADS, LANES), F32)
    for p in range(HEADS // 2):
        kp = kd[:, p * LANES:(p + 1) * LANES]
        ok_diff[0, p] = kp
        kpf = kp.astype(F32)
        sq = kpf * kpf
        kn = jnp.where(head_row == 2 * p, max_sq_norm(jnp.where(left_half, sq, 0.0)), kn)
        kn = jnp.where(head_row == 2 * p + 1, max_sq_norm(jnp.where(left_half, 0.0, sq)), kn)
    okn_diff[0, 0] = kn
    store_values(ov_diff, lax.dot_general(wdvT[...], ub, NT_DIMS, preferred_element_type=F32).astype(BF16))

    opool[0] = _dot(ub, wpool[...])
    ogates[0] = _dot(ub, wgates[...]).astype(BF16)


def _inproj(xs, mod_l, g, w, tables, *, mod_map):
    b, nt, d = xs.shape
    tile = pl.BlockSpec((1, TM, d), lambda i, t: (i, t, 0))
    tabT = pl.BlockSpec((LANES, TM), lambda i, t: (0, t))
    tabN = pl.BlockSpec((TM, LANES), lambda i, t: (t, 0))
    weights = [w["wcq"], w["wckv"], w["wkr2"], w["wdqT"], w["wdk"], w["wdvT"], w["wpool"], w["wgates"],
               w["qng"], w["kvng"], w["wuqT"], w["wkpad"], w["wvT"], w["place"], w["perm"]]
    vT_spec = pl.BlockSpec((1, HEADS * V_ROWS, TM), lambda i, t: (i, 0, t))

    def q_spec(tq):
        r = tq // TM
        return pl.BlockSpec((1, HEADS, 1, LANES, TM), lambda i, t: (i, 0, t // r, 0, t % r))

    out_shapes = [
        jax.ShapeDtypeStruct((b, HEADS, pl.cdiv(nt, TQ_MLA), LANES, TQ_MLA), BF16),
        jax.ShapeDtypeStruct((b, HEADS, nt, LANES), BF16),
        jax.ShapeDtypeStruct((b, HEADS * V_ROWS, nt), BF16),
        jax.ShapeDtypeStruct((b, HEADS, pl.cdiv(nt, TQ_DIFF), LANES, TQ_DIFF), BF16),
        jax.ShapeDtypeStruct((b, HEADS // 2, nt, LANES), BF16),
        jax.ShapeDtypeStruct((b, HEADS * V_ROWS, nt), BF16),
        jax.ShapeDtypeStruct((b, nt, 4 * POOL_G), F32),
        jax.ShapeDtypeStruct((b, nt, 3 * d), BF16),
        jax.ShapeDtypeStruct((b, nt // TM, HEADS, LANES), F32),
        jax.ShapeDtypeStruct((b, nt // TM, HEADS, LANES), F32),
    ]
    kn_spec = pl.BlockSpec((1, 1, HEADS, LANES), lambda i, t: (i, t, 0, 0))
    out_specs = [
        q_spec(TQ_MLA),
        pl.BlockSpec((1, HEADS, TM, LANES), lambda i, t: (i, 0, t, 0)),
        vT_spec,
        q_spec(TQ_DIFF),
        pl.BlockSpec((1, HEADS // 2, TM, LANES), lambda i, t: (i, 0, t, 0)),
        vT_spec,
        pl.BlockSpec((1, TM, 4 * POOL_G), lambda i, t: (i, t, 0)),
        pl.BlockSpec((1, TM, 3 * d), lambda i, t: (i, t, 0)),
        kn_spec, kn_spec,
    ]
    return pl.pallas_call(
        _inproj_kernel,
        grid=(b, nt // TM),
        in_specs=([tile, pl.BlockSpec((1, N_ADA, d), mod_map), _const_spec((1, d))]
                  + [_const_spec(a.shape) for a in weights]
                  + [tabT, tabT, tabN, tabN, tabN, tabN]),
        out_specs=out_specs,
        out_shape=out_shapes,
        compiler_params=_params(),
        name="inproj",
    )(xs, mod_l, g, *weights, tables["cosT"], tables["sinT"], tables["cosN"], tables["sinN"],
      tables["c128"], tables["s128"])


def _attn_kernel(*refs, chunks, diff, lam_init, aliased):
    refs = list(refs)
    q_ref, k_ref, v_ref, kn_ref = refs[:4]
    n_buf = ATT_AHEAD + 1
    s_bufs = refs[-n_buf:]
    o_ref = refs[-n_buf - 1]
    assert len(refs) == 5 + 2 * diff + aliased + n_buf
    nsub, tq = q_ref.shape[2], q_ref.shape[4]
    nc = len(chunks)
    head = pl.program_id(1)

    tile_max = jnp.max(kn_ref[0], axis=0)
    rows = lax.broadcasted_iota(jnp.int32, tile_max.shape, 0)
    kmax2 = jnp.max(jnp.where(rows == head, tile_max, 0.0), axis=0, keepdims=True)[:, 0:1]

    def finish(num, den, sub):
        o = num / den
        if diff:
            dl_ref, sg_ref = refs[4:6]
            dl = dl_ref[...]
            lam = (jnp.exp(jnp.sum(dl[0:1] * dl[1:2], axis=1, keepdims=True))
                   - jnp.exp(jnp.sum(dl[2:3] * dl[3:4], axis=1, keepdims=True)) + lam_init)
            o = o[:, :tq] - lam * o[:, tq:]
            o = o * lax.rsqrt(jnp.mean(o * o, axis=0, keepdims=True) + EPS)
            o = o * sg_ref[...] * (1.0 - lam_init)
        o_ref[0, sub] = o.astype(o_ref.dtype)

    def bounded_shift(rhs, bound, sub):
        acc = jnp.zeros((MLA_V, rhs.shape[1]), F32)
        den = jnp.zeros((1, rhs.shape[1]), F32)
        pieces = lambda c: [(lo, min(lo + ATT_ROWS, chunks[c][1])) for lo in range(*chunks[c], ATT_ROWS)]
        logits = lambda lo, hi: _dot(k_ref[0, 0, lo:hi, :], rhs)
        s_next = [logits(lo, hi) for lo, hi in pieces(0)]
        for c in range(nc):
            s_cur, s_next = s_next, []
            nxt = pieces(c + 1) if c + 1 < nc else []
            for i, (lo, hi) in enumerate(pieces(c)):
                if i < len(nxt):
                    s_next.append(logits(*nxt[i]))
                p = jnp.exp2(s_cur[i] - bound)
                den = den + jnp.sum(p, axis=0, keepdims=True)
                acc = acc + _dot(v_ref[0, 0:MLA_V, lo:hi], p.astype(BF16))
            s_next += [logits(lo, hi) for lo, hi in nxt[len(pieces(c)):]]
        finish(acc, den, sub)

    def running_max(rhs, sub, zero_row):
        n = rhs.shape[1]

        def scores(c):
            lo, hi = chunks[c]
            s = _dot(k_ref[0, 0, lo:hi, :], rhs)
            s_bufs[c % n_buf][pl.ds(zero_row, hi - lo), :] = s
            return jnp.max(s, axis=0, keepdims=True)

        m = jnp.full((1, n), -1e30, F32)
        acc = jnp.zeros((V_ROWS, n), F32)
        cmaxes = {c: scores(c) for c in range(min(ATT_AHEAD, nc))}
        for c in range(nc):
            lo, hi = chunks[c]
            m_new = jnp.maximum(m, cmaxes.pop(c))
            alpha = jnp.exp2(m - m_new)
            if c + ATT_AHEAD < nc:
                cmaxes[c + ATT_AHEAD] = scores(c + ATT_AHEAD)
            p = jnp.exp2(s_bufs[c % n_buf][pl.ds(zero_row, hi - lo), :] - m_new)
            m, acc = m_new, alpha * acc + _dot(v_ref[0, :, lo:hi], p.astype(BF16))
        finish(acc[0:MLA_V], acc[MLA_V:MLA_V + 1], sub)

    def sub_block(sub, zero_row):
        q = q_ref[0, 0, sub]
        if diff:
            first = (lax.broadcasted_iota(jnp.int32, q.shape, 0) & DIFF_HD) == 0
            zero = jnp.zeros_like(q)
            rhs = jnp.concatenate([jnp.where(first, q, zero), jnp.where(first, zero, q)], axis=1)
        else:
            rhs = q
        qf = rhs.astype(F32)
        bound = jnp.sqrt(jnp.sum(qf * qf, axis=0, keepdims=True) * kmax2)
        small = jnp.max(bound) <= ATT_BOUND_MAX

        @pl.when(small)
        def _():
            bounded_shift(rhs, bound, sub)

        @pl.when(jnp.logical_not(small))
        def _():
            running_max(rhs, sub, zero_row)

    zero_row = pl.multiple_of(jnp.minimum(pl.program_id(2), 0), 16)
    if nsub == 1:
        sub_block(0, zero_row)
    else:
        def body(sub, carry):
            sub_block(sub, zero_row)
            return carry
        lax.fori_loop(0, nsub, body, 0)


def _key_chunks(nk):
    head = [c for c in CHUNK_HEAD if sum(CHUNK_HEAD) + sum(CHUNK_TAIL) + TK <= nk]
    tail = CHUNK_TAIL if head else ()
    sizes = list(head)
    body = nk - sum(head) - sum(tail)
    sizes += [TK] * (body // TK) + ([body % TK] if body % TK else [])
    sizes += list(tail)
    edges = np.cumsum([0] + sizes)
    assert edges[-1] == nk and all(s % LANES == 0 for s in sizes)
    return tuple((int(a), int(b)) for a, b in zip(edges[:-1], edges[1:]))


def _attention(qT, k, vT, kn, *, tq, nsub, steps, q_block0, k_rows, k_block0, chunks,
               diff, extra=(), lam_init=0.0, prev_out=None):
    b, heads, nqb, _, tqw = qT.shape
    dv = MLA_V
    kdiv = 2 if diff else 1
    in_specs = [
        pl.BlockSpec((1, 1, nsub, LANES, tq), lambda i, h, j: (i, h, q_block0 + j, 0, 0)),
        pl.BlockSpec((1, 1, k_rows, LANES), lambda i, h, j: (i, h // kdiv, k_block0, 0)),
        pl.BlockSpec((1, V_ROWS, k_rows), lambda i, h, j: (i, h, k_block0)),
        pl.BlockSpec((1,) + kn.shape[1:], lambda i, h, j: (i, 0, 0, 0)),
    ]
    args = [qT, k, vT, kn]
    if diff:
        in_specs += [pl.BlockSpec(extra[0].shape, lambda i, h, j: (0, 0)),
                     pl.BlockSpec(extra[1].shape, lambda i, h, j: (0, 0))]
        args += list(extra)
    aliases = {}
    if prev_out is not None:
        in_specs.append(pl.BlockSpec(memory_space=pl.ANY))
        args.append(prev_out)
        aliases = {len(args) - 1: 0}
    return pl.pallas_call(
        functools.partial(_attn_kernel, chunks=chunks, diff=diff, lam_init=lam_init,
                          aliased=prev_out is not None),
        grid=(b, heads, steps),
        in_specs=in_specs,
        out_specs=pl.BlockSpec((1, nsub, dv, tq), lambda i, h, j: (i, q_block0 + j, h, 0)),
        out_shape=jax.ShapeDtypeStruct((b, nqb, heads * dv, tqw), BF16),
        scratch_shapes=[pltpu.VMEM((max(hi - lo for lo, hi in chunks), (2 * tq if diff else tq)), dt)
                        for dt in [F32] * (ATT_AHEAD + 1)],
        input_output_aliases=aliases,
        compiler_params=pltpu.CompilerParams(vmem_limit_bytes=VMEM_LIMIT, flags=ATT_FLAGS),
        name=("diff" if diff else "mla") + ("_ctx" if prev_out is not None else "_x"),
    )(*args)


def _mix_kernel(x_ref, mod_ref, oa_ref, od_ref, pc_ref, pp_ref, pn_ref, gt_ref, bg_ref,
                pproj_ref, pb_ref, ps_ref, wa_ref, wd_ref, wp_ref, wo_ref, o_ref, *, x_tiles):
    t = pl.program_id(1)
    x = x_ref[0]
    gate = mod_ref[0][5:6]
    tm, d = x.shape
    a = lax.dot_general(oa_ref[0, 0], wa_ref[...], TN_DIMS, preferred_element_type=F32)
    dd = lax.dot_general(od_ref[0, 0], wd_ref[...], TN_DIMS, preferred_element_type=F32)

    is_ctx = t == x_tiles
    has_prev = jnp.logical_and(t != 0, jnp.logical_not(is_ctx))
    has_next = jnp.logical_and(t != x_tiles - 1, jnp.logical_not(is_ctx))
    cur = pc_ref[0]
    prev = jnp.where(has_prev, pp_ref[0], 0.0)
    nxt = jnp.where(has_next, pn_ref[0], 0.0)
    ext = jnp.concatenate([prev, cur, nxt], axis=0)
    rows = ext.shape[0]
    seq_len = jnp.where(is_ctx, tm, x_tiles * tm)
    pos = jnp.where(is_ctx, 0, t * tm) + lax.broadcasted_iota(jnp.int32, (tm, 1), 0)
    outs = []
    for g, w in enumerate(POOL_WINDOWS):
        lanes = slice(g * POOL_G, (g + 1) * POOL_G)
        run = ext[:, lanes]
        span = 1
        while span < w:
            run = run + pltpu.roll(run, rows - span, axis=0)
            span *= 2
        win = pltpu.roll(run, rows - (POOL_HALO - w // 2), axis=0)[0:tm]
        cnt = (jnp.minimum(pos + w // 2, seq_len) - jnp.maximum(pos - w // 2, 0)).astype(F32)
        pooled = (win / cnt - cur[:, lanes]).astype(BF16)
        outs.append((_dot(pooled, pproj_ref[g]) + pb_ref[:, lanes]) * ps_ref[:, lanes])
    pooled_out = _dot(jnp.concatenate(outs, axis=1).astype(BF16), wp_ref[...])

    gs = jax.nn.sigmoid(gt_ref[0].astype(F32) + bg_ref[...])
    merged = gs[:, 0:d] * a + gs[:, d:2 * d] * dd + gs[:, 2 * d:3 * d] * pooled_out
    o_ref[0] = x + gate * _dot(merged.astype(BF16), wo_ref[...])


def _mix(xs, mod_l, oa, od, pool_in, gates, w, *, mod_map, x_tiles):
    b, nt, d = xs.shape
    hb = TM // POOL_HALO
    n_halo = nt // POOL_HALO
    tile = pl.BlockSpec((1, TM, d), lambda i, t: (i, t, 0))
    pw = 4 * POOL_G
    weights = [w["bgate"], w["pproj"], w["pb"], w["ps"], w["wa"], w["wd"], w["wp"], w["wo"]]

    def o_spec(o):
        r = o.shape[3] // TM
        return pl.BlockSpec((1, 1, o.shape[2], TM), lambda i, t: (i, t // r, 0, t % r))

    return pl.pallas_call(
        functools.partial(_mix_kernel, x_tiles=x_tiles),
        grid=(b, nt // TM),
        in_specs=[tile, pl.BlockSpec((1, N_ADA, d), mod_map),
                  o_spec(oa), o_spec(od),
                  pl.BlockSpec((1, TM, pw), lambda i, t: (i, t, 0)),
                  pl.BlockSpec((1, POOL_HALO, pw), lambda i, t: (i, jnp.maximum(t * hb - 1, 0), 0)),
                  pl.BlockSpec((1, POOL_HALO, pw), lambda i, t: (i, jnp.minimum((t + 1) * hb, n_halo - 1), 0)),
                  pl.BlockSpec((1, TM, 3 * d), lambda i, t: (i, t, 0))]
                 + [_const_spec(a.shape) for a in weights],
        out_specs=tile,
        out_shape=jax.ShapeDtypeStruct(xs.shape, F32),
        input_output_aliases={0: 0},
        compiler_params=_params(),
        name="mix",
    )(xs, mod_l, oa, od, pool_in, pool_in, pool_in, gates, *weights)


def _final_kernel(x_ref, g_ref, o_ref):
    o_ref[0] = _rms(x_ref[0]) * g_ref[...]


def _final_norm(xs, g, seq):
    b, _, d = xs.shape
    tile = pl.BlockSpec((1, TM, d), lambda i, t: (i, t, 0))
    return pl.pallas_call(
        _final_kernel,
        grid=(b, seq // TM),
        in_specs=[tile, _const_spec((1, d))],
        out_specs=tile,
        out_shape=jax.ShapeDtypeStruct((b, seq, d), F32),
        compiler_params=_params(),
        name="final_norm",
    )(xs, g)


def _rope_tables(seq, ctx):
    rows = seq // GRID_W
    row_ids = jnp.repeat(jnp.arange(rows), GRID_W).astype(F32)
    col_ids = jnp.tile(jnp.arange(GRID_W), rows).astype(F32)
    n_freq = ROPE_HALF // 2
    inv_freq = ROPE_THETA ** (-jnp.arange(n_freq, dtype=F32) / n_freq)
    ang = jnp.concatenate([row_ids[:, None] * inv_freq, col_ids[:, None] * inv_freq], axis=-1)
    cos = jnp.concatenate([jnp.cos(ang), jnp.ones((ctx, ROPE_HALF), F32)], axis=0)
    sin = jnp.concatenate([jnp.sin(ang), jnp.zeros((ctx, ROPE_HALF), F32)], axis=0)
    pad = jnp.zeros((seq + ctx, LANES - 2 * ROPE_HALF), F32)
    return {
        "cosT": jnp.tile(cos.T, (HEADS, 1)), "sinT": jnp.tile(sin.T, (HEADS, 1)),
        "cosN": jnp.tile(cos, (1, HEADS)), "sinN": jnp.tile(sin, (1, HEADS)),
        "c128": jnp.concatenate([cos, cos, pad], axis=1),
        "s128": jnp.concatenate([sin, sin, pad], axis=1),
    }


def _index_tables():
    hw = ROPE_HALF
    uq = np.zeros(HEADS * MLA_QK, np.int32)
    for h in range(HEADS):
        for j in range(MLA_NOPE):
            uq[h * MLA_NOPE + j] = h * MLA_QK + j
        for f in range(hw):
            uq[HEADS * MLA_NOPE + h * hw + f] = h * MLA_QK + MLA_NOPE + f
            uq[HEADS * MLA_NOPE + LANES + h * hw + f] = h * MLA_QK + MLA_NOPE + hw + f
    dqk = np.zeros(HEADS * 2 * DIFF_HD, np.int32)
    perm = np.zeros((HEADS * 2 * DIFF_HD, HEADS * 2 * DIFF_HD), np.float32)
    for h in range(HEADS):
        for c in range(2):
            for half in range(2):
                for f in range(hw):
                    src = h * 2 * DIFF_HD + c * DIFF_HD + half * hw + f
                    blk = (c * 2 + half) * LANES + h * hw + f
                    dqk[blk] = src
                    perm[blk, src] = 1.0
    v_cols = np.array([h * (MLA_NOPE + MLA_V) + MLA_NOPE + j for h in range(HEADS) for j in range(MLA_V)], np.int32)
    k_mask = ((np.arange(HEADS * (MLA_NOPE + MLA_V)) % (MLA_NOPE + MLA_V)) < MLA_NOPE).astype(np.float32)
    place = np.zeros((LANES, HEADS * LANES), np.float32)
    for h in range(HEADS):
        for f in range(MLA_ROPE):
            place[f, h * LANES + MLA_NOPE + f] = 1.0
    return uq, dqk, perm, v_cols, k_mask, place


def _layer_weights(l, p, idx):
    uq, dqk, perm, v_cols, k_mask, place = idx
    d = p["w_in"].shape[1]
    w_in = p["w_in"][l]
    o_cq, o_ckv, o_kr = 0, 384, 640
    o_dq, o_dk, o_dv, o_pool, o_gate = 672, 1184, 1696, 2208, 2720
    wkr = w_in[:, o_kr:o_kr + MLA_ROPE]
    wkr_rot = jnp.concatenate([-wkr[:, ROPE_HALF:], wkr[:, :ROPE_HALF]], axis=1)
    zpad = jnp.zeros((d, LANES - MLA_ROPE), F32)
    wkr2 = jnp.concatenate([wkr, zpad, wkr_rot, zpad], axis=1)
    w_ukv = p["mla_w_ukv"][l]
    bf = lambda a: a.astype(BF16)
    return {
        "wcq": bf(w_in[:, o_cq:o_ckv]), "wckv": bf(w_in[:, o_ckv:o_kr]), "wkr2": bf(wkr2),
        "wdqT": bf(w_in[:, o_dq + dqk].T), "wdk": bf(w_in[:, o_dk + dqk]),
        "wdvT": bf(w_in[:, o_dv:o_pool].T), "wpool": bf(w_in[:, o_pool:o_gate]), "wgates": bf(w_in[:, o_gate:]),
        "qng": p["mla_q_norm_g"][l][None], "kvng": p["mla_kv_norm_g"][l][None],
        "wuqT": bf(p["mla_w_uq"][l][:, uq].T), "wkpad": bf(w_ukv * k_mask[None]), "wvT": bf(w_ukv[:, v_cols].T),
        "place": jnp.asarray(place, BF16), "perm": jnp.asarray(perm, BF16),
        "bgate": p["b_gate"][l].reshape(1, -1), "pproj": bf(p["pool_proj"][l]),
        "pb": p["pool_b"][l].reshape(1, -1), "ps": p["pool_scale"][l][None],
        "wa": bf(p["w_br_mla"][l]), "wd": bf(p["w_br_diff"][l]), "wp": bf(p["w_br_pool"][l]), "wo": bf(p["w_out"][l]),
    }


def kernel(x, c, ctx, c_ctx, ada_w, ada_b, norm_g, ffa_w_gate, ffa_w_up, ffa_w_down, ffb_w_gate, ffb_w_up, ffb_w_down, w_in, b_gate, mla_q_norm_g, mla_kv_norm_g, mla_w_uq, mla_w_ukv, diff_lambda, diff_subln_g, pool_proj, pool_b, pool_scale, w_br_mla, w_br_diff, w_br_pool, w_out, final_g):
    b, seq, d = x.shape
    n_ctx = ctx.shape[1]
    depth = ada_w.shape[0]
    nt = seq + n_ctx
    nsub = ATT_NSUB if seq % (ATT_NSUB * TQ_MLA) == 0 else 1
    assert n_ctx == TM and seq % (nsub * TQ_MLA) == 0 and seq % GRID_W == 0 and b + 1 <= 8
    x_tiles = seq // TM
    p = dict(w_in=w_in, b_gate=b_gate, mla_q_norm_g=mla_q_norm_g, mla_kv_norm_g=mla_kv_norm_g,
             mla_w_uq=mla_w_uq, mla_w_ukv=mla_w_ukv, pool_proj=pool_proj, pool_b=pool_b,
             pool_scale=pool_scale, w_br_mla=w_br_mla, w_br_diff=w_br_diff, w_br_pool=w_br_pool, w_out=w_out)

    c_rows = jnp.concatenate([c, c_ctx[None], jnp.zeros((8 - b - 1, d), F32)], axis=0)
    mod = _modulation(c_rows, ada_w, ada_b).reshape(depth, 8, N_ADA, d)
    mod_map = lambda i, t: (jnp.where(t == x_tiles, b, i), 0, 0)

    tables = _rope_tables(seq, n_ctx)
    idx = _index_tables()
    xs = jnp.concatenate([x, ctx], axis=1)
    chunks = _key_chunks(nt)
    bf = lambda a: a.astype(BF16)

    for l in range(depth):
        w = _layer_weights(l, p, idx)
        lam_init = 0.8 - 0.6 * math.exp(-0.3 * l)
        xs = _ffn(xs, mod[l], norm_g[l, 0][None], bf(ffa_w_gate[l]), bf(ffa_w_up[l]), bf(ffa_w_down[l]),
                  mi=0, mod_map=mod_map)
        q_mla, k_mla, v_mla, q_diff, k_diff, v_diff, pool_in, gates, kn_mla, kn_diff = _inproj(
            xs, mod[l], norm_g[l, 1][None], w, tables, mod_map=mod_map)
        mla = (q_mla, k_mla, v_mla, kn_mla)
        dif = (q_diff, k_diff, v_diff, kn_diff)
        extra = (diff_lambda[l], diff_subln_g[l][:, None])
        full = dict(nsub=nsub, q_block0=0, k_rows=nt, k_block0=0, chunks=chunks)
        ctx_only = dict(tq=TM, nsub=1, steps=1, k_rows=TM, k_block0=x_tiles, chunks=((0, TM),))
        oa = _attention(*mla, tq=TQ_MLA, steps=seq // (nsub * TQ_MLA), diff=False, **full)
        oa = _attention(*mla, q_block0=seq // TQ_MLA, diff=False, prev_out=oa, **ctx_only)
        od = _attention(*dif, tq=TQ_DIFF, steps=seq // (nsub * TQ_DIFF), diff=True,
                        extra=extra, lam_init=lam_init, **full)
        od = _attention(*dif, q_block0=seq // TQ_DIFF, diff=True, extra=extra,
                        lam_init=lam_init, prev_out=od, **ctx_only)
        xs = _mix(xs, mod[l], oa, od, pool_in, gates, w, mod_map=mod_map, x_tiles=x_tiles)
        xs = _ffn(xs, mod[l], norm_g[l, 2][None], bf(ffb_w_gate[l]), bf(ffb_w_up[l]), bf(ffb_w_down[l]),
                  mi=2, mod_map=mod_map)

    return _final_norm(xs, final_g[None], seq)
```

```python
import functools
import math

import numpy as np
import jax
import jax.numpy as jnp
from jax import lax
from jax.experimental import pallas as pl
from jax.experimental.pallas import tpu as pltpu

F32 = jnp.float32
BF16 = jnp.bfloat16

EPS = 1e-6
ROPE_THETA = 10000.0
GRID_W = 64
N_ADA = 9

HEADS = 8
MLA_NOPE = 64
MLA_ROPE = 32
MLA_V = 64
V_ROWS = MLA_V + 16
MLA_QK = MLA_NOPE + MLA_ROPE
LOG2E = math.log2(math.e)
MLA_SCALE = MLA_QK ** -0.5 * LOG2E
DIFF_HD = 32
DIFF_V = 64
DIFF_SCALE = DIFF_HD ** -0.5 * LOG2E
ROPE_HALF = 16
POOL_WINDOWS = (2, 4, 8, 16)
POOL_G = 128
POOL_HALO = 16

LANES = 128
TM = 256
TK = 512
CHUNK_HEAD = (256,)
CHUNK_TAIL = ()
ATT_AHEAD = 2
ATT_ROWS = 256
ATT_BOUND_MAX = 40.0
ATT_NSUB = 4
TQ_MLA = 1024
TQ_DIFF = 512
VMEM_LIMIT = 52 * 1024 * 1024

ATT_FLAGS = {}

NT_DIMS = (((1,), (1,)), ((), ()))
TN_DIMS = (((0,), (0,)), ((), ()))


def _params():
    return pltpu.CompilerParams(vmem_limit_bytes=VMEM_LIMIT)


def _const_spec(shape):
    zeros = (0,) * len(shape)
    return pl.BlockSpec(shape, lambda *_: zeros, pipeline_mode=pl.Buffered(1))


def _layer_spec(a, *lead):
    tail = a.shape[len(lead):]
    zeros = (0,) * len(tail)
    return pl.BlockSpec((None,) * len(lead) + tail, lambda *_: tuple(lead) + zeros, pipeline_mode=pl.Buffered(1))


def _mod_spec(mod, l, mod_row):
    return pl.BlockSpec((None, 1) + mod.shape[2:], lambda i, t: (l, mod_row(i, t), 0, 0))


def _dot(a, b):
    return jnp.dot(a, b, preferred_element_type=F32)


def _rms(x):
    return x * lax.rsqrt(jnp.mean(x * x, axis=-1, keepdims=True) + EPS)


def _mod_kernel(c_ref, w_ref, b_ref, o_ref):
    c = c_ref[...]
    a = c * jax.nn.sigmoid(c)
    o_ref[0] = jnp.dot(a, w_ref[0], preferred_element_type=F32,
                       precision=lax.Precision.HIGHEST) + b_ref[0]


def _modulation(c_rows, ada_w, ada_b):
    depth, d, n = ada_w.shape
    bn = n // 8
    return pl.pallas_call(
        _mod_kernel,
        grid=(depth, n // bn),
        in_specs=[pl.BlockSpec((8, d), lambda l, j: (0, 0)),
                  pl.BlockSpec((1, d, bn), lambda l, j: (l, 0, j)),
                  pl.BlockSpec((1, 1, bn), lambda l, j: (l, 0, j))],
        out_specs=pl.BlockSpec((1, 8, bn), lambda l, j: (l, 0, j)),
        out_shape=jax.ShapeDtypeStruct((depth, 8, n), F32),
        compiler_params=_params(),
        name="modulation",
    )(c_rows, ada_w, ada_b.reshape(depth, 1, n))


def _ffn_kernel(x_ref, mod_ref, g_ref, wg_ref, wu_ref, wd_ref, *rest, mi, f_chunks, x_tiles):
    o_ref = rest[-1] if x_tiles is None else rest[-2]
    x = x_ref[0]
    m = mod_ref[0]
    shift, scale, gate = m[3 * mi:3 * mi + 1], m[3 * mi + 1:3 * mi + 2], m[3 * mi + 2:3 * mi + 3]
    u = (_rms(x) * g_ref[...]) * (1.0 + scale) + shift
    ub = u.astype(BF16)
    y = jnp.zeros(x.shape, F32)
    for lo, hi in f_chunks:
        a = _dot(ub, wg_ref[:, lo:hi])
        b = _dot(ub, wu_ref[:, lo:hi])
        hid = (a * jax.nn.sigmoid(a) * b).astype(BF16)
        y = y + _dot(hid, wd_ref[lo:hi, :])
    new = x + (0.5 * gate) * y
    o_ref[0] = new
    if x_tiles is not None:
        fg_ref, final_ref = rest[0], rest[-1]

        @pl.when(pl.program_id(1) < x_tiles)
        def _():
            final_ref[0] = _rms(new) * fg_ref[...]


def _ffn(xs, mod, gains, wg, wu, wd, *, l, mi, mod_row, final=None):
    b, nt, d = xs.shape
    f = wg.shape[2]
    chunk = 1024
    f_chunks = tuple((lo, min(lo + chunk, f)) for lo in range(0, f, chunk))
    tile = pl.BlockSpec((1, TM, d), lambda i, t: (i, t, 0))
    in_specs = [tile, _mod_spec(mod, l, mod_row), _layer_spec(gains, l, mi),
                _layer_spec(wg, l), _layer_spec(wu, l), _layer_spec(wd, l)]
    args = [xs, mod, gains, wg, wu, wd]
    out_specs, out_shape, x_tiles = tile, jax.ShapeDtypeStruct(xs.shape, F32), None
    if final is not None:
        fg, x_tiles = final
        in_specs.append(_const_spec(fg.shape))
        args.append(fg)
        out_specs = [tile, pl.BlockSpec((1, TM, d), lambda i, t: (i, jnp.minimum(t, x_tiles - 1), 0))]
        out_shape = [out_shape, jax.ShapeDtypeStruct((b, x_tiles * TM, d), F32)]
    return pl.pallas_call(
        functools.partial(_ffn_kernel, mi=mi, f_chunks=f_chunks, x_tiles=x_tiles),
        grid=(b, nt // TM),
        in_specs=in_specs,
        out_specs=out_specs,
        out_shape=out_shape,
        input_output_aliases={0: 0},
        compiler_params=_params(),
        name=f"ffn{mi}",
    )(*args)


def _inproj_kernel(x_ref, mod_ref, g_ref, wcq, wckv, wkr2, wdqT, wdk, wdvT, wpool, wgates,
                   qng, kvng, wuqT, wkpad, wvT, place, perm,
                   cosT, sinT, cosN, sinN, c128, s128,
                   oq_mla, ok_mla, ov_mla, oq_diff, ok_diff, ov_diff, opool, ogates, okn_mla, okn_diff):
    x = x_ref[0]
    m = mod_ref[0]
    u = (_rms(x) * g_ref[...]) * (1.0 + m[4:5]) + m[3:4]
    ub = u.astype(BF16)
    tm = x.shape[0]
    cT, sT = cosT[...], sinT[...]
    hw = ROPE_HALF

    cqn = (_rms(_dot(ub, wcq[...])) * qng[...]).astype(BF16)
    qT = lax.dot_general(wuqT[...], cqn, NT_DIMS, preferred_element_type=F32) * MLA_SCALE
    nope_w = HEADS * MLA_NOPE
    x1, x2 = qT[nope_w:nope_w + LANES], qT[nope_w + LANES:nope_w + 2 * LANES]
    qn = qT[0:nope_w].astype(BF16)
    r1 = (x1 * cT - x2 * sT).astype(BF16)
    r2 = (x1 * sT + x2 * cT).astype(BF16)
    zpad = jnp.zeros((LANES - MLA_QK, tm), BF16)
    for h in range(HEADS):
        oq_mla[0, h, 0, 0:MLA_NOPE, :] = qn[h * MLA_NOPE:(h + 1) * MLA_NOPE]
        oq_mla[0, h, 0, MLA_NOPE:MLA_NOPE + hw, :] = r1[h * hw:(h + 1) * hw]
        oq_mla[0, h, 0, MLA_NOPE + hw:MLA_QK, :] = r2[h * hw:(h + 1) * hw]
        oq_mla[0, h, 0, MLA_QK:LANES, :] = zpad

    ckvn = (_rms(_dot(ub, wckv[...])) * kvng[...]).astype(BF16)
    kr2 = _dot(ub, wkr2[...])
    krr = (kr2[:, 0:LANES] * c128[...] + kr2[:, LANES:2 * LANES] * s128[...]).astype(BF16)
    kall = (_dot(ckvn, wkpad[...]) + _dot(krr, place[...])).astype(BF16)
    vT = lax.dot_general(wvT[...], ckvn, NT_DIMS, preferred_element_type=F32).astype(BF16)
    head_row = lax.broadcasted_iota(jnp.int32, (HEADS, LANES), 0)
    left_half = lax.broadcasted_iota(jnp.int32, (tm, LANES), 1) < 2 * DIFF_HD

    def max_sq_norm(sq):
        return jnp.max(jnp.sum(sq, axis=1, keepdims=True), axis=0, keepdims=True)

    kn = jnp.zeros((HEADS, LANES), F32)
    for h in range(HEADS):
        kh = kall[:, h * LANES:(h + 1) * LANES]
        ok_mla[0, h] = kh
        khf = kh.astype(F32)
        kn = jnp.where(head_row == h, max_sq_norm(khf * khf), kn)
    okn_mla[0, 0] = kn
    ones_rows = (lax.broadcasted_iota(jnp.int32, (V_ROWS - MLA_V, tm), 0) == 0).astype(BF16)

    def store_values(ov, vals):
        for h in range(HEADS):
            ov[0, h * V_ROWS:h * V_ROWS + MLA_V, :] = vals[h * MLA_V:(h + 1) * MLA_V]
            ov[0, h * V_ROWS + MLA_V:(h + 1) * V_ROWS, :] = ones_rows

    store_values(ov_mla, vT)

    dqT = lax.dot_general(wdqT[...], ub, NT_DIMS, preferred_element_type=F32) * DIFF_SCALE
    a1, a2, b1, b2 = (dqT[i * LANES:(i + 1) * LANES] for i in range(4))
    parts = [(a1 * cT - a2 * sT).astype(BF16), (a1 * sT + a2 * cT).astype(BF16),
             (b1 * cT - b2 * sT).astype(BF16), (b1 * sT + b2 * cT).astype(BF16)]
    zhalf = jnp.zeros((2 * DIFF_HD, tm), BF16)
    for h in range(HEADS):
        base = (h % 2) * 2 * DIFF_HD
        for i, part in enumerate(parts):
            oq_diff[0, h, 0, base + i * hw:base + (i + 1) * hw, :] = part[h * hw:(h + 1) * hw]
        other = 2 * DIFF_HD - base
        oq_diff[0, h, 0, other:other + 2 * DIFF_HD, :] = zhalf

    dk = _dot(ub, wdk[...])
    cN, sN = cosN[...], sinN[...]
    k1a, k1b, k2a, k2b = (dk[:, i * LANES:(i + 1) * LANES] for i in range(4))
    rk = jnp.concatenate([k1a * cN - k1b * sN, k1a * sN + k1b * cN,
                          k2a * cN - k2b * sN, k2a * sN + k2b * cN], axis=1).astype(BF16)
    kd = _dot(rk, perm[...]).astype(BF16)
    kn = jnp.zeros((HEADS, LANES), F32)
    for p in range(HEADS // 2):
        kp = kd[:, p * LANES:(p + 1) * LANES]
        ok_diff[0, p] = kp
        kpf = kp.astype(F32)
        sq = kpf * kpf
        kn = jnp.where(head_row == 2 * p, max_sq_norm(jnp.where(left_half, sq, 0.0)), kn)
        kn = jnp.where(head_row == 2 * p + 1, max_sq_norm(jnp.where(left_half, 0.0, sq)), kn)
    okn_diff[0, 0] = kn
    store_values(ov_diff, lax.dot_general(wdvT[...], ub, NT_DIMS, preferred_element_type=F32).astype(BF16))

    opool[0] = _dot(ub, wpool[...])
    ogates[0] = _dot(ub, wgates[...]).astype(BF16)


def _inproj(xs, mod, gains, w, tables, *, l, mod_row):
    b, nt, d = xs.shape
    tile = pl.BlockSpec((1, TM, d), lambda i, t: (i, t, 0))
    tabT = pl.BlockSpec((LANES, TM), lambda i, t: (0, t))
    tabN = pl.BlockSpec((TM, LANES), lambda i, t: (t, 0))
    stacked = [w["wcq"], w["wckv"], w["wkr2"], w["wdqT"], w["wdk"], w["wdvT"], w["wpool"], w["wgates"],
               w["qng"], w["kvng"], w["wuqT"], w["wkpad"], w["wvT"]]
    shared = [w["place"], w["perm"]]
    vT_spec = pl.BlockSpec((1, HEADS * V_ROWS, TM), lambda i, t: (i, 0, t))

    def q_spec(tq):
        r = tq // TM
        return pl.BlockSpec((1, HEADS, 1, LANES, TM), lambda i, t: (i, 0, t // r, 0, t % r))

    out_shapes = [
        jax.ShapeDtypeStruct((b, HEADS, pl.cdiv(nt, TQ_MLA), LANES, TQ_MLA), BF16),
        jax.ShapeDtypeStruct((b, HEADS, nt, LANES), BF16),
        jax.ShapeDtypeStruct((b, HEADS * V_ROWS, nt), BF16),
        jax.ShapeDtypeStruct((b, HEADS, pl.cdiv(nt, TQ_DIFF), LANES, TQ_DIFF), BF16),
        jax.ShapeDtypeStruct((b, HEADS // 2, nt, LANES), BF16),
        jax.ShapeDtypeStruct((b, HEADS * V_ROWS, nt), BF16),
        jax.ShapeDtypeStruct((b, nt, 4 * POOL_G), F32),
        jax.ShapeDtypeStruct((b, nt, 3 * d), BF16),
        jax.ShapeDtypeStruct((b, nt // TM, HEADS, LANES), F32),
        jax.ShapeDtypeStruct((b, nt // TM, HEADS, LANES), F32),
    ]
    kn_spec = pl.BlockSpec((1, 1, HEADS, LANES), lambda i, t: (i, t, 0, 0))
    out_specs = [
        q_spec(TQ_MLA),
        pl.BlockSpec((1, HEADS, TM, LANES), lambda i, t: (i, 0, t, 0)),
        vT_spec,
        q_spec(TQ_DIFF),
        pl.BlockSpec((1, HEADS // 2, TM, LANES), lambda i, t: (i, 0, t, 0)),
        vT_spec,
        pl.BlockSpec((1, TM, 4 * POOL_G), lambda i, t: (i, t, 0)),
        pl.BlockSpec((1, TM, 3 * d), lambda i, t: (i, t, 0)),
        kn_spec, kn_spec,
    ]
    return pl.pallas_call(
        _inproj_kernel,
        grid=(b, nt // TM),
        in_specs=([tile, _mod_spec(mod, l, mod_row), _layer_spec(gains, l, 1)]
                  + [_layer_spec(a, l) for a in stacked] + [_const_spec(a.shape) for a in shared]
                  + [tabT, tabT, tabN, tabN, tabN, tabN]),
        out_specs=out_specs,
        out_shape=out_shapes,
        compiler_params=_params(),
        name="inproj",
    )(xs, mod, gains, *stacked, *shared, tables["cosT"], tables["sinT"], tables["cosN"], tables["sinN"],
      tables["c128"], tables["s128"])


def _attn_kernel(*refs, chunks, diff, lam_init, aliased):
    refs = list(refs)
    q_ref, k_ref, v_ref, kn_ref = refs[:4]
    n_buf = ATT_AHEAD + 1
    s_bufs = refs[-n_buf:]
    o_ref = refs[-n_buf - 1]
    assert len(refs) == 5 + 2 * diff + aliased + n_buf
    nsub, tq = q_ref.shape[2], q_ref.shape[4]
    nc = len(chunks)
    head = pl.program_id(1)

    tile_max = jnp.max(kn_ref[0], axis=0)
    rows = lax.broadcasted_iota(jnp.int32, tile_max.shape, 0)
    kmax2 = jnp.max(jnp.where(rows == head, tile_max, 0.0), axis=0, keepdims=True)[:, 0:1]

    def finish(num, den, sub):
        o = num / den
        if diff:
            dl_ref, sg_ref = refs[4:6]
            dl = dl_ref[...]
            lam = (jnp.exp(jnp.sum(dl[0:1] * dl[1:2], axis=1, keepdims=True))
                   - jnp.exp(jnp.sum(dl[2:3] * dl[3:4], axis=1, keepdims=True)) + lam_init)
            o = o[:, :tq] - lam * o[:, tq:]
            o = o * lax.rsqrt(jnp.mean(o * o, axis=0, keepdims=True) + EPS)
            o = o * sg_ref[...] * (1.0 - lam_init)
        o_ref[0, sub] = o.astype(o_ref.dtype)

    def bounded_shift(rhs, bound, sub):
        acc = jnp.zeros((MLA_V, rhs.shape[1]), F32)
        den = jnp.zeros((1, rhs.shape[1]), F32)
        pieces = lambda c: [(lo, min(lo + ATT_ROWS, chunks[c][1])) for lo in range(*chunks[c], ATT_ROWS)]
        logits = lambda lo, hi: _dot(k_ref[0, 0, lo:hi, :], rhs)
        s_next = [logits(lo, hi) for lo, hi in pieces(0)]
        for c in range(nc):
            s_cur, s_next = s_next, []
            nxt = pieces(c + 1) if c + 1 < nc else []
            for i, (lo, hi) in enumerate(pieces(c)):
                if i < len(nxt):
                    s_next.append(logits(*nxt[i]))
                p = jnp.exp2(s_cur[i] - bound)
                den = den + jnp.sum(p, axis=0, keepdims=True)
                acc = acc + _dot(v_ref[0, 0:MLA_V, lo:hi], p.astype(BF16))
            s_next += [logits(lo, hi) for lo, hi in nxt[len(pieces(c)):]]
        finish(acc, den, sub)

    def running_max(rhs, sub, zero_row):
        n = rhs.shape[1]

        def scores(c):
            lo, hi = chunks[c]
            s = _dot(k_ref[0, 0, lo:hi, :], rhs)
            s_bufs[c % n_buf][pl.ds(zero_row, hi - lo), :] = s
            return jnp.max(s, axis=0, keepdims=True)

        m = jnp.full((1, n), -1e30, F32)
        acc = jnp.zeros((V_ROWS, n), F32)
        cmaxes = {c: scores(c) for c in range(min(ATT_AHEAD, nc))}
        for c in range(nc):
            lo, hi = chunks[c]
            m_new = jnp.maximum(m, cmaxes.pop(c))
            alpha = jnp.exp2(m - m_new)
            if c + ATT_AHEAD < nc:
                cmaxes[c + ATT_AHEAD] = scores(c + ATT_AHEAD)
            p = jnp.exp2(s_bufs[c % n_buf][pl.ds(zero_row, hi - lo), :] - m_new)
            m, acc = m_new, alpha * acc + _dot(v_ref[0, :, lo:hi], p.astype(BF16))
        finish(acc[0:MLA_V], acc[MLA_V:MLA_V + 1], sub)

    def sub_block(sub, zero_row):
        q = q_ref[0, 0, sub]
        if diff:
            first = (lax.broadcasted_iota(jnp.int32, q.shape, 0) & DIFF_HD) == 0
            zero = jnp.zeros_like(q)
            rhs = jnp.concatenate([jnp.where(first, q, zero), jnp.where(first, zero, q)], axis=1)
        else:
            rhs = q
        qf = rhs.astype(F32)
        bound = jnp.sqrt(jnp.sum(qf * qf, axis=0, keepdims=True) * kmax2)
        small = jnp.max(bound) <= ATT_BOUND_MAX

        @pl.when(small)
        def _():
            bounded_shift(rhs, bound, sub)

        @pl.when(jnp.logical_not(small))
        def _():
            running_max(rhs, sub, zero_row)

    zero_row = pl.multiple_of(jnp.minimum(pl.program_id(2), 0), 16)
    if nsub == 1:
        sub_block(0, zero_row)
    else:
        def body(sub, carry):
            sub_block(sub, zero_row)
            return carry
        lax.fori_loop(0, nsub, body, 0)


def _key_chunks(nk):
    head = [c for c in CHUNK_HEAD if sum(CHUNK_HEAD) + sum(CHUNK_TAIL) + TK <= nk]
    tail = CHUNK_TAIL if head else ()
    sizes = list(head)
    body = nk - sum(head) - sum(tail)
    sizes += [TK] * (body // TK) + ([body % TK] if body % TK else [])
    sizes += list(tail)
    edges = np.cumsum([0] + sizes)
    assert edges[-1] == nk and all(s % LANES == 0 for s in sizes)
    return tuple((int(a), int(b)) for a, b in zip(edges[:-1], edges[1:]))


def _attention(qT, k, vT, kn, *, tq, nsub, steps, q_block0, k_rows, k_block0, chunks,
               diff, extra=(), layer=0, lam_init=0.0, prev_out=None):
    b, heads, nqb, _, tqw = qT.shape
    dv = MLA_V
    kdiv = 2 if diff else 1
    in_specs = [
        pl.BlockSpec((1, 1, nsub, LANES, tq), lambda i, h, j: (i, h, q_block0 + j, 0, 0)),
        pl.BlockSpec((1, 1, k_rows, LANES), lambda i, h, j: (i, h // kdiv, k_block0, 0)),
        pl.BlockSpec((1, V_ROWS, k_rows), lambda i, h, j: (i, h, k_block0)),
        pl.BlockSpec((1,) + kn.shape[1:], lambda i, h, j: (i, 0, 0, 0)),
    ]
    args = [qT, k, vT, kn]
    if diff:
        in_specs += [_layer_spec(a, layer) for a in extra]
        args += list(extra)
    aliases = {}
    if prev_out is not None:
        in_specs.append(pl.BlockSpec(memory_space=pl.ANY))
        args.append(prev_out)
        aliases = {len(args) - 1: 0}
    return pl.pallas_call(
        functools.partial(_attn_kernel, chunks=chunks, diff=diff, lam_init=lam_init,
                          aliased=prev_out is not None),
        grid=(b, heads, steps),
        in_specs=in_specs,
        out_specs=pl.BlockSpec((1, nsub, dv, tq), lambda i, h, j: (i, q_block0 + j, h, 0)),
        out_shape=jax.ShapeDtypeStruct((b, nqb, heads * dv, tqw), BF16),
        scratch_shapes=[pltpu.VMEM((max(hi - lo for lo, hi in chunks), (2 * tq if diff else tq)), dt)
                        for dt in [F32] * (ATT_AHEAD + 1)],
        input_output_aliases=aliases,
        compiler_params=pltpu.CompilerParams(vmem_limit_bytes=VMEM_LIMIT, flags=ATT_FLAGS),
        name=("diff" if diff else "mla") + ("_ctx" if prev_out is not None else "_x"),
    )(*args)


def _mix_kernel(x_ref, mod_ref, oa_ref, od_ref, pc_ref, pp_ref, pn_ref, gt_ref, bg_ref,
                pproj_ref, pb_ref, ps_ref, wa_ref, wd_ref, wp_ref, wo_ref, o_ref, *, x_tiles):
    t = pl.program_id(1)
    x = x_ref[0]
    gate = mod_ref[0][5:6]
    tm, d = x.shape
    a = lax.dot_general(oa_ref[0, 0], wa_ref[...], TN_DIMS, preferred_element_type=F32)
    dd = lax.dot_general(od_ref[0, 0], wd_ref[...], TN_DIMS, preferred_element_type=F32)

    is_ctx = t == x_tiles
    has_prev = jnp.logical_and(t != 0, jnp.logical_not(is_ctx))
    has_next = jnp.logical_and(t != x_tiles - 1, jnp.logical_not(is_ctx))
    cur = pc_ref[0]
    prev = jnp.where(has_prev, pp_ref[0], 0.0)
    nxt = jnp.where(has_next, pn_ref[0], 0.0)
    ext = jnp.concatenate([prev, cur, nxt], axis=0)
    rows = ext.shape[0]
    seq_len = jnp.where(is_ctx, tm, x_tiles * tm)
    pos = jnp.where(is_ctx, 0, t * tm) + lax.broadcasted_iota(jnp.int32, (tm, 1), 0)
    outs = []
    for g, w in enumerate(POOL_WINDOWS):
        lanes = slice(g * POOL_G, (g + 1) * POOL_G)
        run = ext[:, lanes]
        span = 1
        while span < w:
            run = run + pltpu.roll(run, rows - span, axis=0)
            span *= 2
        win = pltpu.roll(run, rows - (POOL_HALO - w // 2), axis=0)[0:tm]
        cnt = (jnp.minimum(pos + w // 2, seq_len) - jnp.maximum(pos - w // 2, 0)).astype(F32)
        pooled = (win / cnt - cur[:, lanes]).astype(BF16)
        outs.append((_dot(pooled, pproj_ref[g]) + pb_ref[:, lanes]) * ps_ref[:, lanes])
    pooled_out = _dot(jnp.concatenate(outs, axis=1).astype(BF16), wp_ref[...])

    gs = jax.nn.sigmoid(gt_ref[0].astype(F32) + bg_ref[...])
    merged = gs[:, 0:d] * a + gs[:, d:2 * d] * dd + gs[:, 2 * d:3 * d] * pooled_out
    o_ref[0] = x + gate * _dot(merged.astype(BF16), wo_ref[...])


def _mix(xs, mod, oa, od, pool_in, gates, w, *, l, mod_row, x_tiles):
    b, nt, d = xs.shape
    hb = TM // POOL_HALO
    n_halo = nt // POOL_HALO
    tile = pl.BlockSpec((1, TM, d), lambda i, t: (i, t, 0))
    pw = 4 * POOL_G
    weights = [w["bgate"], w["pproj"], w["pb"], w["ps"], w["wa"], w["wd"], w["wp"], w["wo"]]

    def o_spec(o):
        r = o.shape[3] // TM
        return pl.BlockSpec((1, 1, o.shape[2], TM), lambda i, t: (i, t // r, 0, t % r))

    return pl.pallas_call(
        functools.partial(_mix_kernel, x_tiles=x_tiles),
        grid=(b, nt // TM),
        in_specs=[tile, _mod_spec(mod, l, mod_row),
                  o_spec(oa), o_spec(od),
                  pl.BlockSpec((1, TM, pw), lambda i, t: (i, t, 0)),
                  pl.BlockSpec((1, POOL_HALO, pw), lambda i, t: (i, jnp.maximum(t * hb - 1, 0), 0)),
                  pl.BlockSpec((1, POOL_HALO, pw), lambda i, t: (i, jnp.minimum((t + 1) * hb, n_halo - 1), 0)),
                  pl.BlockSpec((1, TM, 3 * d), lambda i, t: (i, t, 0))]
                 + [_layer_spec(a, l) for a in weights],
        out_specs=tile,
        out_shape=jax.ShapeDtypeStruct(xs.shape, F32),
        input_output_aliases={0: 0},
        compiler_params=_params(),
        name="mix",
    )(xs, mod, oa, od, pool_in, pool_in, pool_in, gates, *weights)


def _rope_tables(seq, ctx):
    rows = seq // GRID_W
    row_ids = jnp.repeat(jnp.arange(rows), GRID_W).astype(F32)
    col_ids = jnp.tile(jnp.arange(GRID_W), rows).astype(F32)
    n_freq = ROPE_HALF // 2
    inv_freq = ROPE_THETA ** (-jnp.arange(n_freq, dtype=F32) / n_freq)
    ang = jnp.concatenate([row_ids[:, None] * inv_freq, col_ids[:, None] * inv_freq], axis=-1)
    cos = jnp.concatenate([jnp.cos(ang), jnp.ones((ctx, ROPE_HALF), F32)], axis=0)
    sin = jnp.concatenate([jnp.sin(ang), jnp.zeros((ctx, ROPE_HALF), F32)], axis=0)
    pad = jnp.zeros((seq + ctx, LANES - 2 * ROPE_HALF), F32)
    return {
        "cosT": jnp.tile(cos.T, (HEADS, 1)), "sinT": jnp.tile(sin.T, (HEADS, 1)),
        "cosN": jnp.tile(cos, (1, HEADS)), "sinN": jnp.tile(sin, (1, HEADS)),
        "c128": jnp.concatenate([cos, cos, pad], axis=1),
        "s128": jnp.concatenate([sin, sin, pad], axis=1),
    }


def _placement_matrices():
    hw = ROPE_HALF
    perm = np.zeros((HEADS * 2 * DIFF_HD, HEADS * 2 * DIFF_HD), np.float32)
    for h in range(HEADS):
        for c in range(2):
            for half in range(2):
                for f in range(hw):
                    perm[(c * 2 + half) * LANES + h * hw + f, h * 2 * DIFF_HD + c * DIFF_HD + half * hw + f] = 1.0
    place = np.zeros((LANES, HEADS * LANES), np.float32)
    for h in range(HEADS):
        for f in range(MLA_ROPE):
            place[f, h * LANES + MLA_NOPE + f] = 1.0
    return jnp.asarray(place, BF16), jnp.asarray(perm, BF16)


def _stacked_weights(p):
    w_in = p["w_in"]
    nl, d, _ = w_in.shape
    hw = ROPE_HALF
    o_cq, o_ckv, o_kr = 0, 384, 640
    o_dq, o_dk, o_dv, o_pool, o_gate = 672, 1184, 1696, 2208, 2720
    bf = lambda a: a.astype(BF16)
    t = lambda a: jnp.swapaxes(a, 1, 2)

    def block_order(cols):
        a = cols.reshape(nl, d, HEADS, 2, 2, hw)
        return a.transpose(0, 1, 3, 4, 2, 5).reshape(nl, d, HEADS * 2 * DIFF_HD)

    wkr = w_in[:, :, o_kr:o_kr + MLA_ROPE]
    wkr_rot = jnp.concatenate([-wkr[:, :, hw:], wkr[:, :, :hw]], axis=2)
    zpad = jnp.zeros((nl, d, LANES - MLA_ROPE), F32)
    wkr2 = jnp.concatenate([wkr, zpad, wkr_rot, zpad], axis=2)

    w_uq = p["mla_w_uq"].reshape(nl, -1, HEADS, MLA_QK)
    lora = w_uq.shape[1]
    wuq = jnp.concatenate([w_uq[..., :MLA_NOPE].reshape(nl, lora, -1),
                           w_uq[..., MLA_NOPE:MLA_NOPE + hw].reshape(nl, lora, -1),
                           w_uq[..., MLA_NOPE + hw:].reshape(nl, lora, -1)], axis=2)
    w_ukv = p["mla_w_ukv"].reshape(nl, -1, HEADS, MLA_NOPE + MLA_V)
    kv_lora = w_ukv.shape[1]
    is_key = (jnp.arange(MLA_NOPE + MLA_V) < MLA_NOPE).astype(F32)
    wkpad = (w_ukv * is_key).reshape(nl, kv_lora, -1)
    wv = w_ukv[..., MLA_NOPE:].reshape(nl, kv_lora, -1)
    place, perm = _placement_matrices()
    return {
        "wcq": bf(w_in[:, :, o_cq:o_ckv]), "wckv": bf(w_in[:, :, o_ckv:o_kr]), "wkr2": bf(wkr2),
        "wdqT": bf(t(block_order(w_in[:, :, o_dq:o_dk]))), "wdk": bf(block_order(w_in[:, :, o_dk:o_dv])),
        "wdvT": bf(t(w_in[:, :, o_dv:o_pool])), "wpool": bf(w_in[:, :, o_pool:o_gate]), "wgates": bf(w_in[:, :, o_gate:]),
        "qng": p["mla_q_norm_g"][:, None], "kvng": p["mla_kv_norm_g"][:, None],
        "wuqT": bf(t(wuq)), "wkpad": bf(wkpad), "wvT": bf(t(wv)),
        "place": place, "perm": perm,
        "bgate": p["b_gate"].reshape(nl, 1, -1), "pproj": bf(p["pool_proj"]),
        "pb": p["pool_b"].reshape(nl, 1, -1), "ps": p["pool_scale"][:, None],
        "wa": bf(p["w_br_mla"]), "wd": bf(p["w_br_diff"]), "wp": bf(p["w_br_pool"]), "wo": bf(p["w_out"]),
    }


def kernel(x, c, ctx, c_ctx, ada_w, ada_b, norm_g, ffa_w_gate, ffa_w_up, ffa_w_down, ffb_w_gate, ffb_w_up, ffb_w_down, w_in, b_gate, mla_q_norm_g, mla_kv_norm_g, mla_w_uq, mla_w_ukv, diff_lambda, diff_subln_g, pool_proj, pool_b, pool_scale, w_br_mla, w_br_diff, w_br_pool, w_out, final_g):
    b, seq, d = x.shape
    n_ctx = ctx.shape[1]
    depth = ada_w.shape[0]
    nt = seq + n_ctx
    nsub = ATT_NSUB if seq % (ATT_NSUB * TQ_MLA) == 0 else 1
    assert n_ctx == TM and seq % (nsub * TQ_MLA) == 0 and seq % GRID_W == 0 and b + 1 <= 8
    x_tiles = seq // TM
    p = dict(w_in=w_in, b_gate=b_gate, mla_q_norm_g=mla_q_norm_g, mla_kv_norm_g=mla_kv_norm_g,
             mla_w_uq=mla_w_uq, mla_w_ukv=mla_w_ukv, pool_proj=pool_proj, pool_b=pool_b,
             pool_scale=pool_scale, w_br_mla=w_br_mla, w_br_diff=w_br_diff, w_br_pool=w_br_pool, w_out=w_out)

    c_rows = jnp.concatenate([c, c_ctx[None], jnp.zeros((8 - b - 1, d), F32)], axis=0)
    mod = _modulation(c_rows, ada_w, ada_b).reshape(depth, 8, N_ADA, d)
    mod_row = lambda i, t: jnp.where(t == x_tiles, b, i)

    tables = _rope_tables(seq, n_ctx)
    w = _stacked_weights(p)
    gains = norm_g[:, :, None, :]
    bf = lambda a: a.astype(BF16)
    ffa = (bf(ffa_w_gate), bf(ffa_w_up), bf(ffa_w_down))
    ffb = (bf(ffb_w_gate), bf(ffb_w_up), bf(ffb_w_down))
    extra = (diff_lambda, diff_subln_g[:, :, None])
    xs = jnp.concatenate([x, ctx], axis=1)
    chunks = _key_chunks(nt)

    for l in range(depth):
        lam_init = 0.8 - 0.6 * math.exp(-0.3 * l)
        xs = _ffn(xs, mod, gains, *ffa, l=l, mi=0, mod_row=mod_row)
        q_mla, k_mla, v_mla, q_diff, k_diff, v_diff, pool_in, gates, kn_mla, kn_diff = _inproj(
            xs, mod, gains, w, tables, l=l, mod_row=mod_row)
        mla = (q_mla, k_mla, v_mla, kn_mla)
        dif = (q_diff, k_diff, v_diff, kn_diff)
        full = dict(nsub=nsub, q_block0=0, k_rows=nt, k_block0=0, chunks=chunks)
        ctx_only = dict(tq=TM, nsub=1, steps=1, k_rows=TM, k_block0=x_tiles, chunks=((0, TM),))
        oa = _attention(*mla, tq=TQ_MLA, steps=seq // (nsub * TQ_MLA), diff=False, **full)
        oa = _attention(*mla, q_block0=seq // TQ_MLA, diff=False, prev_out=oa, **ctx_only)
        od = _attention(*dif, tq=TQ_DIFF, steps=seq // (nsub * TQ_DIFF), diff=True,
                        extra=extra, layer=l, lam_init=lam_init, **full)
        od = _attention(*dif, q_block0=seq // TQ_DIFF, diff=True, extra=extra, layer=l,
                        lam_init=lam_init, prev_out=od, **ctx_only)
        xs = _mix(xs, mod, oa, od, pool_in, gates, w, l=l, mod_row=mod_row, x_tiles=x_tiles)
        if l + 1 < depth:
            xs = _ffn(xs, mod, gains, *ffb, l=l, mi=2, mod_row=mod_row)
    return _ffn(xs, mod, gains, *ffb, l=depth - 1, mi=2, mod_row=mod_row, final=(final_g[None], x_tiles))[1]
```

```python
import functools
import math

import numpy as np
import jax
import jax.numpy as jnp
from jax import lax
from jax.experimental import pallas as pl
from jax.experimental.pallas import tpu as pltpu

F32 = jnp.float32
BF16 = jnp.bfloat16

EPS = 1e-6
ROPE_THETA = 10000.0
GRID_W = 64
N_ADA = 9

HEADS = 8
MLA_NOPE = 64
MLA_ROPE = 32
MLA_V = 64
V_ROWS = MLA_V + 16
MLA_QK = MLA_NOPE + MLA_ROPE
LOG2E = math.log2(math.e)
MLA_SCALE = MLA_QK ** -0.5 * LOG2E
DIFF_HD = 32
DIFF_V = 64
DIFF_SCALE = DIFF_HD ** -0.5 * LOG2E
ROPE_HALF = 16
POOL_WINDOWS = (2, 4, 8, 16)
POOL_G = 128
POOL_HALO = 16

LANES = 128
TM = 256
TK = 512
CHUNK_HEAD = (256,)
CHUNK_TAIL = ()
ATT_AHEAD = 2
ATT_ROWS = 256
ATT_BOUND_MAX = 40.0
ATT_NSUB = 4
TQ_MLA = 1024
TQ_DIFF = 512
VMEM_LIMIT = 52 * 1024 * 1024

ATT_FLAGS = {}

NT_DIMS = (((1,), (1,)), ((), ()))
TN_DIMS = (((0,), (0,)), ((), ()))


def _params():
    return pltpu.CompilerParams(vmem_limit_bytes=VMEM_LIMIT)


def _const_spec(shape):
    zeros = (0,) * len(shape)
    return pl.BlockSpec(shape, lambda *_: zeros, pipeline_mode=pl.Buffered(1))


def _layer_spec(a, *lead):
    tail = a.shape[len(lead):]
    zeros = (0,) * len(tail)
    return pl.BlockSpec((None,) * len(lead) + tail, lambda *_: tuple(lead) + zeros, pipeline_mode=pl.Buffered(1))


def _mod_spec(mod, l, mod_row):
    return pl.BlockSpec((None, 1) + mod.shape[2:], lambda i, t: (l, mod_row(i, t), 0, 0))


def _dot(a, b):
    return jnp.dot(a, b, preferred_element_type=F32)


def _rms(x):
    return x * lax.rsqrt(jnp.mean(x * x, axis=-1, keepdims=True) + EPS)


def _mod_kernel(c_ref, w_ref, b_ref, o_ref):
    c = c_ref[...]
    a = c * jax.nn.sigmoid(c)
    o_ref[0] = jnp.dot(a, w_ref[0], preferred_element_type=F32,
                       precision=lax.Precision.HIGHEST) + b_ref[0]


def _modulation(c_rows, ada_w, ada_b):
    depth, d, n = ada_w.shape
    bn = n // 8
    return pl.pallas_call(
        _mod_kernel,
        grid=(depth, n // bn),
        in_specs=[pl.BlockSpec((8, d), lambda l, j: (0, 0)),
                  pl.BlockSpec((1, d, bn), lambda l, j: (l, 0, j)),
                  pl.BlockSpec((1, 1, bn), lambda l, j: (l, 0, j))],
        out_specs=pl.BlockSpec((1, 8, bn), lambda l, j: (l, 0, j)),
        out_shape=jax.ShapeDtypeStruct((depth, 8, n), F32),
        compiler_params=_params(),
        name="modulation",
    )(c_rows, ada_w, ada_b.reshape(depth, 1, n))


def _ffn_kernel(x_ref, mod_ref, g_ref, wg_ref, wu_ref, wd_ref, *rest, mi, f_chunks, x_tiles, ctx_tile):
    o_ref = rest[-1] if x_tiles is None else rest[-2]
    x = x_ref[0]
    if ctx_tile is not None:
        x = jnp.where(pl.program_id(1) == ctx_tile, rest[0][0], x)
    m = mod_ref[0]
    shift, scale, gate = m[3 * mi:3 * mi + 1], m[3 * mi + 1:3 * mi + 2], m[3 * mi + 2:3 * mi + 3]
    u = (_rms(x) * g_ref[...]) * (1.0 + scale) + shift
    ub = u.astype(BF16)
    y = jnp.zeros(x.shape, F32)
    for lo, hi in f_chunks:
        a = _dot(ub, wg_ref[:, lo:hi])
        b = _dot(ub, wu_ref[:, lo:hi])
        hid = (a * jax.nn.sigmoid(a) * b).astype(BF16)
        y = y + _dot(hid, wd_ref[lo:hi, :])
    new = x + (0.5 * gate) * y
    o_ref[0] = new
    if x_tiles is not None:
        fg_ref, final_ref = rest[0], rest[-1]

        @pl.when(pl.program_id(1) < x_tiles)
        def _():
            final_ref[0] = _rms(new) * fg_ref[...]


def _ffn(xs, mod, gains, wg, wu, wd, *, l, mi, mod_row, final=None, ctx=None):
    b, nt, d = xs.shape
    f = wg.shape[2]
    chunk = 1024
    f_chunks = tuple((lo, min(lo + chunk, f)) for lo in range(0, f, chunk))
    tile = pl.BlockSpec((1, TM, d), lambda i, t: (i, t, 0))
    in_specs = [tile, _mod_spec(mod, l, mod_row), _layer_spec(gains, l, mi),
                _layer_spec(wg, l), _layer_spec(wu, l), _layer_spec(wd, l)]
    args = [xs, mod, gains, wg, wu, wd]
    aliases, ctx_tile = {0: 0}, None
    if ctx is not None:
        ctx_tile = nt // TM
        nt += ctx.shape[1]
        in_specs[0] = pl.BlockSpec((1, TM, d), lambda i, t: (i, jnp.minimum(t, ctx_tile - 1), 0))
        in_specs.append(pl.BlockSpec((1, TM, d), lambda i, t: (i, 0, 0)))
        args.append(ctx)
        aliases = {}
    out_specs, out_shape, x_tiles = tile, jax.ShapeDtypeStruct((b, nt, d), F32), None
    if final is not None:
        fg, x_tiles = final
        in_specs.append(_const_spec(fg.shape))
        args.append(fg)
        out_specs = [tile, pl.BlockSpec((1, TM, d), lambda i, t: (i, jnp.minimum(t, x_tiles - 1), 0))]
        out_shape = [out_shape, jax.ShapeDtypeStruct((b, x_tiles * TM, d), F32)]
    return pl.pallas_call(
        functools.partial(_ffn_kernel, mi=mi, f_chunks=f_chunks, x_tiles=x_tiles, ctx_tile=ctx_tile),
        grid=(b, nt // TM),
        in_specs=in_specs,
        out_specs=out_specs,
        out_shape=out_shape,
        input_output_aliases=aliases,
        compiler_params=_params(),
        name=f"ffn{mi}",
    )(*args)


def _inproj_kernel(x_ref, mod_ref, g_ref, wcq, wckv, wkr2, wdqT, wdk, wdvT, wpool, wgates,
                   qng, kvng, wuqT, wkpad, wvT, place, perm,
                   cosT, sinT, cosN, sinN, c128, s128,
                   oq_mla, ok_mla, ov_mla, oq_diff, ok_diff, ov_diff, opool, ogates, okn_mla, okn_diff):
    x = x_ref[0]
    m = mod_ref[0]
    u = (_rms(x) * g_ref[...]) * (1.0 + m[4:5]) + m[3:4]
    ub = u.astype(BF16)
    tm = x.shape[0]
    cT, sT = cosT[...], sinT[...]
    hw = ROPE_HALF

    cqn = (_rms(_dot(ub, wcq[...])) * qng[...]).astype(BF16)
    qT = lax.dot_general(wuqT[...], cqn, NT_DIMS, preferred_element_type=F32) * MLA_SCALE
    nope_w = HEADS * MLA_NOPE
    x1, x2 = qT[nope_w:nope_w + LANES], qT[nope_w + LANES:nope_w + 2 * LANES]
    qn = qT[0:nope_w].astype(BF16)
    r1 = (x1 * cT - x2 * sT).astype(BF16)
    r2 = (x1 * sT + x2 * cT).astype(BF16)
    zpad = jnp.zeros((LANES - MLA_QK, tm), BF16)
    for h in range(HEADS):
        oq_mla[0, h, 0, 0:MLA_NOPE, :] = qn[h * MLA_NOPE:(h + 1) * MLA_NOPE]
        oq_mla[0, h, 0, MLA_NOPE:MLA_NOPE + hw, :] = r1[h * hw:(h + 1) * hw]
        oq_mla[0, h, 0, MLA_NOPE + hw:MLA_QK, :] = r2[h * hw:(h + 1) * hw]
        oq_mla[0, h, 0, MLA_QK:LANES, :] = zpad

    ckvn = (_rms(_dot(ub, wckv[...])) * kvng[...]).astype(BF16)
    kr2 = _dot(ub, wkr2[...])
    krr = (kr2[:, 0:LANES] * c128[...] + kr2[:, LANES:2 * LANES] * s128[...]).astype(BF16)
    kall = (_dot(ckvn, wkpad[...]) + _dot(krr, place[...])).astype(BF16)
    vT = lax.dot_general(wvT[...], ckvn, NT_DIMS, preferred_element_type=F32).astype(BF16)
    head_row = lax.broadcasted_iota(jnp.int32, (HEADS, LANES), 0)
    left_half = lax.broadcasted_iota(jnp.int32, (tm, LANES), 1) < 2 * DIFF_HD

    def max_sq_norm(sq):
        return jnp.max(jnp.sum(sq, axis=1, keepdims=True), axis=0, keepdims=True)

    kn = jnp.zeros((HEADS, LANES), F32)
    for h in range(HEADS):
        kh = kall[:, h * LANES:(h + 1) * LANES]
        ok_mla[0, h] = kh
        khf = kh.astype(F32)
        kn = jnp.where(head_row == h, max_sq_norm(khf * khf), kn)
    okn_mla[0, 0] = kn
    ones_rows = (lax.broadcasted_iota(jnp.int32, (V_ROWS - MLA_V, tm), 0) == 0).astype(BF16)

    def store_values(ov, vals):
        for h in range(HEADS):
            ov[0, h * V_ROWS:h * V_ROWS + MLA_V, :] = vals[h * MLA_V:(h + 1) * MLA_V]
            ov[0, h * V_ROWS + MLA_V:(h + 1) * V_ROWS, :] = ones_rows

    store_values(ov_mla, vT)

    dqT = lax.dot_general(wdqT[...], ub, NT_DIMS, preferred_element_type=F32) * DIFF_SCALE
    a1, a2, b1, b2 = (dqT[i * LANES:(i + 1) * LANES] for i in range(4))
    parts = [(a1 * cT - a2 * sT).astype(BF16), (a1 * sT + a2 * cT).astype(BF16),
             (b1 * cT - b2 * sT).astype(BF16), (b1 * sT + b2 * cT).astype(BF16)]
    zhalf = jnp.zeros((2 * DIFF_HD, tm), BF16)
    for h in range(HEADS):
        base = (h % 2) * 2 * DIFF_HD
        for i, part in enumerate(parts):
            oq_diff[0, h, 0, base + i * hw:base + (i + 1) * hw, :] = part[h * hw:(h + 1) * hw]
        other = 2 * DIFF_HD - base
        oq_diff[0, h, 0, other:other + 2 * DIFF_HD, :] = zhalf

    dk = _dot(ub, wdk[...])
    cN, sN = cosN[...], sinN[...]
    k1a, k1b, k2a, k2b = (dk[:, i * LANES:(i + 1) * LANES] for i in range(4))
    rk = jnp.concatenate([k1a * cN - k1b * sN, k1a * sN + k1b * cN,
                          k2a * cN - k2b * sN, k2a * sN + k2b * cN], axis=1).astype(BF16)
    kd = _dot(rk, perm[...]).astype(BF16)
    kn = jnp.zeros((HEADS, LANES), F32)
    for p in range(HEADS // 2):
        kp = kd[:, p * LANES:(p + 1) * LANES]
        ok_diff[0, p] = kp
        kpf = kp.astype(F32)
        sq = kpf * kpf
        kn = jnp.where(head_row == 2 * p, max_sq_norm(jnp.where(left_half, sq, 0.0)), kn)
        kn = jnp.where(head_row == 2 * p + 1, max_sq_norm(jnp.where(left_half, 0.0, sq)), kn)
    okn_diff[0, 0] = kn
    store_values(ov_diff, lax.dot_general(wdvT[...], ub, NT_DIMS, preferred_element_type=F32).astype(BF16))

    opool[0] = _dot(ub, wpool[...])
    ogates[0] = _dot(ub, wgates[...]).astype(BF16)


def _inproj(xs, mod, gains, w, tables, *, l, mod_row):
    b, nt, d = xs.shape
    tile = pl.BlockSpec((1, TM, d), lambda i, t: (i, t, 0))
    tabT = pl.BlockSpec((LANES, TM), lambda i, t: (0, t))
    tabN = pl.BlockSpec((TM, LANES), lambda i, t: (t, 0))
    stacked = [w["wcq"], w["wckv"], w["wkr2"], w["wdqT"], w["wdk"], w["wdvT"], w["wpool"], w["wgates"],
               w["qng"], w["kvng"], w["wuqT"], w["wkpad"], w["wvT"]]
    shared = [w["place"], w["perm"]]
    vT_spec = pl.BlockSpec((1, HEADS * V_ROWS, TM), lambda i, t: (i, 0, t))

    def q_spec(tq):
        r = tq // TM
        return pl.BlockSpec((1, HEADS, 1, LANES, TM), lambda i, t: (i, 0, t // r, 0, t % r))

    out_shapes = [
        jax.ShapeDtypeStruct((b, HEADS, pl.cdiv(nt, TQ_MLA), LANES, TQ_MLA), BF16),
        jax.ShapeDtypeStruct((b, HEADS, nt, LANES), BF16),
        jax.ShapeDtypeStruct((b, HEADS * V_ROWS, nt), BF16),
        jax.ShapeDtypeStruct((b, HEADS, pl.cdiv(nt, TQ_DIFF), LANES, TQ_DIFF), BF16),
        jax.ShapeDtypeStruct((b, HEADS // 2, nt, LANES), BF16),
        jax.ShapeDtypeStruct((b, HEADS * V_ROWS, nt), BF16),
        jax.ShapeDtypeStruct((b, nt, 4 * POOL_G), F32),
        jax.ShapeDtypeStruct((b, nt, 3 * d), BF16),
        jax.ShapeDtypeStruct((b, nt // TM, HEADS, LANES), F32),
        jax.ShapeDtypeStruct((b, nt // TM, HEADS, LANES), F32),
    ]
    kn_spec = pl.BlockSpec((1, 1, HEADS, LANES), lambda i, t: (i, t, 0, 0))
    out_specs = [
        q_spec(TQ_MLA),
        pl.BlockSpec((1, HEADS, TM, LANES), lambda i, t: (i, 0, t, 0)),
        vT_spec,
        q_spec(TQ_DIFF),
        pl.BlockSpec((1, HEADS // 2, TM, LANES), lambda i, t: (i, 0, t, 0)),
        vT_spec,
        pl.BlockSpec((1, TM, 4 * POOL_G), lambda i, t: (i, t, 0)),
        pl.BlockSpec((1, TM, 3 * d), lambda i, t: (i, t, 0)),
        kn_spec, kn_spec,
    ]
    return pl.pallas_call(
        _inproj_kernel,
        grid=(b, nt // TM),
        in_specs=([tile, _mod_spec(mod, l, mod_row), _layer_spec(gains, l, 1)]
                  + [_layer_spec(a, l) for a in stacked] + [_const_spec(a.shape) for a in shared]
                  + [tabT, tabT, tabN, tabN, tabN, tabN]),
        out_specs=out_specs,
        out_shape=out_shapes,
        compiler_params=_params(),
        name="inproj",
    )(xs, mod, gains, *stacked, *shared, tables["cosT"], tables["sinT"], tables["cosN"], tables["sinN"],
      tables["c128"], tables["s128"])


def _attn_kernel(*refs, chunks, diff, lam_init, aliased):
    refs = list(refs)
    q_ref, k_ref, v_ref, kn_ref = refs[:4]
    n_buf = ATT_AHEAD + 1
    s_bufs = refs[-n_buf:]
    o_ref = refs[-n_buf - 1]
    assert len(refs) == 5 + 2 * diff + aliased + n_buf
    nsub, tq = q_ref.shape[2], q_ref.shape[4]
    nc = len(chunks)
    head = pl.program_id(1)

    tile_max = jnp.max(kn_ref[0], axis=0)
    rows = lax.broadcasted_iota(jnp.int32, tile_max.shape, 0)
    kmax2 = jnp.max(jnp.where(rows == head, tile_max, 0.0), axis=0, keepdims=True)[:, 0:1]

    def finish(num, den, sub):
        o = num / den
        if diff:
            dl_ref, sg_ref = refs[4:6]
            dl = dl_ref[...]
            lam = (jnp.exp(jnp.sum(dl[0:1] * dl[1:2], axis=1, keepdims=True))
                   - jnp.exp(jnp.sum(dl[2:3] * dl[3:4], axis=1, keepdims=True)) + lam_init)
            o = o[:, :tq] - lam * o[:, tq:]
            o = o * lax.rsqrt(jnp.mean(o * o, axis=0, keepdims=True) + EPS)
            o = o * sg_ref[...] * (1.0 - lam_init)
        o_ref[0, sub] = o.astype(o_ref.dtype)

    def bounded_shift(rhs, bound, sub):
        acc = jnp.zeros((MLA_V, rhs.shape[1]), F32)
        den = jnp.zeros((1, rhs.shape[1]), F32)
        pieces = lambda c: [(lo, min(lo + ATT_ROWS, chunks[c][1])) for lo in range(*chunks[c], ATT_ROWS)]
        logits = lambda lo, hi: _dot(k_ref[0, 0, lo:hi, :], rhs)
        s_next = [logits(lo, hi) for lo, hi in pieces(0)]
        for c in range(nc):
            s_cur, s_next = s_next, []
            nxt = pieces(c + 1) if c + 1 < nc else []
            for i, (lo, hi) in enumerate(pieces(c)):
                if i < len(nxt):
                    s_next.append(logits(*nxt[i]))
                p = jnp.exp2(s_cur[i] - bound)
                den = den + jnp.sum(p, axis=0, keepdims=True)
                acc = acc + _dot(v_ref[0, 0:MLA_V, lo:hi], p.astype(BF16))
            s_next += [logits(lo, hi) for lo, hi in nxt[len(pieces(c)):]]
        finish(acc, den, sub)

    def running_max(rhs, sub, zero_row):
        n = rhs.shape[1]

        def scores(c):
            lo, hi = chunks[c]
            s = _dot(k_ref[0, 0, lo:hi, :], rhs)
            s_bufs[c % n_buf][pl.ds(zero_row, hi - lo), :] = s
            return jnp.max(s, axis=0, keepdims=True)

        m = jnp.full((1, n), -1e30, F32)
        acc = jnp.zeros((V_ROWS, n), F32)
        cmaxes = {c: scores(c) for c in range(min(ATT_AHEAD, nc))}
        for c in range(nc):
            lo, hi = chunks[c]
            m_new = jnp.maximum(m, cmaxes.pop(c))
            alpha = jnp.exp2(m - m_new)
            if c + ATT_AHEAD < nc:
                cmaxes[c + ATT_AHEAD] = scores(c + ATT_AHEAD)
            p = jnp.exp2(s_bufs[c % n_buf][pl.ds(zero_row, hi - lo), :] - m_new)
            m, acc = m_new, alpha * acc + _dot(v_ref[0, :, lo:hi], p.astype(BF16))
        finish(acc[0:MLA_V], acc[MLA_V:MLA_V + 1], sub)

    def queries(sub):
        q = q_ref[0, 0, sub]
        if diff:
            first = (lax.broadcasted_iota(jnp.int32, q.shape, 0) & DIFF_HD) == 0
            zero = jnp.zeros_like(q)
            rhs = jnp.concatenate([jnp.where(first, q, zero), jnp.where(first, zero, q)], axis=1)
        else:
            rhs = q
        qf = rhs.astype(F32)
        return rhs, jnp.sqrt(jnp.sum(qf * qf, axis=0, keepdims=True) * kmax2)

    def for_each_block(fn):
        if nsub == 1:
            fn(0)
        else:
            lax.fori_loop(0, nsub, lambda sub, carry: (fn(sub), carry)[1], 0)

    qa = q_ref[0, 0].astype(F32)
    q_norm2 = jnp.max(jnp.sum(qa * qa, axis=1, keepdims=True))
    small = q_norm2 * jnp.max(kmax2) <= ATT_BOUND_MAX * ATT_BOUND_MAX

    @pl.when(small)
    def _():
        for_each_block(lambda sub: bounded_shift(*queries(sub), sub))

    @pl.when(jnp.logical_not(small))
    def _():
        zero_row = pl.multiple_of(jnp.minimum(pl.program_id(2), 0), 16)
        for_each_block(lambda sub: running_max(queries(sub)[0], sub, zero_row))


def _key_chunks(nk):
    head = [c for c in CHUNK_HEAD if sum(CHUNK_HEAD) + sum(CHUNK_TAIL) + TK <= nk]
    tail = CHUNK_TAIL if head else ()
    sizes = list(head)
    body = nk - sum(head) - sum(tail)
    sizes += [TK] * (body // TK) + ([body % TK] if body % TK else [])
    sizes += list(tail)
    edges = np.cumsum([0] + sizes)
    assert edges[-1] == nk and all(s % LANES == 0 for s in sizes)
    return tuple((int(a), int(b)) for a, b in zip(edges[:-1], edges[1:]))


def _attention(qT, k, vT, kn, *, tq, nsub, steps, q_block0, k_rows, k_block0, chunks,
               diff, extra=(), layer=0, lam_init=0.0, prev_out=None):
    b, heads, nqb, _, tqw = qT.shape
    dv = MLA_V
    kdiv = 2 if diff else 1
    in_specs = [
        pl.BlockSpec((1, 1, nsub, LANES, tq), lambda i, h, j: (i, h, q_block0 + j, 0, 0)),
        pl.BlockSpec((1, 1, k_rows, LANES), lambda i, h, j: (i, h // kdiv, k_block0, 0)),
        pl.BlockSpec((1, V_ROWS, k_rows), lambda i, h, j: (i, h, k_block0)),
        pl.BlockSpec((1,) + kn.shape[1:], lambda i, h, j: (i, 0, 0, 0)),
    ]
    args = [qT, k, vT, kn]
    if diff:
        in_specs += [_layer_spec(a, layer) for a in extra]
        args += list(extra)
    aliases = {}
    if prev_out is not None:
        in_specs.append(pl.BlockSpec(memory_space=pl.ANY))
        args.append(prev_out)
        aliases = {len(args) - 1: 0}
    return pl.pallas_call(
        functools.partial(_attn_kernel, chunks=chunks, diff=diff, lam_init=lam_init,
                          aliased=prev_out is not None),
        grid=(b, heads, steps),
        in_specs=in_specs,
        out_specs=pl.BlockSpec((1, nsub, dv, tq), lambda i, h, j: (i, q_block0 + j, h, 0)),
        out_shape=jax.ShapeDtypeStruct((b, nqb, heads * dv, tqw), BF16),
        scratch_shapes=[pltpu.VMEM((max(hi - lo for lo, hi in chunks), (2 * tq if diff else tq)), dt)
                        for dt in [F32] * (ATT_AHEAD + 1)],
        input_output_aliases=aliases,
        compiler_params=pltpu.CompilerParams(vmem_limit_bytes=VMEM_LIMIT, flags=ATT_FLAGS),
        name=("diff" if diff else "mla") + ("_ctx" if prev_out is not None else "_x"),
    )(*args)


def _mix_kernel(x_ref, mod_ref, oa_ref, od_ref, pc_ref, pp_ref, pn_ref, gt_ref, bg_ref,
                pproj_ref, pb_ref, ps_ref, wa_ref, wd_ref, wp_ref, wo_ref, o_ref, *, x_tiles):
    t = pl.program_id(1)
    x = x_ref[0]
    gate = mod_ref[0][5:6]
    tm, d = x.shape
    a = lax.dot_general(oa_ref[0, 0], wa_ref[...], TN_DIMS, preferred_element_type=F32)
    dd = lax.dot_general(od_ref[0, 0], wd_ref[...], TN_DIMS, preferred_element_type=F32)

    is_ctx = t == x_tiles
    has_prev = jnp.logical_and(t != 0, jnp.logical_not(is_ctx))
    has_next = jnp.logical_and(t != x_tiles - 1, jnp.logical_not(is_ctx))
    cur = pc_ref[0]
    prev = jnp.where(has_prev, pp_ref[0], 0.0)
    nxt = jnp.where(has_next, pn_ref[0], 0.0)
    ext = jnp.concatenate([prev, cur, nxt], axis=0)
    rows = ext.shape[0]
    seq_len = jnp.where(is_ctx, tm, x_tiles * tm)
    pos = jnp.where(is_ctx, 0, t * tm) + lax.broadcasted_iota(jnp.int32, (tm, 1), 0)
    outs = []
    for g, w in enumerate(POOL_WINDOWS):
        lanes = slice(g * POOL_G, (g + 1) * POOL_G)
        run = ext[:, lanes]
        span = 1
        while span < w:
            run = run + pltpu.roll(run, rows - span, axis=0)
            span *= 2
        win = pltpu.roll(run, rows - (POOL_HALO - w // 2), axis=0)[0:tm]
        cnt = (jnp.minimum(pos + w // 2, seq_len) - jnp.maximum(pos - w // 2, 0)).astype(F32)
        pooled = (win / cnt - cur[:, lanes]).astype(BF16)
        outs.append((_dot(pooled, pproj_ref[g]) + pb_ref[:, lanes]) * ps_ref[:, lanes])
    pooled_out = _dot(jnp.concatenate(outs, axis=1).astype(BF16), wp_ref[...])

    gs = jax.nn.sigmoid(gt_ref[0].astype(F32) + bg_ref[...])
    merged = gs[:, 0:d] * a + gs[:, d:2 * d] * dd + gs[:, 2 * d:3 * d] * pooled_out
    o_ref[0] = x + gate * _dot(merged.astype(BF16), wo_ref[...])


def _mix(xs, mod, oa, od, pool_in, gates, w, *, l, mod_row, x_tiles):
    b, nt, d = xs.shape
    hb = TM // POOL_HALO
    n_halo = nt // POOL_HALO
    tile = pl.BlockSpec((1, TM, d), lambda i, t: (i, t, 0))
    pw = 4 * POOL_G
    weights = [w["bgate"], w["pproj"], w["pb"], w["ps"], w["wa"], w["wd"], w["wp"], w["wo"]]

    def o_spec(o):
        r = o.shape[3] // TM
        return pl.BlockSpec((1, 1, o.shape[2], TM), lambda i, t: (i, t // r, 0, t % r))

    return pl.pallas_call(
        functools.partial(_mix_kernel, x_tiles=x_tiles),
        grid=(b, nt // TM),
        in_specs=[tile, _mod_spec(mod, l, mod_row),
                  o_spec(oa), o_spec(od),
                  pl.BlockSpec((1, TM, pw), lambda i, t: (i, t, 0)),
                  pl.BlockSpec((1, POOL_HALO, pw), lambda i, t: (i, jnp.maximum(t * hb - 1, 0), 0)),
                  pl.BlockSpec((1, POOL_HALO, pw), lambda i, t: (i, jnp.minimum((t + 1) * hb, n_halo - 1), 0)),
                  pl.BlockSpec((1, TM, 3 * d), lambda i, t: (i, t, 0))]
                 + [_layer_spec(a, l) for a in weights],
        out_specs=tile,
        out_shape=jax.ShapeDtypeStruct(xs.shape, F32),
        input_output_aliases={0: 0},
        compiler_params=_params(),
        name="mix",
    )(xs, mod, oa, od, pool_in, pool_in, pool_in, gates, *weights)


def _rope_tables(seq, ctx):
    rows = seq // GRID_W
    row_ids = jnp.repeat(jnp.arange(rows), GRID_W).astype(F32)
    col_ids = jnp.tile(jnp.arange(GRID_W), rows).astype(F32)
    n_freq = ROPE_HALF // 2
    inv_freq = ROPE_THETA ** (-jnp.arange(n_freq, dtype=F32) / n_freq)
    ang = jnp.concatenate([row_ids[:, None] * inv_freq, col_ids[:, None] * inv_freq], axis=-1)
    cos = jnp.concatenate([jnp.cos(ang), jnp.ones((ctx, ROPE_HALF), F32)], axis=0)
    sin = jnp.concatenate([jnp.sin(ang), jnp.zeros((ctx, ROPE_HALF), F32)], axis=0)
    pad = jnp.zeros((seq + ctx, LANES - 2 * ROPE_HALF), F32)
    return {
        "cosT": jnp.tile(cos.T, (HEADS, 1)), "sinT": jnp.tile(sin.T, (HEADS, 1)),
        "cosN": jnp.tile(cos, (1, HEADS)), "sinN": jnp.tile(sin, (1, HEADS)),
        "c128": jnp.concatenate([cos, cos, pad], axis=1),
        "s128": jnp.concatenate([sin, sin, pad], axis=1),
    }


def _placement_matrices():
    hw = ROPE_HALF
    perm = np.zeros((HEADS * 2 * DIFF_HD, HEADS * 2 * DIFF_HD), np.float32)
    for h in range(HEADS):
        for c in range(2):
            for half in range(2):
                for f in range(hw):
                    perm[(c * 2 + half) * LANES + h * hw + f, h * 2 * DIFF_HD + c * DIFF_HD + half * hw + f] = 1.0
    place = np.zeros((LANES, HEADS * LANES), np.float32)
    for h in range(HEADS):
        for f in range(MLA_ROPE):
            place[f, h * LANES + MLA_NOPE + f] = 1.0
    return jnp.asarray(place, BF16), jnp.asarray(perm, BF16)


def _stacked_weights(p):
    w_in = p["w_in"]
    nl, d, _ = w_in.shape
    hw = ROPE_HALF
    o_cq, o_ckv, o_kr = 0, 384, 640
    o_dq, o_dk, o_dv, o_pool, o_gate = 672, 1184, 1696, 2208, 2720
    bf = lambda a: a.astype(BF16)
    t = lambda a: jnp.swapaxes(a, 1, 2)

    def block_order(cols):
        a = cols.reshape(nl, d, HEADS, 2, 2, hw)
        return a.transpose(0, 1, 3, 4, 2, 5).reshape(nl, d, HEADS * 2 * DIFF_HD)

    wkr = w_in[:, :, o_kr:o_kr + MLA_ROPE]
    wkr_rot = jnp.concatenate([-wkr[:, :, hw:], wkr[:, :, :hw]], axis=2)
    zpad = jnp.zeros((nl, d, LANES - MLA_ROPE), F32)
    wkr2 = jnp.concatenate([wkr, zpad, wkr_rot, zpad], axis=2)

    w_uq = p["mla_w_uq"].reshape(nl, -1, HEADS, MLA_QK)
    lora = w_uq.shape[1]
    wuq = jnp.concatenate([w_uq[..., :MLA_NOPE].reshape(nl, lora, -1),
                           w_uq[..., MLA_NOPE:MLA_NOPE + hw].reshape(nl, lora, -1),
                           w_uq[..., MLA_NOPE + hw:].reshape(nl, lora, -1)], axis=2)
    w_ukv = p["mla_w_ukv"].reshape(nl, -1, HEADS, MLA_NOPE + MLA_V)
    kv_lora = w_ukv.shape[1]
    is_key = (jnp.arange(MLA_NOPE + MLA_V) < MLA_NOPE).astype(F32)
    wkpad = (w_ukv * is_key).reshape(nl, kv_lora, -1)
    wv = w_ukv[..., MLA_NOPE:].reshape(nl, kv_lora, -1)
    place, perm = _placement_matrices()
    return {
        "wcq": bf(w_in[:, :, o_cq:o_ckv]), "wckv": bf(w_in[:, :, o_ckv:o_kr]), "wkr2": bf(wkr2),
        "wdqT": bf(t(block_order(w_in[:, :, o_dq:o_dk]))), "wdk": bf(block_order(w_in[:, :, o_dk:o_dv])),
        "wdvT": bf(t(w_in[:, :, o_dv:o_pool])), "wpool": bf(w_in[:, :, o_pool:o_gate]), "wgates": bf(w_in[:, :, o_gate:]),
        "qng": p["mla_q_norm_g"][:, None], "kvng": p["mla_kv_norm_g"][:, None],
        "wuqT": bf(t(wuq)), "wkpad": bf(wkpad), "wvT": bf(t(wv)),
        "place": place, "perm": perm,
        "bgate": p["b_gate"].reshape(nl, 1, -1), "pproj": bf(p["pool_proj"]),
        "pb": p["pool_b"].reshape(nl, 1, -1), "ps": p["pool_scale"][:, None],
        "wa": bf(p["w_br_mla"]), "wd": bf(p["w_br_diff"]), "wp": bf(p["w_br_pool"]), "wo": bf(p["w_out"]),
    }


def kernel(x, c, ctx, c_ctx, ada_w, ada_b, norm_g, ffa_w_gate, ffa_w_up, ffa_w_down, ffb_w_gate, ffb_w_up, ffb_w_down, w_in, b_gate, mla_q_norm_g, mla_kv_norm_g, mla_w_uq, mla_w_ukv, diff_lambda, diff_subln_g, pool_proj, pool_b, pool_scale, w_br_mla, w_br_diff, w_br_pool, w_out, final_g):
    b, seq, d = x.shape
    n_ctx = ctx.shape[1]
    depth = ada_w.shape[0]
    nt = seq + n_ctx
    nsub = ATT_NSUB if seq % (ATT_NSUB * TQ_MLA) == 0 else 1
    assert n_ctx == TM and seq % (nsub * TQ_MLA) == 0 and seq % GRID_W == 0 and b + 1 <= 8
    x_tiles = seq // TM
    p = dict(w_in=w_in, b_gate=b_gate, mla_q_norm_g=mla_q_norm_g, mla_kv_norm_g=mla_kv_norm_g,
             mla_w_uq=mla_w_uq, mla_w_ukv=mla_w_ukv, pool_proj=pool_proj, pool_b=pool_b,
             pool_scale=pool_scale, w_br_mla=w_br_mla, w_br_diff=w_br_diff, w_br_pool=w_br_pool, w_out=w_out)

    c_rows = jnp.concatenate([c, c_ctx[None], jnp.zeros((8 - b - 1, d), F32)], axis=0)
    mod = _modulation(c_rows, ada_w, ada_b).reshape(depth, 8, N_ADA, d)
    mod_row = lambda i, t: jnp.where(t == x_tiles, b, i)

    tables = _rope_tables(seq, n_ctx)
    w = _stacked_weights(p)
    gains = norm_g[:, :, None, :]
    bf = lambda a: a.astype(BF16)
    ffa = (bf(ffa_w_gate), bf(ffa_w_up), bf(ffa_w_down))
    ffb = (bf(ffb_w_gate), bf(ffb_w_up), bf(ffb_w_down))
    extra = (diff_lambda, diff_subln_g[:, :, None])
    chunks = _key_chunks(nt)

    xs = x
    for l in range(depth):
        lam_init = 0.8 - 0.6 * math.exp(-0.3 * l)
        xs = _ffn(xs, mod, gains, *ffa, l=l, mi=0, mod_row=mod_row, ctx=ctx if l == 0 else None)
        q_mla, k_mla, v_mla, q_diff, k_diff, v_diff, pool_in, gates, kn_mla, kn_diff = _inproj(
            xs, mod, gains, w, tables, l=l, mod_row=mod_row)
        mla = (q_mla, k_mla, v_mla, kn_mla)
        dif = (q_diff, k_diff, v_diff, kn_diff)
        full = dict(nsub=nsub, q_block0=0, k_rows=nt, k_block0=0, chunks=chunks)
        ctx_only = dict(tq=TM, nsub=1, steps=1, k_rows=TM, k_block0=x_tiles, chunks=((0, TM),))
        oa = _attention(*mla, tq=TQ_MLA, steps=seq // (nsub * TQ_MLA), diff=False, **full)
        oa = _attention(*mla, q_block0=seq // TQ_MLA, diff=False, prev_out=oa, **ctx_only)
        od = _attention(*dif, tq=TQ_DIFF, steps=seq // (nsub * TQ_DIFF), diff=True,
                        extra=extra, layer=l, lam_init=lam_init, **full)
        od = _attention(*dif, q_block0=seq // TQ_DIFF, diff=True, extra=extra, layer=l,
                        lam_init=lam_init, prev_out=od, **ctx_only)
        xs = _mix(xs, mod, oa, od, pool_in, gates, w, l=l, mod_row=mod_row, x_tiles=x_tiles)
        if l + 1 < depth:
            xs = _ffn(xs, mod, gains, *ffb, l=l, mi=2, mod_row=mod_row)
    return _ffn(xs, mod, gains, *ffb, l=depth - 1, mi=2, mod_row=mod_row, final=(final_g[None], x_tiles))[1]
```

```python
import functools
import math

import numpy as np
import jax
import jax.numpy as jnp
from jax import lax
from jax.experimental import pallas as pl
from jax.experimental.pallas import tpu as pltpu

F32 = jnp.float32
BF16 = jnp.bfloat16

EPS = 1e-6
ROPE_THETA = 10000.0
GRID_W = 64
N_ADA = 9

HEADS = 8
MLA_NOPE = 64
MLA_ROPE = 32
MLA_V = 64
V_ROWS = MLA_V + 16
MLA_QK = MLA_NOPE + MLA_ROPE
LOG2E = math.log2(math.e)
MLA_SCALE = MLA_QK ** -0.5 * LOG2E
DIFF_HD = 32
DIFF_V = 64
DIFF_SCALE = DIFF_HD ** -0.5 * LOG2E
ROPE_HALF = 16
POOL_WINDOWS = (2, 4, 8, 16)
POOL_G = 128
POOL_HALO = 16

LANES = 128
TM = 256
TK = 512
CHUNK_HEAD = (256,)
CHUNK_TAIL = ()
ATT_AHEAD = 2
ATT_ROWS = 256
ATT_BOUND_MAX = 40.0
ATT_UNROLL = 2
ATT_NSUB = 4
TQ_MLA = 1024
TQ_DIFF = 512
VMEM_LIMIT = 52 * 1024 * 1024

ATT_FLAGS = {}

NT_DIMS = (((1,), (1,)), ((), ()))
TN_DIMS = (((0,), (0,)), ((), ()))


def _params():
    return pltpu.CompilerParams(vmem_limit_bytes=VMEM_LIMIT)


def _const_spec(shape):
    zeros = (0,) * len(shape)
    return pl.BlockSpec(shape, lambda *_: zeros, pipeline_mode=pl.Buffered(1))


def _layer_spec(a, *lead):
    tail = a.shape[len(lead):]
    zeros = (0,) * len(tail)
    return pl.BlockSpec((None,) * len(lead) + tail, lambda *_: tuple(lead) + zeros, pipeline_mode=pl.Buffered(1))


def _mod_spec(mod, l, mod_row):
    return pl.BlockSpec((None, 1) + mod.shape[2:], lambda i, t: (l, mod_row(i, t), 0, 0))


def _dot(a, b):
    return jnp.dot(a, b, preferred_element_type=F32)


def _rms(x):
    return x * lax.rsqrt(jnp.mean(x * x, axis=-1, keepdims=True) + EPS)


def _mod_kernel(c_ref, w_ref, b_ref, o_ref):
    c = c_ref[...]
    a = c * jax.nn.sigmoid(c)
    o_ref[0] = jnp.dot(a, w_ref[0], preferred_element_type=F32,
                       precision=lax.Precision.HIGHEST) + b_ref[0]


def _modulation(c_rows, ada_w, ada_b):
    depth, d, n = ada_w.shape
    bn = n // 8
    return pl.pallas_call(
        _mod_kernel,
        grid=(depth, n // bn),
        in_specs=[pl.BlockSpec((8, d), lambda l, j: (0, 0)),
                  pl.BlockSpec((1, d, bn), lambda l, j: (l, 0, j)),
                  pl.BlockSpec((1, 1, bn), lambda l, j: (l, 0, j))],
        out_specs=pl.BlockSpec((1, 8, bn), lambda l, j: (l, 0, j)),
        out_shape=jax.ShapeDtypeStruct((depth, 8, n), F32),
        compiler_params=_params(),
        name="modulation",
    )(c_rows, ada_w, ada_b.reshape(depth, 1, n))


def _ffn_kernel(x_ref, mod_ref, g_ref, wg_ref, wu_ref, wd_ref, *rest, mi, f_chunks, x_tiles, ctx_tile):
    o_ref = rest[-1] if x_tiles is None else rest[-2]
    x = x_ref[0]
    if ctx_tile is not None:
        x = jnp.where(pl.program_id(1) == ctx_tile, rest[0][0], x)
    m = mod_ref[0]
    shift, scale, gate = m[3 * mi:3 * mi + 1], m[3 * mi + 1:3 * mi + 2], m[3 * mi + 2:3 * mi + 3]
    u = (_rms(x) * g_ref[...]) * (1.0 + scale) + shift
    ub = u.astype(BF16)
    y = jnp.zeros(x.shape, F32)
    for lo, hi in f_chunks:
        a = _dot(ub, wg_ref[:, lo:hi])
        b = _dot(ub, wu_ref[:, lo:hi])
        hid = (a * jax.nn.sigmoid(a) * b).astype(BF16)
        y = y + _dot(hid, wd_ref[lo:hi, :])
    new = x + (0.5 * gate) * y
    o_ref[0] = new
    if x_tiles is not None:
        fg_ref, final_ref = rest[0], rest[-1]

        @pl.when(pl.program_id(1) < x_tiles)
        def _():
            final_ref[0] = _rms(new) * fg_ref[...]


def _ffn(xs, mod, gains, wg, wu, wd, *, l, mi, mod_row, final=None, ctx=None):
    b, nt, d = xs.shape
    f = wg.shape[2]
    chunk = 1024
    f_chunks = tuple((lo, min(lo + chunk, f)) for lo in range(0, f, chunk))
    tile = pl.BlockSpec((1, TM, d), lambda i, t: (i, t, 0))
    in_specs = [tile, _mod_spec(mod, l, mod_row), _layer_spec(gains, l, mi),
                _layer_spec(wg, l), _layer_spec(wu, l), _layer_spec(wd, l)]
    args = [xs, mod, gains, wg, wu, wd]
    aliases, ctx_tile = {0: 0}, None
    if ctx is not None:
        ctx_tile = nt // TM
        nt += ctx.shape[1]
        in_specs[0] = pl.BlockSpec((1, TM, d), lambda i, t: (i, jnp.minimum(t, ctx_tile - 1), 0))
        in_specs.append(pl.BlockSpec((1, TM, d), lambda i, t: (i, 0, 0)))
        args.append(ctx)
        aliases = {}
    out_specs, out_shape, x_tiles = tile, jax.ShapeDtypeStruct((b, nt, d), F32), None
    if final is not None:
        fg, x_tiles = final
        in_specs.append(_const_spec(fg.shape))
        args.append(fg)
        out_specs = [tile, pl.BlockSpec((1, TM, d), lambda i, t: (i, jnp.minimum(t, x_tiles - 1), 0))]
        out_shape = [out_shape, jax.ShapeDtypeStruct((b, x_tiles * TM, d), F32)]
    return pl.pallas_call(
        functools.partial(_ffn_kernel, mi=mi, f_chunks=f_chunks, x_tiles=x_tiles, ctx_tile=ctx_tile),
        grid=(b, nt // TM),
        in_specs=in_specs,
        out_specs=out_specs,
        out_shape=out_shape,
        input_output_aliases=aliases,
        compiler_params=_params(),
        name=f"ffn{mi}",
    )(*args)


def _inproj_kernel(x_ref, mod_ref, g_ref, wcq, wckv, wkr2, wdqT, wdk, wdvT, wpool, wgates,
                   qng, kvng, wuqT, wkpad, wvT, place, perm,
                   cosT, sinT, cosN, sinN, c128, s128,
                   oq_mla, ok_mla, ov_mla, oq_diff, ok_diff, ov_diff, opool, ogates, okn_mla, okn_diff):
    x = x_ref[0]
    m = mod_ref[0]
    u = (_rms(x) * g_ref[...]) * (1.0 + m[4:5]) + m[3:4]
    ub = u.astype(BF16)
    tm = x.shape[0]
    cT, sT = cosT[...], sinT[...]
    hw = ROPE_HALF

    cqn = (_rms(_dot(ub, wcq[...])) * qng[...]).astype(BF16)
    qT = lax.dot_general(wuqT[...], cqn, NT_DIMS, preferred_element_type=F32) * MLA_SCALE
    nope_w = HEADS * MLA_NOPE
    x1, x2 = qT[nope_w:nope_w + LANES], qT[nope_w + LANES:nope_w + 2 * LANES]
    qn = qT[0:nope_w].astype(BF16)
    r1 = (x1 * cT - x2 * sT).astype(BF16)
    r2 = (x1 * sT + x2 * cT).astype(BF16)
    zpad = jnp.zeros((LANES - MLA_QK, tm), BF16)
    for h in range(HEADS):
        oq_mla[0, h, 0, 0:MLA_NOPE, :] = qn[h * MLA_NOPE:(h + 1) * MLA_NOPE]
        oq_mla[0, h, 0, MLA_NOPE:MLA_NOPE + hw, :] = r1[h * hw:(h + 1) * hw]
        oq_mla[0, h, 0, MLA_NOPE + hw:MLA_QK, :] = r2[h * hw:(h + 1) * hw]
        oq_mla[0, h, 0, MLA_QK:LANES, :] = zpad

    ckvn = (_rms(_dot(ub, wckv[...])) * kvng[...]).astype(BF16)
    kr2 = _dot(ub, wkr2[...])
    krr = (kr2[:, 0:LANES] * c128[...] + kr2[:, LANES:2 * LANES] * s128[...]).astype(BF16)
    kall = (_dot(ckvn, wkpad[...]) + _dot(krr, place[...])).astype(BF16)
    vT = lax.dot_general(wvT[...], ckvn, NT_DIMS, preferred_element_type=F32).astype(BF16)
    head_row = lax.broadcasted_iota(jnp.int32, (HEADS, LANES), 0)
    left_half = lax.broadcasted_iota(jnp.int32, (tm, LANES), 1) < 2 * DIFF_HD

    def max_sq_norm(sq):
        return jnp.max(jnp.sum(sq, axis=1, keepdims=True), axis=0, keepdims=True)

    kn = jnp.zeros((HEADS, LANES), F32)
    for h in range(HEADS):
        kh = kall[:, h * LANES:(h + 1) * LANES]
        ok_mla[0, h] = kh
        khf = kh.astype(F32)
        kn = jnp.where(head_row == h, max_sq_norm(khf * khf), kn)
    okn_mla[0, 0] = kn
    ones_rows = (lax.broadcasted_iota(jnp.int32, (V_ROWS - MLA_V, tm), 0) == 0).astype(BF16)

    def store_values(ov, vals):
        for h in range(HEADS):
            ov[0, h * V_ROWS:h * V_ROWS + MLA_V, :] = vals[h * MLA_V:(h + 1) * MLA_V]
            ov[0, h * V_ROWS + MLA_V:(h + 1) * V_ROWS, :] = ones_rows

    store_values(ov_mla, vT)

    dqT = lax.dot_general(wdqT[...], ub, NT_DIMS, preferred_element_type=F32) * DIFF_SCALE
    a1, a2, b1, b2 = (dqT[i * LANES:(i + 1) * LANES] for i in range(4))
    parts = [(a1 * cT - a2 * sT).astype(BF16), (a1 * sT + a2 * cT).astype(BF16),
             (b1 * cT - b2 * sT).astype(BF16), (b1 * sT + b2 * cT).astype(BF16)]
    zhalf = jnp.zeros((2 * DIFF_HD, tm), BF16)
    for h in range(HEADS):
        base = (h % 2) * 2 * DIFF_HD
        for i, part in enumerate(parts):
            oq_diff[0, h, 0, base + i * hw:base + (i + 1) * hw, :] = part[h * hw:(h + 1) * hw]
        other = 2 * DIFF_HD - base
        oq_diff[0, h, 0, other:other + 2 * DIFF_HD, :] = zhalf

    dk = _dot(ub, wdk[...])
    cN, sN = cosN[...], sinN[...]
    k1a, k1b, k2a, k2b = (dk[:, i * LANES:(i + 1) * LANES] for i in range(4))
    rk = jnp.concatenate([k1a * cN - k1b * sN, k1a * sN + k1b * cN,
                          k2a * cN - k2b * sN, k2a * sN + k2b * cN], axis=1).astype(BF16)
    kd = _dot(rk, perm[...]).astype(BF16)
    kn = jnp.zeros((HEADS, LANES), F32)
    for p in range(HEADS // 2):
        kp = kd[:, p * LANES:(p + 1) * LANES]
        ok_diff[0, p] = kp
        kpf = kp.astype(F32)
        sq = kpf * kpf
        kn = jnp.where(head_row == 2 * p, max_sq_norm(jnp.where(left_half, sq, 0.0)), kn)
        kn = jnp.where(head_row == 2 * p + 1, max_sq_norm(jnp.where(left_half, 0.0, sq)), kn)
    okn_diff[0, 0] = kn
    store_values(ov_diff, lax.dot_general(wdvT[...], ub, NT_DIMS, preferred_element_type=F32).astype(BF16))

    opool[0] = _dot(ub, wpool[...])
    ogates[0] = _dot(ub, wgates[...]).astype(BF16)


def _inproj(xs, mod, gains, w, tables, *, l, mod_row):
    b, nt, d = xs.shape
    tile = pl.BlockSpec((1, TM, d), lambda i, t: (i, t, 0))
    tabT = pl.BlockSpec((LANES, TM), lambda i, t: (0, t))
    tabN = pl.BlockSpec((TM, LANES), lambda i, t: (t, 0))
    stacked = [w["wcq"], w["wckv"], w["wkr2"], w["wdqT"], w["wdk"], w["wdvT"], w["wpool"], w["wgates"],
               w["qng"], w["kvng"], w["wuqT"], w["wkpad"], w["wvT"]]
    shared = [w["place"], w["perm"]]
    vT_spec = pl.BlockSpec((1, HEADS * V_ROWS, TM), lambda i, t: (i, 0, t))

    def q_spec(tq):
        r = tq // TM
        return pl.BlockSpec((1, HEADS, 1, LANES, TM), lambda i, t: (i, 0, t // r, 0, t % r))

    out_shapes = [
        jax.ShapeDtypeStruct((b, HEADS, pl.cdiv(nt, TQ_MLA), LANES, TQ_MLA), BF16),
        jax.ShapeDtypeStruct((b, HEADS, nt, LANES), BF16),
        jax.ShapeDtypeStruct((b, HEADS * V_ROWS, nt), BF16),
        jax.ShapeDtypeStruct((b, HEADS, pl.cdiv(nt, TQ_DIFF), LANES, TQ_DIFF), BF16),
        jax.ShapeDtypeStruct((b, HEADS // 2, nt, LANES), BF16),
        jax.ShapeDtypeStruct((b, HEADS * V_ROWS, nt), BF16),
        jax.ShapeDtypeStruct((b, nt, 4 * POOL_G), F32),
        jax.ShapeDtypeStruct((b, nt, 3 * d), BF16),
        jax.ShapeDtypeStruct((b, nt // TM, HEADS, LANES), F32),
        jax.ShapeDtypeStruct((b, nt // TM, HEADS, LANES), F32),
    ]
    kn_spec = pl.BlockSpec((1, 1, HEADS, LANES), lambda i, t: (i, t, 0, 0))
    out_specs = [
        q_spec(TQ_MLA),
        pl.BlockSpec((1, HEADS, TM, LANES), lambda i, t: (i, 0, t, 0)),
        vT_spec,
        q_spec(TQ_DIFF),
        pl.BlockSpec((1, HEADS // 2, TM, LANES), lambda i, t: (i, 0, t, 0)),
        vT_spec,
        pl.BlockSpec((1, TM, 4 * POOL_G), lambda i, t: (i, t, 0)),
        pl.BlockSpec((1, TM, 3 * d), lambda i, t: (i, t, 0)),
        kn_spec, kn_spec,
    ]
    return pl.pallas_call(
        _inproj_kernel,
        grid=(b, nt // TM),
        in_specs=([tile, _mod_spec(mod, l, mod_row), _layer_spec(gains, l, 1)]
                  + [_layer_spec(a, l) for a in stacked] + [_const_spec(a.shape) for a in shared]
                  + [tabT, tabT, tabN, tabN, tabN, tabN]),
        out_specs=out_specs,
        out_shape=out_shapes,
        compiler_params=_params(),
        name="inproj",
    )(xs, mod, gains, *stacked, *shared, tables["cosT"], tables["sinT"], tables["cosN"], tables["sinN"],
      tables["c128"], tables["s128"])


def _attn_kernel(*refs, chunks, diff, lam_init, aliased):
    refs = list(refs)
    q_ref, k_ref, v_ref, kn_ref = refs[:4]
    n_buf = ATT_AHEAD + 1
    s_bufs = refs[-n_buf:]
    o_ref = refs[-n_buf - 1]
    assert len(refs) == 5 + 2 * diff + aliased + n_buf
    nsub, tq = q_ref.shape[2], q_ref.shape[4]
    nc = len(chunks)
    head = pl.program_id(1)

    tile_max = jnp.max(kn_ref[0], axis=0)
    rows = lax.broadcasted_iota(jnp.int32, tile_max.shape, 0)
    kmax2 = jnp.max(jnp.where(rows == head, tile_max, 0.0), axis=0, keepdims=True)[:, 0:1]

    def finish(num, den, sub):
        o = num / den
        if diff:
            dl_ref, sg_ref = refs[4:6]
            dl = dl_ref[...]
            lam = (jnp.exp(jnp.sum(dl[0:1] * dl[1:2], axis=1, keepdims=True))
                   - jnp.exp(jnp.sum(dl[2:3] * dl[3:4], axis=1, keepdims=True)) + lam_init)
            o = o[:, :tq] - lam * o[:, tq:]
            o = o * lax.rsqrt(jnp.mean(o * o, axis=0, keepdims=True) + EPS)
            o = o * sg_ref[...] * (1.0 - lam_init)
        o_ref[0, sub] = o.astype(o_ref.dtype)

    def bounded_shift(rhs, bound, sub):
        acc = jnp.zeros((MLA_V, rhs.shape[1]), F32)
        den = jnp.zeros((1, rhs.shape[1]), F32)
        pieces = lambda c: [(lo, min(lo + ATT_ROWS, chunks[c][1])) for lo in range(*chunks[c], ATT_ROWS)]
        logits = lambda lo, hi: _dot(k_ref[0, 0, lo:hi, :], rhs)
        s_next = [logits(lo, hi) for lo, hi in pieces(0)]
        for c in range(nc):
            s_cur, s_next = s_next, []
            nxt = pieces(c + 1) if c + 1 < nc else []
            for i, (lo, hi) in enumerate(pieces(c)):
                if i < len(nxt):
                    s_next.append(logits(*nxt[i]))
                p = jnp.exp2(s_cur[i] - bound)
                den = den + jnp.sum(p, axis=0, keepdims=True)
                acc = acc + _dot(v_ref[0, 0:MLA_V, lo:hi], p.astype(BF16))
            s_next += [logits(lo, hi) for lo, hi in nxt[len(pieces(c)):]]
        finish(acc, den, sub)

    def running_max(rhs, sub, zero_row):
        n = rhs.shape[1]

        def scores(c):
            lo, hi = chunks[c]
            s = _dot(k_ref[0, 0, lo:hi, :], rhs)
            s_bufs[c % n_buf][pl.ds(zero_row, hi - lo), :] = s
            return jnp.max(s, axis=0, keepdims=True)

        m = jnp.full((1, n), -1e30, F32)
        acc = jnp.zeros((V_ROWS, n), F32)
        cmaxes = {c: scores(c) for c in range(min(ATT_AHEAD, nc))}
        for c in range(nc):
            lo, hi = chunks[c]
            m_new = jnp.maximum(m, cmaxes.pop(c))
            alpha = jnp.exp2(m - m_new)
            if c + ATT_AHEAD < nc:
                cmaxes[c + ATT_AHEAD] = scores(c + ATT_AHEAD)
            p = jnp.exp2(s_bufs[c % n_buf][pl.ds(zero_row, hi - lo), :] - m_new)
            m, acc = m_new, alpha * acc + _dot(v_ref[0, :, lo:hi], p.astype(BF16))
        finish(acc[0:MLA_V], acc[MLA_V:MLA_V + 1], sub)

    def queries(sub):
        q = q_ref[0, 0, sub]
        if diff:
            first = (lax.broadcasted_iota(jnp.int32, q.shape, 0) & DIFF_HD) == 0
            zero = jnp.zeros_like(q)
            rhs = jnp.concatenate([jnp.where(first, q, zero), jnp.where(first, zero, q)], axis=1)
        else:
            rhs = q
        qf = rhs.astype(F32)
        return rhs, jnp.sqrt(jnp.sum(qf * qf, axis=0, keepdims=True) * kmax2)

    def for_each_block(fn, unroll=1):
        if nsub == 1:
            fn(0)
        else:
            lax.fori_loop(0, nsub, lambda sub, carry: (fn(sub), carry)[1], 0, unroll=unroll)

    qa = q_ref[0, 0].astype(F32)
    q_norm2 = jnp.max(jnp.sum(qa * qa, axis=1, keepdims=True))
    small = q_norm2 * jnp.max(kmax2) <= ATT_BOUND_MAX * ATT_BOUND_MAX

    @pl.when(small)
    def _():
        for_each_block(lambda sub: bounded_shift(*queries(sub), sub), unroll=ATT_UNROLL)

    @pl.when(jnp.logical_not(small))
    def _():
        zero_row = pl.multiple_of(jnp.minimum(pl.program_id(2), 0), 16)
        for_each_block(lambda sub: running_max(queries(sub)[0], sub, zero_row))


def _key_chunks(nk):
    head = [c for c in CHUNK_HEAD if sum(CHUNK_HEAD) + sum(CHUNK_TAIL) + TK <= nk]
    tail = CHUNK_TAIL if head else ()
    sizes = list(head)
    body = nk - sum(head) - sum(tail)
    sizes += [TK] * (body // TK) + ([body % TK] if body % TK else [])
    sizes += list(tail)
    edges = np.cumsum([0] + sizes)
    assert edges[-1] == nk and all(s % LANES == 0 for s in sizes)
    return tuple((int(a), int(b)) for a, b in zip(edges[:-1], edges[1:]))


def _attention(qT, k, vT, kn, *, tq, nsub, steps, q_block0, k_rows, k_block0, chunks,
               diff, extra=(), layer=0, lam_init=0.0, prev_out=None):
    b, heads, nqb, _, tqw = qT.shape
    dv = MLA_V
    kdiv = 2 if diff else 1
    in_specs = [
        pl.BlockSpec((1, 1, nsub, LANES, tq), lambda i, h, j: (i, h, q_block0 + j, 0, 0)),
        pl.BlockSpec((1, 1, k_rows, LANES), lambda i, h, j: (i, h // kdiv, k_block0, 0)),
        pl.BlockSpec((1, V_ROWS, k_rows), lambda i, h, j: (i, h, k_block0)),
        pl.BlockSpec((1,) + kn.shape[1:], lambda i, h, j: (i, 0, 0, 0)),
    ]
    args = [qT, k, vT, kn]
    if diff:
        in_specs += [_layer_spec(a, layer) for a in extra]
        args += list(extra)
    aliases = {}
    if prev_out is not None:
        in_specs.append(pl.BlockSpec(memory_space=pl.ANY))
        args.append(prev_out)
        aliases = {len(args) - 1: 0}
    return pl.pallas_call(
        functools.partial(_attn_kernel, chunks=chunks, diff=diff, lam_init=lam_init,
                          aliased=prev_out is not None),
        grid=(b, heads, steps),
        in_specs=in_specs,
        out_specs=pl.BlockSpec((1, nsub, dv, tq), lambda i, h, j: (i, q_block0 + j, h, 0)),
        out_shape=jax.ShapeDtypeStruct((b, nqb, heads * dv, tqw), BF16),
        scratch_shapes=[pltpu.VMEM((max(hi - lo for lo, hi in chunks), (2 * tq if diff else tq)), dt)
                        for dt in [F32] * (ATT_AHEAD + 1)],
        input_output_aliases=aliases,
        compiler_params=pltpu.CompilerParams(vmem_limit_bytes=VMEM_LIMIT, flags=ATT_FLAGS),
        name=("diff" if diff else "mla") + ("_ctx" if prev_out is not None else "_x"),
    )(*args)


def _mix_kernel(x_ref, mod_ref, oa_ref, od_ref, pc_ref, pp_ref, pn_ref, gt_ref, bg_ref,
                pproj_ref, pb_ref, ps_ref, wa_ref, wd_ref, wp_ref, wo_ref, o_ref, *, x_tiles):
    t = pl.program_id(1)
    x = x_ref[0]
    gate = mod_ref[0][5:6]
    tm, d = x.shape
    a = lax.dot_general(oa_ref[0, 0], wa_ref[...], TN_DIMS, preferred_element_type=F32)
    dd = lax.dot_general(od_ref[0, 0], wd_ref[...], TN_DIMS, preferred_element_type=F32)

    is_ctx = t == x_tiles
    has_prev = jnp.logical_and(t != 0, jnp.logical_not(is_ctx))
    has_next = jnp.logical_and(t != x_tiles - 1, jnp.logical_not(is_ctx))
    cur = pc_ref[0]
    prev = jnp.where(has_prev, pp_ref[0], 0.0)
    nxt = jnp.where(has_next, pn_ref[0], 0.0)
    ext = jnp.concatenate([prev, cur, nxt], axis=0)
    rows = ext.shape[0]
    seq_len = jnp.where(is_ctx, tm, x_tiles * tm)
    pos = jnp.where(is_ctx, 0, t * tm) + lax.broadcasted_iota(jnp.int32, (tm, 1), 0)
    outs = []
    for g, w in enumerate(POOL_WINDOWS):
        lanes = slice(g * POOL_G, (g + 1) * POOL_G)
        run = ext[:, lanes]
        span = 1
        while span < w:
            run = run + pltpu.roll(run, rows - span, axis=0)
            span *= 2
        win = pltpu.roll(run, rows - (POOL_HALO - w // 2), axis=0)[0:tm]
        cnt = (jnp.minimum(pos + w // 2, seq_len) - jnp.maximum(pos - w // 2, 0)).astype(F32)
        pooled = (win / cnt - cur[:, lanes]).astype(BF16)
        outs.append((_dot(pooled, pproj_ref[g]) + pb_ref[:, lanes]) * ps_ref[:, lanes])
    pooled_out = _dot(jnp.concatenate(outs, axis=1).astype(BF16), wp_ref[...])

    gs = jax.nn.sigmoid(gt_ref[0].astype(F32) + bg_ref[...])
    merged = gs[:, 0:d] * a + gs[:, d:2 * d] * dd + gs[:, 2 * d:3 * d] * pooled_out
    o_ref[0] = x + gate * _dot(merged.astype(BF16), wo_ref[...])


def _mix(xs, mod, oa, od, pool_in, gates, w, *, l, mod_row, x_tiles):
    b, nt, d = xs.shape
    hb = TM // POOL_HALO
    n_halo = nt // POOL_HALO
    tile = pl.BlockSpec((1, TM, d), lambda i, t: (i, t, 0))
    pw = 4 * POOL_G
    weights = [w["bgate"], w["pproj"], w["pb"], w["ps"], w["wa"], w["wd"], w["wp"], w["wo"]]

    def o_spec(o):
        r = o.shape[3] // TM
        return pl.BlockSpec((1, 1, o.shape[2], TM), lambda i, t: (i, t // r, 0, t % r))

    return pl.pallas_call(
        functools.partial(_mix_kernel, x_tiles=x_tiles),
        grid=(b, nt // TM),
        in_specs=[tile, _mod_spec(mod, l, mod_row),
                  o_spec(oa), o_spec(od),
                  pl.BlockSpec((1, TM, pw), lambda i, t: (i, t, 0)),
                  pl.BlockSpec((1, POOL_HALO, pw), lambda i, t: (i, jnp.maximum(t * hb - 1, 0), 0)),
                  pl.BlockSpec((1, POOL_HALO, pw), lambda i, t: (i, jnp.minimum((t + 1) * hb, n_halo - 1), 0)),
                  pl.BlockSpec((1, TM, 3 * d), lambda i, t: (i, t, 0))]
                 + [_layer_spec(a, l) for a in weights],
        out_specs=tile,
        out_shape=jax.ShapeDtypeStruct(xs.shape, F32),
        input_output_aliases={0: 0},
        compiler_params=_params(),
        name="mix",
    )(xs, mod, oa, od, pool_in, pool_in, pool_in, gates, *weights)


def _rope_tables(seq, ctx):
    rows = seq // GRID_W
    row_ids = jnp.repeat(jnp.arange(rows), GRID_W).astype(F32)
    col_ids = jnp.tile(jnp.arange(GRID_W), rows).astype(F32)
    n_freq = ROPE_HALF // 2
    inv_freq = ROPE_THETA ** (-jnp.arange(n_freq, dtype=F32) / n_freq)
    ang = jnp.concatenate([row_ids[:, None] * inv_freq, col_ids[:, None] * inv_freq], axis=-1)
    cos = jnp.concatenate([jnp.cos(ang), jnp.ones((ctx, ROPE_HALF), F32)], axis=0)
    sin = jnp.concatenate([jnp.sin(ang), jnp.zeros((ctx, ROPE_HALF), F32)], axis=0)
    pad = jnp.zeros((seq + ctx, LANES - 2 * ROPE_HALF), F32)
    return {
        "cosT": jnp.tile(cos.T, (HEADS, 1)), "sinT": jnp.tile(sin.T, (HEADS, 1)),
        "cosN": jnp.tile(cos, (1, HEADS)), "sinN": jnp.tile(sin, (1, HEADS)),
        "c128": jnp.concatenate([cos, cos, pad], axis=1),
        "s128": jnp.concatenate([sin, sin, pad], axis=1),
    }


def _placement_matrices():
    hw = ROPE_HALF
    perm = np.zeros((HEADS * 2 * DIFF_HD, HEADS * 2 * DIFF_HD), np.float32)
    for h in range(HEADS):
        for c in range(2):
            for half in range(2):
                for f in range(hw):
                    perm[(c * 2 + half) * LANES + h * hw + f, h * 2 * DIFF_HD + c * DIFF_HD + half * hw + f] = 1.0
    place = np.zeros((LANES, HEADS * LANES), np.float32)
    for h in range(HEADS):
        for f in range(MLA_ROPE):
            place[f, h * LANES + MLA_NOPE + f] = 1.0
    return jnp.asarray(place, BF16), jnp.asarray(perm, BF16)


def _stacked_weights(p):
    w_in = p["w_in"]
    nl, d, _ = w_in.shape
    hw = ROPE_HALF
    o_cq, o_ckv, o_kr = 0, 384, 640
    o_dq, o_dk, o_dv, o_pool, o_gate = 672, 1184, 1696, 2208, 2720
    bf = lambda a: a.astype(BF16)
    t = lambda a: jnp.swapaxes(a, 1, 2)

    def block_order(cols):
        a = cols.reshape(nl, d, HEADS, 2, 2, hw)
        return a.transpose(0, 1, 3, 4, 2, 5).reshape(nl, d, HEADS * 2 * DIFF_HD)

    wkr = w_in[:, :, o_kr:o_kr + MLA_ROPE]
    wkr_rot = jnp.concatenate([-wkr[:, :, hw:], wkr[:, :, :hw]], axis=2)
    zpad = jnp.zeros((nl, d, LANES - MLA_ROPE), F32)
    wkr2 = jnp.concatenate([wkr, zpad, wkr_rot, zpad], axis=2)

    w_uq = p["mla_w_uq"].reshape(nl, -1, HEADS, MLA_QK)
    lora = w_uq.shape[1]
    wuq = jnp.concatenate([w_uq[..., :MLA_NOPE].reshape(nl, lora, -1),
                           w_uq[..., MLA_NOPE:MLA_NOPE + hw].reshape(nl, lora, -1),
                           w_uq[..., MLA_NOPE + hw:].reshape(nl, lora, -1)], axis=2)
    w_ukv = p["mla_w_ukv"].reshape(nl, -1, HEADS, MLA_NOPE + MLA_V)
    kv_lora = w_ukv.shape[1]
    is_key = (jnp.arange(MLA_NOPE + MLA_V) < MLA_NOPE).astype(F32)
    wkpad = (w_ukv * is_key).reshape(nl, kv_lora, -1)
    wv = w_ukv[..., MLA_NOPE:].reshape(nl, kv_lora, -1)
    place, perm = _placement_matrices()
    return {
        "wcq": bf(w_in[:, :, o_cq:o_ckv]), "wckv": bf(w_in[:, :, o_ckv:o_kr]), "wkr2": bf(wkr2),
        "wdqT": bf(t(block_order(w_in[:, :, o_dq:o_dk]))), "wdk": bf(block_order(w_in[:, :, o_dk:o_dv])),
        "wdvT": bf(t(w_in[:, :, o_dv:o_pool])), "wpool": bf(w_in[:, :, o_pool:o_gate]), "wgates": bf(w_in[:, :, o_gate:]),
        "qng": p["mla_q_norm_g"][:, None], "kvng": p["mla_kv_norm_g"][:, None],
        "wuqT": bf(t(wuq)), "wkpad": bf(wkpad), "wvT": bf(t(wv)),
        "place": place, "perm": perm,
        "bgate": p["b_gate"].reshape(nl, 1, -1), "pproj": bf(p["pool_proj"]),
        "pb": p["pool_b"].reshape(nl, 1, -1), "ps": p["pool_scale"][:, None],
        "wa": bf(p["w_br_mla"]), "wd": bf(p["w_br_diff"]), "wp": bf(p["w_br_pool"]), "wo": bf(p["w_out"]),
    }


def kernel(x, c, ctx, c_ctx, ada_w, ada_b, norm_g, ffa_w_gate, ffa_w_up, ffa_w_down, ffb_w_gate, ffb_w_up, ffb_w_down, w_in, b_gate, mla_q_norm_g, mla_kv_norm_g, mla_w_uq, mla_w_ukv, diff_lambda, diff_subln_g, pool_proj, pool_b, pool_scale, w_br_mla, w_br_diff, w_br_pool, w_out, final_g):
    b, seq, d = x.shape
    n_ctx = ctx.shape[1]
    depth = ada_w.shape[0]
    nt = seq + n_ctx
    nsub = ATT_NSUB if seq % (ATT_NSUB * TQ_MLA) == 0 else 1
    assert n_ctx == TM and seq % (nsub * TQ_MLA) == 0 and seq % GRID_W == 0 and b + 1 <= 8
    x_tiles = seq // TM
    p = dict(w_in=w_in, b_gate=b_gate, mla_q_norm_g=mla_q_norm_g, mla_kv_norm_g=mla_kv_norm_g,
             mla_w_uq=mla_w_uq, mla_w_ukv=mla_w_ukv, pool_proj=pool_proj, pool_b=pool_b,
             pool_scale=pool_scale, w_br_mla=w_br_mla, w_br_diff=w_br_diff, w_br_pool=w_br_pool, w_out=w_out)

    c_rows = jnp.concatenate([c, c_ctx[None], jnp.zeros((8 - b - 1, d), F32)], axis=0)
    mod = _modulation(c_rows, ada_w, ada_b).reshape(depth, 8, N_ADA, d)
    mod_row = lambda i, t: jnp.where(t == x_tiles, b, i)

    tables = _rope_tables(seq, n_ctx)
    w = _stacked_weights(p)
    gains = norm_g[:, :, None, :]
    bf = lambda a: a.astype(BF16)
    ffa = (bf(ffa_w_gate), bf(ffa_w_up), bf(ffa_w_down))
    ffb = (bf(ffb_w_gate), bf(ffb_w_up), bf(ffb_w_down))
    extra = (diff_lambda, diff_subln_g[:, :, None])
    chunks = _key_chunks(nt)

    xs = x
    for l in range(depth):
        lam_init = 0.8 - 0.6 * math.exp(-0.3 * l)
        xs = _ffn(xs, mod, gains, *ffa, l=l, mi=0, mod_row=mod_row, ctx=ctx if l == 0 else None)
        q_mla, k_mla, v_mla, q_diff, k_diff, v_diff, pool_in, gates, kn_mla, kn_diff = _inproj(
            xs, mod, gains, w, tables, l=l, mod_row=mod_row)
        mla = (q_mla, k_mla, v_mla, kn_mla)
        dif = (q_diff, k_diff, v_diff, kn_diff)
        full = dict(nsub=nsub, q_block0=0, k_rows=nt, k_block0=0, chunks=chunks)
        ctx_only = dict(tq=TM, nsub=1, steps=1, k_rows=TM, k_block0=x_tiles, chunks=((0, TM),))
        oa = _attention(*mla, tq=TQ_MLA, steps=seq // (nsub * TQ_MLA), diff=False, **full)
        oa = _attention(*mla, q_block0=seq // TQ_MLA, diff=False, prev_out=oa, **ctx_only)
        od = _attention(*dif, tq=TQ_DIFF, steps=seq // (nsub * TQ_DIFF), diff=True,
                        extra=extra, layer=l, lam_init=lam_init, **full)
        od = _attention(*dif, q_block0=seq // TQ_DIFF, diff=True, extra=extra, layer=l,
                        lam_init=lam_init, prev_out=od, **ctx_only)
        xs = _mix(xs, mod, oa, od, pool_in, gates, w, l=l, mod_row=mod_row, x_tiles=x_tiles)
        if l + 1 < depth:
            xs = _ffn(xs, mod, gains, *ffb, l=l, mi=2, mod_row=mod_row)
    return _ffn(xs, mod, gains, *ffb, l=depth - 1, mi=2, mod_row=mod_row, final=(final_g[None], x_tiles))[1]
```

```python
import functools
import math

import numpy as np
import jax
import jax.numpy as jnp
from jax import lax
from jax.experimental import pallas as pl
from jax.experimental.pallas import tpu as pltpu

F32 = jnp.float32
BF16 = jnp.bfloat16

EPS = 1e-6
ROPE_THETA = 10000.0
GRID_W = 64
N_ADA = 9

HEADS = 8
MLA_NOPE = 64
MLA_ROPE = 32
MLA_V = 64
V_ROWS = MLA_V + 16
MLA_QK = MLA_NOPE + MLA_ROPE
LOG2E = math.log2(math.e)
MLA_SCALE = MLA_QK ** -0.5 * LOG2E
DIFF_HD = 32
DIFF_V = 64
DIFF_SCALE = DIFF_HD ** -0.5 * LOG2E
ROPE_HALF = 16
POOL_WINDOWS = (2, 4, 8, 16)
POOL_G = 128
POOL_HALO = 16

LANES = 128
MOD_ROWS = 8
MOD_COL_BLOCKS = 8
TM = 256
TK = 512
CHUNK_HEAD = (256,)
CHUNK_TAIL = ()
ATT_AHEAD = 2
ATT_ROWS = 256
ATT_BOUND_MAX = 40.0
ATT_NSUB = 4
TQ_MLA = 1024
TQ_DIFF = 512
VMEM_LIMIT = 52 * 1024 * 1024

ATT_FLAGS = {}

NT_DIMS = (((1,), (1,)), ((), ()))
TN_DIMS = (((0,), (0,)), ((), ()))


def _params():
    return pltpu.CompilerParams(vmem_limit_bytes=VMEM_LIMIT)


def _const_spec(shape):
    zeros = (0,) * len(shape)
    return pl.BlockSpec(shape, lambda *_: zeros, pipeline_mode=pl.Buffered(1))


def _layer_spec(a, *lead):
    tail = a.shape[len(lead):]
    zeros = (0,) * len(tail)
    return pl.BlockSpec((None,) * len(lead) + tail, lambda *_: tuple(lead) + zeros, pipeline_mode=pl.Buffered(1))


def _mod_spec(mod, l, mod_row):
    return pl.BlockSpec((None, 1) + mod.shape[2:], lambda i, t: (l, mod_row(i, t), 0, 0))


def _dot(a, b):
    return jnp.dot(a, b, preferred_element_type=F32)


def _rms(x):
    return x * lax.rsqrt(jnp.mean(x * x, axis=-1, keepdims=True) + EPS)


def _mod_kernel(c_ref, w_ref, b_ref, o_ref):
    c = c_ref[...]
    a = c * jax.nn.sigmoid(c)
    o_ref[0] = jnp.dot(a, w_ref[0], preferred_element_type=F32,
                       precision=lax.Precision.HIGHEST) + b_ref[0]


def _modulation(c_rows, ada_w, ada_b):
    depth, d, n = ada_w.shape
    bn = n // MOD_COL_BLOCKS
    return pl.pallas_call(
        _mod_kernel,
        grid=(depth, MOD_COL_BLOCKS),
        in_specs=[pl.BlockSpec((MOD_ROWS, d), lambda l, j: (0, 0)),
                  pl.BlockSpec((1, d, bn), lambda l, j: (l, 0, j)),
                  pl.BlockSpec((1, 1, bn), lambda l, j: (l, 0, j))],
        out_specs=pl.BlockSpec((1, MOD_ROWS, bn), lambda l, j: (l, 0, j)),
        out_shape=jax.ShapeDtypeStruct((depth, MOD_ROWS, n), F32),
        compiler_params=_params(),
        name="modulation",
    )(c_rows, ada_w, ada_b.reshape(depth, 1, n))


def _ffn_kernel(x_ref, mod_ref, g_ref, wg_ref, wu_ref, wd_ref, *rest, mi, f_chunks, x_tiles, ctx_tile):
    o_ref = rest[-1] if x_tiles is None else rest[-2]
    x = x_ref[0]
    if ctx_tile is not None:
        x = jnp.where(pl.program_id(1) == ctx_tile, rest[0][0], x)
    m = mod_ref[0]
    shift, scale, gate = m[3 * mi:3 * mi + 1], m[3 * mi + 1:3 * mi + 2], m[3 * mi + 2:3 * mi + 3]
    u = (_rms(x) * g_ref[...]) * (1.0 + scale) + shift
    ub = u.astype(BF16)
    y = jnp.zeros(x.shape, F32)
    for lo, hi in f_chunks:
        a = _dot(ub, wg_ref[:, lo:hi])
        b = _dot(ub, wu_ref[:, lo:hi])
        hid = (a * jax.nn.sigmoid(a) * b).astype(BF16)
        y = y + _dot(hid, wd_ref[lo:hi, :])
    new = x + (0.5 * gate) * y
    o_ref[0] = new
    if x_tiles is not None:
        fg_ref, final_ref = rest[0], rest[-1]

        @pl.when(pl.program_id(1) < x_tiles)
        def _():
            final_ref[0] = _rms(new) * fg_ref[...]


def _ffn(xs, mod, gains, wg, wu, wd, *, l, mi, mod_row, final=None, ctx=None):
    b, nt, d = xs.shape
    f = wg.shape[2]
    chunk = 1024
    f_chunks = tuple((lo, min(lo + chunk, f)) for lo in range(0, f, chunk))
    tile = pl.BlockSpec((1, TM, d), lambda i, t: (i, t, 0))
    in_specs = [tile, _mod_spec(mod, l, mod_row), _layer_spec(gains, l, mi),
                _layer_spec(wg, l), _layer_spec(wu, l), _layer_spec(wd, l)]
    args = [xs, mod, gains, wg, wu, wd]
    aliases, ctx_tile = {0: 0}, None
    if ctx is not None:
        ctx_tile = nt // TM
        nt += ctx.shape[1]
        in_specs[0] = pl.BlockSpec((1, TM, d), lambda i, t: (i, jnp.minimum(t, ctx_tile - 1), 0))
        in_specs.append(pl.BlockSpec((1, TM, d), lambda i, t: (i, 0, 0)))
        args.append(ctx)
        aliases = {}
    out_specs, out_shape, x_tiles = tile, jax.ShapeDtypeStruct((b, nt, d), F32), None
    if final is not None:
        fg, x_tiles = final
        in_specs.append(_const_spec(fg.shape))
        args.append(fg)
        out_specs = [tile, pl.BlockSpec((1, TM, d), lambda i, t: (i, jnp.minimum(t, x_tiles - 1), 0))]
        out_shape = [out_shape, jax.ShapeDtypeStruct((b, x_tiles * TM, d), F32)]
    return pl.pallas_call(
        functools.partial(_ffn_kernel, mi=mi, f_chunks=f_chunks, x_tiles=x_tiles, ctx_tile=ctx_tile),
        grid=(b, nt // TM),
        in_specs=in_specs,
        out_specs=out_specs,
        out_shape=out_shape,
        input_output_aliases=aliases,
        compiler_params=_params(),
        name=f"ffn{mi}",
    )(*args)


def _inproj_kernel(x_ref, mod_ref, g_ref, wcq, wckv, wkr2, wdqT, wdk, wdvT, wpool, wgates,
                   qng, kvng, wuqT, wkpad, wvT, place, perm,
                   cosT, sinT, cosN, sinN, c128, s128,
                   oq_mla, ok_mla, ov_mla, oq_diff, ok_diff, ov_diff, opool, ogates, okn_mla, okn_diff):
    x = x_ref[0]
    m = mod_ref[0]
    u = (_rms(x) * g_ref[...]) * (1.0 + m[4:5]) + m[3:4]
    ub = u.astype(BF16)
    tm = x.shape[0]
    cT, sT = cosT[...], sinT[...]
    hw = ROPE_HALF

    cqn = (_rms(_dot(ub, wcq[...])) * qng[...]).astype(BF16)
    qT = lax.dot_general(wuqT[...], cqn, NT_DIMS, preferred_element_type=F32) * MLA_SCALE
    nope_w = HEADS * MLA_NOPE
    x1, x2 = qT[nope_w:nope_w + LANES], qT[nope_w + LANES:nope_w + 2 * LANES]
    qn = qT[0:nope_w].astype(BF16)
    r1 = (x1 * cT - x2 * sT).astype(BF16)
    r2 = (x1 * sT + x2 * cT).astype(BF16)
    zpad = jnp.zeros((LANES - MLA_QK, tm), BF16)
    for h in range(HEADS):
        oq_mla[0, h, 0, 0:MLA_NOPE, :] = qn[h * MLA_NOPE:(h + 1) * MLA_NOPE]
        oq_mla[0, h, 0, MLA_NOPE:MLA_NOPE + hw, :] = r1[h * hw:(h + 1) * hw]
        oq_mla[0, h, 0, MLA_NOPE + hw:MLA_QK, :] = r2[h * hw:(h + 1) * hw]
        oq_mla[0, h, 0, MLA_QK:LANES, :] = zpad

    ckvn = (_rms(_dot(ub, wckv[...])) * kvng[...]).astype(BF16)
    kr2 = _dot(ub, wkr2[...])
    krr = (kr2[:, 0:LANES] * c128[...] + kr2[:, LANES:2 * LANES] * s128[...]).astype(BF16)
    kall = (_dot(ckvn, wkpad[...]) + _dot(krr, place[...])).astype(BF16)
    vT = lax.dot_general(wvT[...], ckvn, NT_DIMS, preferred_element_type=F32).astype(BF16)
    head_row = lax.broadcasted_iota(jnp.int32, (HEADS, LANES), 0)
    left_half = lax.broadcasted_iota(jnp.int32, (tm, LANES), 1) < 2 * DIFF_HD

    def max_sq_norm(sq):
        return jnp.max(jnp.sum(sq, axis=1, keepdims=True), axis=0, keepdims=True)

    kn = jnp.zeros((HEADS, LANES), F32)
    for h in range(HEADS):
        kh = kall[:, h * LANES:(h + 1) * LANES]
        ok_mla[0, h] = kh
        khf = kh.astype(F32)
        kn = jnp.where(head_row == h, max_sq_norm(khf * khf), kn)
    okn_mla[0, 0] = kn
    ones_rows = (lax.broadcasted_iota(jnp.int32, (V_ROWS - MLA_V, tm), 0) == 0).astype(BF16)

    def store_values(ov, vals):
        for h in range(HEADS):
            ov[0, h * V_ROWS:h * V_ROWS + MLA_V, :] = vals[h * MLA_V:(h + 1) * MLA_V]
            ov[0, h * V_ROWS + MLA_V:(h + 1) * V_ROWS, :] = ones_rows

    store_values(ov_mla, vT)

    dqT = lax.dot_general(wdqT[...], ub, NT_DIMS, preferred_element_type=F32) * DIFF_SCALE
    a1, a2, b1, b2 = (dqT[i * LANES:(i + 1) * LANES] for i in range(4))
    parts = [(a1 * cT - a2 * sT).astype(BF16), (a1 * sT + a2 * cT).astype(BF16),
             (b1 * cT - b2 * sT).astype(BF16), (b1 * sT + b2 * cT).astype(BF16)]
    zhalf = jnp.zeros((2 * DIFF_HD, tm), BF16)
    for h in range(HEADS):
        base = (h % 2) * 2 * DIFF_HD
        for i, part in enumerate(parts):
            oq_diff[0, h, 0, base + i * hw:base + (i + 1) * hw, :] = part[h * hw:(h + 1) * hw]
        other = 2 * DIFF_HD - base
        oq_diff[0, h, 0, other:other + 2 * DIFF_HD, :] = zhalf

    dk = _dot(ub, wdk[...])
    cN, sN = cosN[...], sinN[...]
    k1a, k1b, k2a, k2b = (dk[:, i * LANES:(i + 1) * LANES] for i in range(4))
    rk = jnp.concatenate([k1a * cN - k1b * sN, k1a * sN + k1b * cN,
                          k2a * cN - k2b * sN, k2a * sN + k2b * cN], axis=1).astype(BF16)
    kd = _dot(rk, perm[...]).astype(BF16)
    kn = jnp.zeros((HEADS, LANES), F32)
    for p in range(HEADS // 2):
        kp = kd[:, p * LANES:(p + 1) * LANES]
        ok_diff[0, p] = kp
        kpf = kp.astype(F32)
        sq = kpf * kpf
        kn = jnp.where(head_row == 2 * p, max_sq_norm(jnp.where(left_half, sq, 0.0)), kn)
        kn = jnp.where(head_row == 2 * p + 1, max_sq_norm(jnp.where(left_half, 0.0, sq)), kn)
    okn_diff[0, 0] = kn
    store_values(ov_diff, lax.dot_general(wdvT[...], ub, NT_DIMS, preferred_element_type=F32).astype(BF16))

    opool[0] = _dot(ub, wpool[...])
    ogates[0] = _dot(ub, wgates[...]).astype(BF16)


def _inproj(xs, mod, gains, w, tables, *, l, mod_row):
    b, nt, d = xs.shape
    tile = pl.BlockSpec((1, TM, d), lambda i, t: (i, t, 0))
    tabT = pl.BlockSpec((LANES, TM), lambda i, t: (0, t))
    tabN = pl.BlockSpec((TM, LANES), lambda i, t: (t, 0))
    stacked = [w["wcq"], w["wckv"], w["wkr2"], w["wdqT"], w["wdk"], w["wdvT"], w["wpool"], w["wgates"],
               w["qng"], w["kvng"], w["wuqT"], w["wkpad"], w["wvT"]]
    shared = [w["place"], w["perm"]]
    vT_spec = pl.BlockSpec((1, HEADS * V_ROWS, TM), lambda i, t: (i, 0, t))

    def q_spec(tq):
        r = tq // TM
        return pl.BlockSpec((1, HEADS, 1, LANES, TM), lambda i, t: (i, 0, t // r, 0, t % r))

    out_shapes = [
        jax.ShapeDtypeStruct((b, HEADS, pl.cdiv(nt, TQ_MLA), LANES, TQ_MLA), BF16),
        jax.ShapeDtypeStruct((b, HEADS, nt, LANES), BF16),
        jax.ShapeDtypeStruct((b, HEADS * V_ROWS, nt), BF16),
        jax.ShapeDtypeStruct((b, HEADS, pl.cdiv(nt, TQ_DIFF), LANES, TQ_DIFF), BF16),
        jax.ShapeDtypeStruct((b, HEADS // 2, nt, LANES), BF16),
        jax.ShapeDtypeStruct((b, HEADS * V_ROWS, nt), BF16),
        jax.ShapeDtypeStruct((b, nt, 4 * POOL_G), F32),
        jax.ShapeDtypeStruct((b, nt, 3 * d), BF16),
        jax.ShapeDtypeStruct((b, nt // TM, HEADS, LANES), F32),
        jax.ShapeDtypeStruct((b, nt // TM, HEADS, LANES), F32),
    ]
    kn_spec = pl.BlockSpec((1, 1, HEADS, LANES), lambda i, t: (i, t, 0, 0))
    out_specs = [
        q_spec(TQ_MLA),
        pl.BlockSpec((1, HEADS, TM, LANES), lambda i, t: (i, 0, t, 0)),
        vT_spec,
        q_spec(TQ_DIFF),
        pl.BlockSpec((1, HEADS // 2, TM, LANES), lambda i, t: (i, 0, t, 0)),
        vT_spec,
        pl.BlockSpec((1, TM, 4 * POOL_G), lambda i, t: (i, t, 0)),
        pl.BlockSpec((1, TM, 3 * d), lambda i, t: (i, t, 0)),
        kn_spec, kn_spec,
    ]
    return pl.pallas_call(
        _inproj_kernel,
        grid=(b, nt // TM),
        in_specs=([tile, _mod_spec(mod, l, mod_row), _layer_spec(gains, l, 1)]
                  + [_layer_spec(a, l) for a in stacked] + [_const_spec(a.shape) for a in shared]
                  + [tabT, tabT, tabN, tabN, tabN, tabN]),
        out_specs=out_specs,
        out_shape=out_shapes,
        compiler_params=_params(),
        name="inproj",
    )(xs, mod, gains, *stacked, *shared, tables["cosT"], tables["sinT"], tables["cosN"], tables["sinN"],
      tables["c128"], tables["s128"])


def _attn_kernel(*refs, chunks, diff, lam_init, aliased):
    refs = list(refs)
    q_ref, k_ref, v_ref, kn_ref = refs[:4]
    n_buf = ATT_AHEAD + 1
    s_bufs = refs[-n_buf:]
    o_ref = refs[-n_buf - 1]
    assert len(refs) == 5 + 2 * diff + aliased + n_buf
    nsub, tq = q_ref.shape[2], q_ref.shape[4]
    nc = len(chunks)
    head = pl.program_id(1)

    tile_max = jnp.max(kn_ref[0], axis=0)
    rows = lax.broadcasted_iota(jnp.int32, tile_max.shape, 0)
    kmax2 = jnp.max(jnp.where(rows == head, tile_max, 0.0), axis=0, keepdims=True)[:, 0:1]

    def finish(num, den, sub):
        o = num / den
        if diff:
            dl_ref, sg_ref = refs[4:6]
            dl = dl_ref[...]
            lam = (jnp.exp(jnp.sum(dl[0:1] * dl[1:2], axis=1, keepdims=True))
                   - jnp.exp(jnp.sum(dl[2:3] * dl[3:4], axis=1, keepdims=True)) + lam_init)
            o = o[:, :tq] - lam * o[:, tq:]
            o = o * lax.rsqrt(jnp.mean(o * o, axis=0, keepdims=True) + EPS)
            o = o * sg_ref[...] * (1.0 - lam_init)
        o_ref[0, sub] = o.astype(o_ref.dtype)

    def bounded_shift(rhs, bound, sub):
        acc = jnp.zeros((MLA_V, rhs.shape[1]), F32)
        den = jnp.zeros((1, rhs.shape[1]), F32)
        pieces = lambda c: [(lo, min(lo + ATT_ROWS, chunks[c][1])) for lo in range(*chunks[c], ATT_ROWS)]
        logits = lambda lo, hi: _dot(k_ref[0, 0, lo:hi, :], rhs)
        s_next = [logits(lo, hi) for lo, hi in pieces(0)]
        for c in range(nc):
            s_cur, s_next = s_next, []
            nxt = pieces(c + 1) if c + 1 < nc else []
            for i, (lo, hi) in enumerate(pieces(c)):
                if i < len(nxt):
                    s_next.append(logits(*nxt[i]))
                p = jnp.exp2(s_cur[i] - bound)
                den = den + jnp.sum(p, axis=0, keepdims=True)
                acc = acc + _dot(v_ref[0, 0:MLA_V, lo:hi], p.astype(BF16))
            s_next += [logits(lo, hi) for lo, hi in nxt[len(pieces(c)):]]
        finish(acc, den, sub)

    def running_max(rhs, sub, zero_row):
        n = rhs.shape[1]

        def scores(c):
            lo, hi = chunks[c]
            s = _dot(k_ref[0, 0, lo:hi, :], rhs)
            s_bufs[c % n_buf][pl.ds(zero_row, hi - lo), :] = s
            return jnp.max(s, axis=0, keepdims=True)

        m = jnp.full((1, n), -1e30, F32)
        acc = jnp.zeros((V_ROWS, n), F32)
        cmaxes = {c: scores(c) for c in range(min(ATT_AHEAD, nc))}
        for c in range(nc):
            lo, hi = chunks[c]
            m_new = jnp.maximum(m, cmaxes.pop(c))
            alpha = jnp.exp2(m - m_new)
            if c + ATT_AHEAD < nc:
                cmaxes[c + ATT_AHEAD] = scores(c + ATT_AHEAD)
            p = jnp.exp2(s_bufs[c % n_buf][pl.ds(zero_row, hi - lo), :] - m_new)
            m, acc = m_new, alpha * acc + _dot(v_ref[0, :, lo:hi], p.astype(BF16))
        finish(acc[0:MLA_V], acc[MLA_V:MLA_V + 1], sub)

    def queries(sub):
        q = q_ref[0, 0, sub]
        if diff:
            first = (lax.broadcasted_iota(jnp.int32, q.shape, 0) & DIFF_HD) == 0
            zero = jnp.zeros_like(q)
            rhs = jnp.concatenate([jnp.where(first, q, zero), jnp.where(first, zero, q)], axis=1)
        else:
            rhs = q
        qf = rhs.astype(F32)
        return rhs, jnp.sqrt(jnp.sum(qf * qf, axis=0, keepdims=True) * kmax2)

    def for_each_block(fn):
        if nsub == 1:
            fn(0)
        else:
            lax.fori_loop(0, nsub, lambda sub, carry: (fn(sub), carry)[1], 0)

    qa = q_ref[0, 0].astype(F32)
    q_norm2 = jnp.max(jnp.sum(qa * qa, axis=1, keepdims=True))
    small = q_norm2 * jnp.max(kmax2) <= ATT_BOUND_MAX * ATT_BOUND_MAX

    @pl.when(small)
    def _():
        for_each_block(lambda sub: bounded_shift(*queries(sub), sub))

    @pl.when(jnp.logical_not(small))
    def _():
        zero_row = pl.multiple_of(jnp.minimum(pl.program_id(2), 0), 16)
        for_each_block(lambda sub: running_max(queries(sub)[0], sub, zero_row))


def _key_chunks(nk):
    head = [c for c in CHUNK_HEAD if sum(CHUNK_HEAD) + sum(CHUNK_TAIL) + TK <= nk]
    tail = CHUNK_TAIL if head else ()
    sizes = list(head)
    body = nk - sum(head) - sum(tail)
    sizes += [TK] * (body // TK) + ([body % TK] if body % TK else [])
    sizes += list(tail)
    edges = np.cumsum([0] + sizes)
    assert edges[-1] == nk and all(s % LANES == 0 for s in sizes)
    return tuple((int(a), int(b)) for a, b in zip(edges[:-1], edges[1:]))


def _attention(qT, k, vT, kn, *, tq, nsub, steps, q_block0, k_rows, k_block0, chunks,
               diff, extra=(), layer=0, lam_init=0.0, prev_out=None):
    b, heads, nqb, _, tqw = qT.shape
    dv = MLA_V
    kdiv = 2 if diff else 1
    in_specs = [
        pl.BlockSpec((1, 1, nsub, LANES, tq), lambda i, h, j: (i, h, q_block0 + j, 0, 0)),
        pl.BlockSpec((1, 1, k_rows, LANES), lambda i, h, j: (i, h // kdiv, k_block0, 0)),
        pl.BlockSpec((1, V_ROWS, k_rows), lambda i, h, j: (i, h, k_block0)),
        pl.BlockSpec((1,) + kn.shape[1:], lambda i, h, j: (i, 0, 0, 0)),
    ]
    args = [qT, k, vT, kn]
    if diff:
        in_specs += [_layer_spec(a, layer) for a in extra]
        args += list(extra)
    aliases = {}
    if prev_out is not None:
        in_specs.append(pl.BlockSpec(memory_space=pl.ANY))
        args.append(prev_out)
        aliases = {len(args) - 1: 0}
    return pl.pallas_call(
        functools.partial(_attn_kernel, chunks=chunks, diff=diff, lam_init=lam_init,
                          aliased=prev_out is not None),
        grid=(b, heads, steps),
        in_specs=in_specs,
        out_specs=pl.BlockSpec((1, nsub, dv, tq), lambda i, h, j: (i, q_block0 + j, h, 0)),
        out_shape=jax.ShapeDtypeStruct((b, nqb, heads * dv, tqw), BF16),
        scratch_shapes=[pltpu.VMEM((max(hi - lo for lo, hi in chunks), (2 * tq if diff else tq)), dt)
                        for dt in [F32] * (ATT_AHEAD + 1)],
        input_output_aliases=aliases,
        compiler_params=pltpu.CompilerParams(vmem_limit_bytes=VMEM_LIMIT, flags=ATT_FLAGS),
        name=("diff" if diff else "mla") + ("_ctx" if prev_out is not None else "_x"),
    )(*args)


def _mix_kernel(x_ref, mod_ref, oa_ref, od_ref, pc_ref, pp_ref, pn_ref, gt_ref, bg_ref,
                pproj_ref, pb_ref, ps_ref, wa_ref, wd_ref, wp_ref, wo_ref, o_ref, *, x_tiles):
    t = pl.program_id(1)
    x = x_ref[0]
    gate = mod_ref[0][5:6]
    tm, d = x.shape
    a = lax.dot_general(oa_ref[0, 0], wa_ref[...], TN_DIMS, preferred_element_type=F32)
    dd = lax.dot_general(od_ref[0, 0], wd_ref[...], TN_DIMS, preferred_element_type=F32)

    is_ctx = t == x_tiles
    has_prev = jnp.logical_and(t != 0, jnp.logical_not(is_ctx))
    has_next = jnp.logical_and(t != x_tiles - 1, jnp.logical_not(is_ctx))
    cur = pc_ref[0]
    prev = jnp.where(has_prev, pp_ref[0], 0.0)
    nxt = jnp.where(has_next, pn_ref[0], 0.0)
    ext = jnp.concatenate([prev, cur, nxt], axis=0)
    rows = ext.shape[0]
    seq_len = jnp.where(is_ctx, tm, x_tiles * tm)
    pos = jnp.where(is_ctx, 0, t * tm) + lax.broadcasted_iota(jnp.int32, (tm, 1), 0)
    outs = []
    for g, w in enumerate(POOL_WINDOWS):
        lanes = slice(g * POOL_G, (g + 1) * POOL_G)
        run = ext[:, lanes]
        span = 1
        while span < w:
            run = run + pltpu.roll(run, rows - span, axis=0)
            span *= 2
        win = pltpu.roll(run, rows - (POOL_HALO - w // 2), axis=0)[0:tm]
        cnt = (jnp.minimum(pos + w // 2, seq_len) - jnp.maximum(pos - w // 2, 0)).astype(F32)
        pooled = (win / cnt - cur[:, lanes]).astype(BF16)
        outs.append((_dot(pooled, pproj_ref[g]) + pb_ref[:, lanes]) * ps_ref[:, lanes])
    pooled_out = _dot(jnp.concatenate(outs, axis=1).astype(BF16), wp_ref[...])

    gs = jax.nn.sigmoid(gt_ref[0].astype(F32) + bg_ref[...])
    merged = gs[:, 0:d] * a + gs[:, d:2 * d] * dd + gs[:, 2 * d:3 * d] * pooled_out
    o_ref[0] = x + gate * _dot(merged.astype(BF16), wo_ref[...])


def _mix(xs, mod, oa, od, pool_in, gates, w, *, l, mod_row, x_tiles):
    b, nt, d = xs.shape
    hb = TM // POOL_HALO
    n_halo = nt // POOL_HALO
    tile = pl.BlockSpec((1, TM, d), lambda i, t: (i, t, 0))
    pw = 4 * POOL_G
    weights = [w["bgate"], w["pproj"], w["pb"], w["ps"], w["wa"], w["wd"], w["wp"], w["wo"]]

    def o_spec(o):
        r = o.shape[3] // TM
        return pl.BlockSpec((1, 1, o.shape[2], TM), lambda i, t: (i, t // r, 0, t % r))

    return pl.pallas_call(
        functools.partial(_mix_kernel, x_tiles=x_tiles),
        grid=(b, nt // TM),
        in_specs=[tile, _mod_spec(mod, l, mod_row),
                  o_spec(oa), o_spec(od),
                  pl.BlockSpec((1, TM, pw), lambda i, t: (i, t, 0)),
                  pl.BlockSpec((1, POOL_HALO, pw), lambda i, t: (i, jnp.maximum(t * hb - 1, 0), 0)),
                  pl.BlockSpec((1, POOL_HALO, pw), lambda i, t: (i, jnp.minimum((t + 1) * hb, n_halo - 1), 0)),
                  pl.BlockSpec((1, TM, 3 * d), lambda i, t: (i, t, 0))]
                 + [_layer_spec(a, l) for a in weights],
        out_specs=tile,
        out_shape=jax.ShapeDtypeStruct(xs.shape, F32),
        input_output_aliases={0: 0},
        compiler_params=_params(),
        name="mix",
    )(xs, mod, oa, od, pool_in, pool_in, pool_in, gates, *weights)


def _rope_tables(seq, ctx):
    rows = seq // GRID_W
    row_ids = jnp.repeat(jnp.arange(rows), GRID_W).astype(F32)
    col_ids = jnp.tile(jnp.arange(GRID_W), rows).astype(F32)
    n_freq = ROPE_HALF // 2
    inv_freq = ROPE_THETA ** (-jnp.arange(n_freq, dtype=F32) / n_freq)
    ang = jnp.concatenate([row_ids[:, None] * inv_freq, col_ids[:, None] * inv_freq], axis=-1)
    cos = jnp.concatenate([jnp.cos(ang), jnp.ones((ctx, ROPE_HALF), F32)], axis=0)
    sin = jnp.concatenate([jnp.sin(ang), jnp.zeros((ctx, ROPE_HALF), F32)], axis=0)
    pad = jnp.zeros((seq + ctx, LANES - 2 * ROPE_HALF), F32)
    return {
        "cosT": jnp.tile(cos.T, (HEADS, 1)), "sinT": jnp.tile(sin.T, (HEADS, 1)),
        "cosN": jnp.tile(cos, (1, HEADS)), "sinN": jnp.tile(sin, (1, HEADS)),
        "c128": jnp.concatenate([cos, cos, pad], axis=1),
        "s128": jnp.concatenate([sin, sin, pad], axis=1),
    }


def _placement_matrices():
    hw = ROPE_HALF
    perm = np.zeros((HEADS * 2 * DIFF_HD, HEADS * 2 * DIFF_HD), np.float32)
    for h in range(HEADS):
        for c in range(2):
            for half in range(2):
                for f in range(hw):
                    perm[(c * 2 + half) * LANES + h * hw + f, h * 2 * DIFF_HD + c * DIFF_HD + half * hw + f] = 1.0
    place = np.zeros((LANES, HEADS * LANES), np.float32)
    for h in range(HEADS):
        for f in range(MLA_ROPE):
            place[f, h * LANES + MLA_NOPE + f] = 1.0
    return jnp.asarray(place, BF16), jnp.asarray(perm, BF16)


def _stacked_weights(p):
    w_in = p["w_in"]
    nl, d, _ = w_in.shape
    hw = ROPE_HALF
    q_lora, kv_lora = p["mla_w_uq"].shape[1], p["mla_w_ukv"].shape[1]
    diff_w = HEADS * 2 * DIFF_HD
    widths = (q_lora, kv_lora, MLA_ROPE, diff_w, diff_w, HEADS * DIFF_V, 4 * POOL_G)
    o_cq, o_ckv, o_kr, o_dq, o_dk, o_dv, o_pool, o_gate = (int(v) for v in np.cumsum((0,) + widths))
    bf = lambda a: a.astype(BF16)
    t = lambda a: jnp.swapaxes(a, 1, 2)

    def block_order(cols):
        a = cols.reshape(nl, d, HEADS, 2, 2, hw)
        return a.transpose(0, 1, 3, 4, 2, 5).reshape(nl, d, HEADS * 2 * DIFF_HD)

    wkr = w_in[:, :, o_kr:o_kr + MLA_ROPE]
    wkr_rot = jnp.concatenate([-wkr[:, :, hw:], wkr[:, :, :hw]], axis=2)
    zpad = jnp.zeros((nl, d, LANES - MLA_ROPE), F32)
    wkr2 = jnp.concatenate([wkr, zpad, wkr_rot, zpad], axis=2)

    w_uq = p["mla_w_uq"].reshape(nl, q_lora, HEADS, MLA_QK)
    wuq = jnp.concatenate([w_uq[..., :MLA_NOPE].reshape(nl, q_lora, -1),
                           w_uq[..., MLA_NOPE:MLA_NOPE + hw].reshape(nl, q_lora, -1),
                           w_uq[..., MLA_NOPE + hw:].reshape(nl, q_lora, -1)], axis=2)
    w_ukv = p["mla_w_ukv"].reshape(nl, kv_lora, HEADS, MLA_NOPE + MLA_V)
    is_key = (jnp.arange(MLA_NOPE + MLA_V) < MLA_NOPE).astype(F32)
    wkpad = (w_ukv * is_key).reshape(nl, kv_lora, -1)
    wv = w_ukv[..., MLA_NOPE:].reshape(nl, kv_lora, -1)
    place, perm = _placement_matrices()
    return {
        "wcq": bf(w_in[:, :, o_cq:o_ckv]), "wckv": bf(w_in[:, :, o_ckv:o_kr]), "wkr2": bf(wkr2),
        "wdqT": bf(t(block_order(w_in[:, :, o_dq:o_dk]))), "wdk": bf(block_order(w_in[:, :, o_dk:o_dv])),
        "wdvT": bf(t(w_in[:, :, o_dv:o_pool])), "wpool": bf(w_in[:, :, o_pool:o_gate]), "wgates": bf(w_in[:, :, o_gate:]),
        "qng": p["mla_q_norm_g"][:, None], "kvng": p["mla_kv_norm_g"][:, None],
        "wuqT": bf(t(wuq)), "wkpad": bf(wkpad), "wvT": bf(t(wv)),
        "place": place, "perm": perm,
        "bgate": p["b_gate"].reshape(nl, 1, -1), "pproj": bf(p["pool_proj"]),
        "pb": p["pool_b"].reshape(nl, 1, -1), "ps": p["pool_scale"][:, None],
        "wa": bf(p["w_br_mla"]), "wd": bf(p["w_br_diff"]), "wp": bf(p["w_br_pool"]), "wo": bf(p["w_out"]),
    }


def kernel(x, c, ctx, c_ctx, ada_w, ada_b, norm_g, ffa_w_gate, ffa_w_up, ffa_w_down, ffb_w_gate, ffb_w_up, ffb_w_down, w_in, b_gate, mla_q_norm_g, mla_kv_norm_g, mla_w_uq, mla_w_ukv, diff_lambda, diff_subln_g, pool_proj, pool_b, pool_scale, w_br_mla, w_br_diff, w_br_pool, w_out, final_g):
    b, seq, d = x.shape
    n_ctx = ctx.shape[1]
    depth = ada_w.shape[0]
    nt = seq + n_ctx
    nsub = ATT_NSUB if seq % (ATT_NSUB * TQ_MLA) == 0 else 1
    assert n_ctx == TM and seq % (nsub * TQ_MLA) == 0 and seq % GRID_W == 0 and b + 1 <= MOD_ROWS
    x_tiles = seq // TM
    p = dict(w_in=w_in, b_gate=b_gate, mla_q_norm_g=mla_q_norm_g, mla_kv_norm_g=mla_kv_norm_g,
             mla_w_uq=mla_w_uq, mla_w_ukv=mla_w_ukv, pool_proj=pool_proj, pool_b=pool_b,
             pool_scale=pool_scale, w_br_mla=w_br_mla, w_br_diff=w_br_diff, w_br_pool=w_br_pool, w_out=w_out)

    c_rows = jnp.concatenate([c, c_ctx[None], jnp.zeros((MOD_ROWS - b - 1, d), F32)], axis=0)
    mod = _modulation(c_rows, ada_w, ada_b).reshape(depth, MOD_ROWS, N_ADA, d)
    mod_row = lambda i, t: jnp.where(t == x_tiles, b, i)

    tables = _rope_tables(seq, n_ctx)
    w = _stacked_weights(p)
    gains = norm_g[:, :, None, :]
    bf = lambda a: a.astype(BF16)
    ffa = (bf(ffa_w_gate), bf(ffa_w_up), bf(ffa_w_down))
    ffb = (bf(ffb_w_gate), bf(ffb_w_up), bf(ffb_w_down))
    extra = (diff_lambda, diff_subln_g[:, :, None])
    chunks = _key_chunks(nt)

    xs = x
    for l in range(depth):
        lam_init = 0.8 - 0.6 * math.exp(-0.3 * l)
        xs = _ffn(xs, mod, gains, *ffa, l=l, mi=0, mod_row=mod_row, ctx=ctx if l == 0 else None)
        q_mla, k_mla, v_mla, q_diff, k_diff, v_diff, pool_in, gates, kn_mla, kn_diff = _inproj(
            xs, mod, gains, w, tables, l=l, mod_row=mod_row)
        mla = (q_mla, k_mla, v_mla, kn_mla)
        dif = (q_diff, k_diff, v_diff, kn_diff)
        full = dict(nsub=nsub, q_block0=0, k_rows=nt, k_block0=0, chunks=chunks)
        ctx_only = dict(tq=TM, nsub=1, steps=1, k_rows=TM, k_block0=x_tiles, chunks=((0, TM),))
        oa = _attention(*mla, tq=TQ_MLA, steps=seq // (nsub * TQ_MLA), diff=False, **full)
        oa = _attention(*mla, q_block0=seq // TQ_MLA, diff=False, prev_out=oa, **ctx_only)
        od = _attention(*dif, tq=TQ_DIFF, steps=seq // (nsub * TQ_DIFF), diff=True,
                        extra=extra, layer=l, lam_init=lam_init, **full)
        od = _attention(*dif, q_block0=seq // TQ_DIFF, diff=True, extra=extra, layer=l,
                        lam_init=lam_init, prev_out=od, **ctx_only)
        xs = _mix(xs, mod, oa, od, pool_in, gates, w, l=l, mod_row=mod_row, x_tiles=x_tiles)
        if l + 1 < depth:
            xs = _ffn(xs, mod, gains, *ffb, l=l, mi=2, mod_row=mod_row)
    return _ffn(xs, mod, gains, *ffb, l=depth - 1, mi=2, mod_row=mod_row, final=(final_g[None], x_tiles))[1]
```

```python
import functools
import math

import numpy as np
import jax
import jax.numpy as jnp
from jax import lax
from jax.experimental import pallas as pl
from jax.experimental.pallas import tpu as pltpu

F32 = jnp.float32
BF16 = jnp.bfloat16

EPS = 1e-6
ROPE_THETA = 10000.0
GRID_W = 64
N_ADA = 9

HEADS = 8
MLA_NOPE = 64
MLA_ROPE = 32
MLA_V = 64
V_ROWS = MLA_V + 16
MLA_QK = MLA_NOPE + MLA_ROPE
LOG2E = math.log2(math.e)
MLA_SCALE = MLA_QK ** -0.5 * LOG2E
DIFF_HD = 32
DIFF_V = 64
DIFF_SCALE = DIFF_HD ** -0.5 * LOG2E
ROPE_HALF = 16
POOL_WINDOWS = (2, 4, 8, 16)
POOL_G = 128
POOL_HALO = 16

LANES = 128
MOD_ROWS = 8
MOD_COL_BLOCKS = 8
TM = 256
TK = 512
CHUNK_HEAD = (256,)
CHUNK_TAIL = ()
ATT_AHEAD = 2
ATT_ROWS = 256
ATT_BOUND_MAX = 40.0
ATT_NSUB = 4
TQ_MLA = 1024
TQ_DIFF = 512
VMEM_LIMIT = 52 * 1024 * 1024

ATT_FLAGS = {}

NT_DIMS = (((1,), (1,)), ((), ()))
TN_DIMS = (((0,), (0,)), ((), ()))


def _params():
    return pltpu.CompilerParams(vmem_limit_bytes=VMEM_LIMIT)


def _const_spec(shape):
    zeros = (0,) * len(shape)
    return pl.BlockSpec(shape, lambda *_: zeros, pipeline_mode=pl.Buffered(1))


def _layer_spec(a, *lead):
    tail = a.shape[len(lead):]
    zeros = (0,) * len(tail)
    return pl.BlockSpec((None,) * len(lead) + tail, lambda *_: tuple(lead) + zeros, pipeline_mode=pl.Buffered(1))


def _mod_spec(mod, l, mod_row):
    return pl.BlockSpec((None, 1) + mod.shape[2:], lambda i, t: (l, mod_row(i, t), 0, 0))


def _dot(a, b):
    return jnp.dot(a, b, preferred_element_type=F32)


def _rms(x):
    return x * lax.rsqrt(jnp.mean(x * x, axis=-1, keepdims=True) + EPS)


def _mod_kernel(c_ref, w_ref, b_ref, o_ref):
    c = c_ref[...]
    a = c * jax.nn.sigmoid(c)
    o_ref[0] = jnp.dot(a, w_ref[0], preferred_element_type=F32,
                       precision=lax.Precision.HIGHEST) + b_ref[0]


def _modulation(c_rows, ada_w, ada_b):
    depth, d, n = ada_w.shape
    bn = n // MOD_COL_BLOCKS
    return pl.pallas_call(
        _mod_kernel,
        grid=(depth, MOD_COL_BLOCKS),
        in_specs=[pl.BlockSpec((MOD_ROWS, d), lambda l, j: (0, 0)),
                  pl.BlockSpec((1, d, bn), lambda l, j: (l, 0, j)),
                  pl.BlockSpec((1, 1, bn), lambda l, j: (l, 0, j))],
        out_specs=pl.BlockSpec((1, MOD_ROWS, bn), lambda l, j: (l, 0, j)),
        out_shape=jax.ShapeDtypeStruct((depth, MOD_ROWS, n), F32),
        compiler_params=_params(),
        name="modulation",
    )(c_rows, ada_w, ada_b.reshape(depth, 1, n))


def _swiglu_half_step(x, m, g_ref, wg_ref, wu_ref, wd_ref, mi, f_chunks):
    shift, scale, gate = m[3 * mi:3 * mi + 1], m[3 * mi + 1:3 * mi + 2], m[3 * mi + 2:3 * mi + 3]
    u = (_rms(x) * g_ref[...]) * (1.0 + scale) + shift
    ub = u.astype(BF16)
    y = jnp.zeros(x.shape, F32)
    for lo, hi in f_chunks:
        a = _dot(ub, wg_ref[:, lo:hi])
        b = _dot(ub, wu_ref[:, lo:hi])
        hid = (a * jax.nn.sigmoid(a) * b).astype(BF16)
        y = y + _dot(hid, wd_ref[lo:hi, :])
    return x + (0.5 * gate) * y


def _ffn_kernel(x_ref, mod_ref, g_ref, wg_ref, wu_ref, wd_ref, *rest, mi, f_chunks, ctx_tile):
    o_ref = rest[-1]
    x = x_ref[0]
    if ctx_tile is not None:
        x = jnp.where(pl.program_id(1) == ctx_tile, rest[0][0], x)
    o_ref[0] = _swiglu_half_step(x, mod_ref[0], g_ref, wg_ref, wu_ref, wd_ref, mi, f_chunks)


def _ffn(xs, mod, gains, wg, wu, wd, *, l, mi, mod_row, ctx=None):
    b, nt, d = xs.shape
    f = wg.shape[2]
    chunk = 1024
    f_chunks = tuple((lo, min(lo + chunk, f)) for lo in range(0, f, chunk))
    tile = pl.BlockSpec((1, TM, d), lambda i, t: (i, t, 0))
    in_specs = [tile, _mod_spec(mod, l, mod_row), _layer_spec(gains, l, mi),
                _layer_spec(wg, l), _layer_spec(wu, l), _layer_spec(wd, l)]
    args = [xs, mod, gains, wg, wu, wd]
    aliases, ctx_tile = {0: 0}, None
    if ctx is not None:
        ctx_tile = nt // TM
        nt += ctx.shape[1]
        in_specs[0] = pl.BlockSpec((1, TM, d), lambda i, t: (i, jnp.minimum(t, ctx_tile - 1), 0))
        in_specs.append(pl.BlockSpec((1, TM, d), lambda i, t: (i, 0, 0)))
        args.append(ctx)
        aliases = {}
    return pl.pallas_call(
        functools.partial(_ffn_kernel, mi=mi, f_chunks=f_chunks, ctx_tile=ctx_tile),
        grid=(b, nt // TM),
        in_specs=in_specs,
        out_specs=tile,
        out_shape=jax.ShapeDtypeStruct((b, nt, d), F32),
        input_output_aliases=aliases,
        compiler_params=_params(),
        name=f"ffn{mi}",
    )(*args)


def _inproj_kernel(x_ref, mod_ref, g_ref, wcq, wckv, wkr2, wdqT, wdk, wdvT, wpool, wgates,
                   qng, kvng, wuqT, wkpad, wvT, place, perm,
                   cosT, sinT, cosN, sinN, c128, s128,
                   oq_mla, ok_mla, ov_mla, oq_diff, ok_diff, ov_diff, opool, ogates, okn_mla, okn_diff):
    x = x_ref[0]
    m = mod_ref[0]
    u = (_rms(x) * g_ref[...]) * (1.0 + m[4:5]) + m[3:4]
    ub = u.astype(BF16)
    tm = x.shape[0]
    cT, sT = cosT[...], sinT[...]
    hw = ROPE_HALF

    cqn = (_rms(_dot(ub, wcq[...])) * qng[...]).astype(BF16)
    qT = lax.dot_general(wuqT[...], cqn, NT_DIMS, preferred_element_type=F32) * MLA_SCALE
    nope_w = HEADS * MLA_NOPE
    x1, x2 = qT[nope_w:nope_w + LANES], qT[nope_w + LANES:nope_w + 2 * LANES]
    qn = qT[0:nope_w].astype(BF16)
    r1 = (x1 * cT - x2 * sT).astype(BF16)
    r2 = (x1 * sT + x2 * cT).astype(BF16)
    zpad = jnp.zeros((LANES - MLA_QK, tm), BF16)
    for h in range(HEADS):
        oq_mla[0, h, 0, 0:MLA_NOPE, :] = qn[h * MLA_NOPE:(h + 1) * MLA_NOPE]
        oq_mla[0, h, 0, MLA_NOPE:MLA_NOPE + hw, :] = r1[h * hw:(h + 1) * hw]
        oq_mla[0, h, 0, MLA_NOPE + hw:MLA_QK, :] = r2[h * hw:(h + 1) * hw]
        oq_mla[0, h, 0, MLA_QK:LANES, :] = zpad

    ckvn = (_rms(_dot(ub, wckv[...])) * kvng[...]).astype(BF16)
    kr2 = _dot(ub, wkr2[...])
    krr = (kr2[:, 0:LANES] * c128[...] + kr2[:, LANES:2 * LANES] * s128[...]).astype(BF16)
    kall = (_dot(ckvn, wkpad[...]) + _dot(krr, place[...])).astype(BF16)
    vT = lax.dot_general(wvT[...], ckvn, NT_DIMS, preferred_element_type=F32).astype(BF16)
    head_row = lax.broadcasted_iota(jnp.int32, (HEADS, LANES), 0)
    left_half = lax.broadcasted_iota(jnp.int32, (tm, LANES), 1) < 2 * DIFF_HD

    def max_sq_norm(sq):
        return jnp.max(jnp.sum(sq, axis=1, keepdims=True), axis=0, keepdims=True)

    kn = jnp.zeros((HEADS, LANES), F32)
    for h in range(HEADS):
        kh = kall[:, h * LANES:(h + 1) * LANES]
        ok_mla[0, h] = kh
        khf = kh.astype(F32)
        kn = jnp.where(head_row == h, max_sq_norm(khf * khf), kn)
    okn_mla[0, 0] = kn
    ones_rows = (lax.broadcasted_iota(jnp.int32, (V_ROWS - MLA_V, tm), 0) == 0).astype(BF16)

    def store_values(ov, vals):
        for h in range(HEADS):
            ov[0, h * V_ROWS:h * V_ROWS + MLA_V, :] = vals[h * MLA_V:(h + 1) * MLA_V]
            ov[0, h * V_ROWS + MLA_V:(h + 1) * V_ROWS, :] = ones_rows

    store_values(ov_mla, vT)

    dqT = lax.dot_general(wdqT[...], ub, NT_DIMS, preferred_element_type=F32) * DIFF_SCALE
    a1, a2, b1, b2 = (dqT[i * LANES:(i + 1) * LANES] for i in range(4))
    parts = [(a1 * cT - a2 * sT).astype(BF16), (a1 * sT + a2 * cT).astype(BF16),
             (b1 * cT - b2 * sT).astype(BF16), (b1 * sT + b2 * cT).astype(BF16)]
    zhalf = jnp.zeros((2 * DIFF_HD, tm), BF16)
    for h in range(HEADS):
        base = (h % 2) * 2 * DIFF_HD
        for i, part in enumerate(parts):
            oq_diff[0, h, 0, base + i * hw:base + (i + 1) * hw, :] = part[h * hw:(h + 1) * hw]
        other = 2 * DIFF_HD - base
        oq_diff[0, h, 0, other:other + 2 * DIFF_HD, :] = zhalf

    dk = _dot(ub, wdk[...])
    cN, sN = cosN[...], sinN[...]
    k1a, k1b, k2a, k2b = (dk[:, i * LANES:(i + 1) * LANES] for i in range(4))
    rk = jnp.concatenate([k1a * cN - k1b * sN, k1a * sN + k1b * cN,
                          k2a * cN - k2b * sN, k2a * sN + k2b * cN], axis=1).astype(BF16)
    kd = _dot(rk, perm[...]).astype(BF16)
    kn = jnp.zeros((HEADS, LANES), F32)
    for p in range(HEADS // 2):
        kp = kd[:, p * LANES:(p + 1) * LANES]
        ok_diff[0, p] = kp
        kpf = kp.astype(F32)
        sq = kpf * kpf
        kn = jnp.where(head_row == 2 * p, max_sq_norm(jnp.where(left_half, sq, 0.0)), kn)
        kn = jnp.where(head_row == 2 * p + 1, max_sq_norm(jnp.where(left_half, 0.0, sq)), kn)
    okn_diff[0, 0] = kn
    store_values(ov_diff, lax.dot_general(wdvT[...], ub, NT_DIMS, preferred_element_type=F32).astype(BF16))

    opool[0] = _dot(ub, wpool[...])
    ogates[0] = _dot(ub, wgates[...]).astype(BF16)


def _inproj(xs, mod, gains, w, tables, *, l, mod_row):
    b, nt, d = xs.shape
    tile = pl.BlockSpec((1, TM, d), lambda i, t: (i, t, 0))
    tabT = pl.BlockSpec((LANES, TM), lambda i, t: (0, t))
    tabN = pl.BlockSpec((TM, LANES), lambda i, t: (t, 0))
    stacked = [w["wcq"], w["wckv"], w["wkr2"], w["wdqT"], w["wdk"], w["wdvT"], w["wpool"], w["wgates"],
               w["qng"], w["kvng"], w["wuqT"], w["wkpad"], w["wvT"]]
    shared = [w["place"], w["perm"]]
    vT_spec = pl.BlockSpec((1, HEADS * V_ROWS, TM), lambda i, t: (i, 0, t))

    def q_spec(tq):
        r = tq // TM
        return pl.BlockSpec((1, HEADS, 1, LANES, TM), lambda i, t: (i, 0, t // r, 0, t % r))

    out_shapes = [
        jax.ShapeDtypeStruct((b, HEADS, pl.cdiv(nt, TQ_MLA), LANES, TQ_MLA), BF16),
        jax.ShapeDtypeStruct((b, HEADS, nt, LANES), BF16),
        jax.ShapeDtypeStruct((b, HEADS * V_ROWS, nt), BF16),
        jax.ShapeDtypeStruct((b, HEADS, pl.cdiv(nt, TQ_DIFF), LANES, TQ_DIFF), BF16),
        jax.ShapeDtypeStruct((b, HEADS // 2, nt, LANES), BF16),
        jax.ShapeDtypeStruct((b, HEADS * V_ROWS, nt), BF16),
        jax.ShapeDtypeStruct((b, nt, 4 * POOL_G), F32),
        jax.ShapeDtypeStruct((b, nt, 3 * d), BF16),
        jax.ShapeDtypeStruct((b, nt // TM, HEADS, LANES), F32),
        jax.ShapeDtypeStruct((b, nt // TM, HEADS, LANES), F32),
    ]
    kn_spec = pl.BlockSpec((1, 1, HEADS, LANES), lambda i, t: (i, t, 0, 0))
    out_specs = [
        q_spec(TQ_MLA),
        pl.BlockSpec((1, HEADS, TM, LANES), lambda i, t: (i, 0, t, 0)),
        vT_spec,
        q_spec(TQ_DIFF),
        pl.BlockSpec((1, HEADS // 2, TM, LANES), lambda i, t: (i, 0, t, 0)),
        vT_spec,
        pl.BlockSpec((1, TM, 4 * POOL_G), lambda i, t: (i, t, 0)),
        pl.BlockSpec((1, TM, 3 * d), lambda i, t: (i, t, 0)),
        kn_spec, kn_spec,
    ]
    return pl.pallas_call(
        _inproj_kernel,
        grid=(b, nt // TM),
        in_specs=([tile, _mod_spec(mod, l, mod_row), _layer_spec(gains, l, 1)]
                  + [_layer_spec(a, l) for a in stacked] + [_const_spec(a.shape) for a in shared]
                  + [tabT, tabT, tabN, tabN, tabN, tabN]),
        out_specs=out_specs,
        out_shape=out_shapes,
        compiler_params=_params(),
        name="inproj",
    )(xs, mod, gains, *stacked, *shared, tables["cosT"], tables["sinT"], tables["cosN"], tables["sinN"],
      tables["c128"], tables["s128"])


def _attn_kernel(*refs, chunks, diff, lam_init, aliased):
    refs = list(refs)
    q_ref, k_ref, v_ref, kn_ref = refs[:4]
    n_buf = ATT_AHEAD + 1
    s_bufs = refs[-n_buf:]
    o_ref = refs[-n_buf - 1]
    assert len(refs) == 5 + 2 * diff + aliased + n_buf
    nsub, tq = q_ref.shape[2], q_ref.shape[4]
    nc = len(chunks)
    head = pl.program_id(1)

    tile_max = jnp.max(kn_ref[0], axis=0)
    rows = lax.broadcasted_iota(jnp.int32, tile_max.shape, 0)
    kmax2 = jnp.max(jnp.where(rows == head, tile_max, 0.0), axis=0, keepdims=True)[:, 0:1]

    def finish(num, den, sub):
        o = num / den
        if diff:
            dl_ref, sg_ref = refs[4:6]
            dl = dl_ref[...]
            lam = (jnp.exp(jnp.sum(dl[0:1] * dl[1:2], axis=1, keepdims=True))
                   - jnp.exp(jnp.sum(dl[2:3] * dl[3:4], axis=1, keepdims=True)) + lam_init)
            o = o[:, :tq] - lam * o[:, tq:]
            o = o * lax.rsqrt(jnp.mean(o * o, axis=0, keepdims=True) + EPS)
            o = o * sg_ref[...] * (1.0 - lam_init)
        o_ref[0, sub] = o.astype(o_ref.dtype)

    def bounded_shift(rhs, bound, sub):
        acc = jnp.zeros((MLA_V, rhs.shape[1]), F32)
        den = jnp.zeros((1, rhs.shape[1]), F32)
        pieces = lambda c: [(lo, min(lo + ATT_ROWS, chunks[c][1])) for lo in range(*chunks[c], ATT_ROWS)]
        logits = lambda lo, hi: _dot(k_ref[0, 0, lo:hi, :], rhs)
        s_next = [logits(lo, hi) for lo, hi in pieces(0)]
        for c in range(nc):
            s_cur, s_next = s_next, []
            nxt = pieces(c + 1) if c + 1 < nc else []
            for i, (lo, hi) in enumerate(pieces(c)):
                if i < len(nxt):
                    s_next.append(logits(*nxt[i]))
                p = jnp.exp2(s_cur[i] - bound)
                den = den + jnp.sum(p, axis=0, keepdims=True)
                acc = acc + _dot(v_ref[0, 0:MLA_V, lo:hi], p.astype(BF16))
            s_next += [logits(lo, hi) for lo, hi in nxt[len(pieces(c)):]]
        finish(acc, den, sub)

    def running_max(rhs, sub, zero_row):
        n = rhs.shape[1]

        def scores(c):
            lo, hi = chunks[c]
            s = _dot(k_ref[0, 0, lo:hi, :], rhs)
            s_bufs[c % n_buf][pl.ds(zero_row, hi - lo), :] = s
            return jnp.max(s, axis=0, keepdims=True)

        m = jnp.full((1, n), -1e30, F32)
        acc = jnp.zeros((V_ROWS, n), F32)
        cmaxes = {c: scores(c) for c in range(min(ATT_AHEAD, nc))}
        for c in range(nc):
            lo, hi = chunks[c]
            m_new = jnp.maximum(m, cmaxes.pop(c))
            alpha = jnp.exp2(m - m_new)
            if c + ATT_AHEAD < nc:
                cmaxes[c + ATT_AHEAD] = scores(c + ATT_AHEAD)
            p = jnp.exp2(s_bufs[c % n_buf][pl.ds(zero_row, hi - lo), :] - m_new)
            m, acc = m_new, alpha * acc + _dot(v_ref[0, :, lo:hi], p.astype(BF16))
        finish(acc[0:MLA_V], acc[MLA_V:MLA_V + 1], sub)

    def queries(sub):
        q = q_ref[0, 0, sub]
        if diff:
            first = (lax.broadcasted_iota(jnp.int32, q.shape, 0) & DIFF_HD) == 0
            zero = jnp.zeros_like(q)
            rhs = jnp.concatenate([jnp.where(first, q, zero), jnp.where(first, zero, q)], axis=1)
        else:
            rhs = q
        qf = rhs.astype(F32)
        return rhs, jnp.sqrt(jnp.sum(qf * qf, axis=0, keepdims=True) * kmax2)

    def for_each_block(fn):
        if nsub == 1:
            fn(0)
        else:
            lax.fori_loop(0, nsub, lambda sub, carry: (fn(sub), carry)[1], 0)

    qa = q_ref[0, 0].astype(F32)
    q_norm2 = jnp.max(jnp.sum(qa * qa, axis=1, keepdims=True))
    small = q_norm2 * jnp.max(kmax2) <= ATT_BOUND_MAX * ATT_BOUND_MAX

    @pl.when(small)
    def _():
        for_each_block(lambda sub: bounded_shift(*queries(sub), sub))

    @pl.when(jnp.logical_not(small))
    def _():
        zero_row = pl.multiple_of(jnp.minimum(pl.program_id(2), 0), 16)
        for_each_block(lambda sub: running_max(queries(sub)[0], sub, zero_row))


def _key_chunks(nk):
    head = [c for c in CHUNK_HEAD if sum(CHUNK_HEAD) + sum(CHUNK_TAIL) + TK <= nk]
    tail = CHUNK_TAIL if head else ()
    sizes = list(head)
    body = nk - sum(head) - sum(tail)
    sizes += [TK] * (body // TK) + ([body % TK] if body % TK else [])
    sizes += list(tail)
    edges = np.cumsum([0] + sizes)
    assert edges[-1] == nk and all(s % LANES == 0 for s in sizes)
    return tuple((int(a), int(b)) for a, b in zip(edges[:-1], edges[1:]))


def _attention(qT, k, vT, kn, *, tq, nsub, steps, q_block0, k_rows, k_block0, chunks,
               diff, extra=(), layer=0, lam_init=0.0, prev_out=None):
    b, heads, nqb, _, tqw = qT.shape
    dv = MLA_V
    kdiv = 2 if diff else 1
    in_specs = [
        pl.BlockSpec((1, 1, nsub, LANES, tq), lambda i, h, j: (i, h, q_block0 + j, 0, 0)),
        pl.BlockSpec((1, 1, k_rows, LANES), lambda i, h, j: (i, h // kdiv, k_block0, 0)),
        pl.BlockSpec((1, V_ROWS, k_rows), lambda i, h, j: (i, h, k_block0)),
        pl.BlockSpec((1,) + kn.shape[1:], lambda i, h, j: (i, 0, 0, 0)),
    ]
    args = [qT, k, vT, kn]
    if diff:
        in_specs += [_layer_spec(a, layer) for a in extra]
        args += list(extra)
    aliases = {}
    if prev_out is not None:
        in_specs.append(pl.BlockSpec(memory_space=pl.ANY))
        args.append(prev_out)
        aliases = {len(args) - 1: 0}
    return pl.pallas_call(
        functools.partial(_attn_kernel, chunks=chunks, diff=diff, lam_init=lam_init,
                          aliased=prev_out is not None),
        grid=(b, heads, steps),
        in_specs=in_specs,
        out_specs=pl.BlockSpec((1, nsub, dv, tq), lambda i, h, j: (i, q_block0 + j, h, 0)),
        out_shape=jax.ShapeDtypeStruct((b, nqb, heads * dv, tqw), BF16),
        scratch_shapes=[pltpu.VMEM((max(hi - lo for lo, hi in chunks), (2 * tq if diff else tq)), dt)
                        for dt in [F32] * (ATT_AHEAD + 1)],
        input_output_aliases=aliases,
        compiler_params=pltpu.CompilerParams(vmem_limit_bytes=VMEM_LIMIT, flags=ATT_FLAGS),
        name=("diff" if diff else "mla") + ("_ctx" if prev_out is not None else "_x"),
    )(*args)


def _mix_kernel(x_ref, mod_ref, oa_ref, od_ref, pc_ref, pp_ref, pn_ref, gt_ref, bg_ref,
                pproj_ref, pb_ref, ps_ref, wa_ref, wd_ref, wp_ref, wo_ref,
                g2_ref, fg_ref_w, fu_ref_w, fd_ref_w, *rest, x_tiles, f_chunks, final):
    o_ref = rest[-2] if final else rest[-1]
    t = pl.program_id(1)
    x = x_ref[0]
    gate = mod_ref[0][5:6]
    tm, d = x.shape
    a = lax.dot_general(oa_ref[0, 0], wa_ref[...], TN_DIMS, preferred_element_type=F32)
    dd = lax.dot_general(od_ref[0, 0], wd_ref[...], TN_DIMS, preferred_element_type=F32)

    is_ctx = t == x_tiles
    has_prev = jnp.logical_and(t != 0, jnp.logical_not(is_ctx))
    has_next = jnp.logical_and(t != x_tiles - 1, jnp.logical_not(is_ctx))
    cur = pc_ref[0]
    prev = jnp.where(has_prev, pp_ref[0], 0.0)
    nxt = jnp.where(has_next, pn_ref[0], 0.0)
    ext = jnp.concatenate([prev, cur, nxt], axis=0)
    rows = ext.shape[0]
    seq_len = jnp.where(is_ctx, tm, x_tiles * tm)
    pos = jnp.where(is_ctx, 0, t * tm) + lax.broadcasted_iota(jnp.int32, (tm, 1), 0)
    outs = []
    for g, w in enumerate(POOL_WINDOWS):
        lanes = slice(g * POOL_G, (g + 1) * POOL_G)
        run = ext[:, lanes]
        span = 1
        while span < w:
            run = run + pltpu.roll(run, rows - span, axis=0)
            span *= 2
        win = pltpu.roll(run, rows - (POOL_HALO - w // 2), axis=0)[0:tm]
        cnt = (jnp.minimum(pos + w // 2, seq_len) - jnp.maximum(pos - w // 2, 0)).astype(F32)
        pooled = (win / cnt - cur[:, lanes]).astype(BF16)
        outs.append((_dot(pooled, pproj_ref[g]) + pb_ref[:, lanes]) * ps_ref[:, lanes])
    pooled_out = _dot(jnp.concatenate(outs, axis=1).astype(BF16), wp_ref[...])

    gs = jax.nn.sigmoid(gt_ref[0].astype(F32) + bg_ref[...])
    merged = gs[:, 0:d] * a + gs[:, d:2 * d] * dd + gs[:, 2 * d:3 * d] * pooled_out
    mixed = x + gate * _dot(merged.astype(BF16), wo_ref[...])
    new = _swiglu_half_step(mixed, mod_ref[0], g2_ref, fg_ref_w, fu_ref_w, fd_ref_w, 2, f_chunks)
    o_ref[0] = new
    if final:
        final_g_ref, final_ref = rest[0], rest[-1]

        @pl.when(t < x_tiles)
        def _():
            final_ref[0] = _rms(new) * final_g_ref[...]


def _mix(xs, mod, oa, od, pool_in, gates, w, gains, ffb, *, l, mod_row, x_tiles, final_g=None):
    b, nt, d = xs.shape
    f = ffb[0].shape[2]
    f_chunks = tuple((lo, min(lo + 1024, f)) for lo in range(0, f, 1024))
    hb = TM // POOL_HALO
    n_halo = nt // POOL_HALO
    tile = pl.BlockSpec((1, TM, d), lambda i, t: (i, t, 0))
    pw = 4 * POOL_G
    weights = [w["bgate"], w["pproj"], w["pb"], w["ps"], w["wa"], w["wd"], w["wp"], w["wo"]]

    def o_spec(o):
        r = o.shape[3] // TM
        return pl.BlockSpec((1, 1, o.shape[2], TM), lambda i, t: (i, t // r, 0, t % r))

    in_specs = ([tile, _mod_spec(mod, l, mod_row),
                 o_spec(oa), o_spec(od),
                 pl.BlockSpec((1, TM, pw), lambda i, t: (i, t, 0)),
                 pl.BlockSpec((1, POOL_HALO, pw), lambda i, t: (i, jnp.maximum(t * hb - 1, 0), 0)),
                 pl.BlockSpec((1, POOL_HALO, pw), lambda i, t: (i, jnp.minimum((t + 1) * hb, n_halo - 1), 0)),
                 pl.BlockSpec((1, TM, 3 * d), lambda i, t: (i, t, 0))]
                + [_layer_spec(a, l) for a in weights]
                + [_layer_spec(gains, l, 2)] + [_layer_spec(a, l) for a in ffb])
    args = [xs, mod, oa, od, pool_in, pool_in, pool_in, gates, *weights, gains, *ffb]
    out_specs, out_shape = tile, jax.ShapeDtypeStruct(xs.shape, F32)
    if final_g is not None:
        in_specs.append(_const_spec(final_g.shape))
        args.append(final_g)
        out_specs = [tile, pl.BlockSpec((1, TM, d), lambda i, t: (i, jnp.minimum(t, x_tiles - 1), 0))]
        out_shape = [out_shape, jax.ShapeDtypeStruct((b, x_tiles * TM, d), F32)]
    return pl.pallas_call(
        functools.partial(_mix_kernel, x_tiles=x_tiles, f_chunks=f_chunks, final=final_g is not None),
        grid=(b, nt // TM),
        in_specs=in_specs,
        out_specs=out_specs,
        out_shape=out_shape,
        input_output_aliases={0: 0},
        compiler_params=_params(),
        name="mix_ffn2",
    )(*args)


def _rope_tables(seq, ctx):
    rows = seq // GRID_W
    row_ids = jnp.repeat(jnp.arange(rows), GRID_W).astype(F32)
    col_ids = jnp.tile(jnp.arange(GRID_W), rows).astype(F32)
    n_freq = ROPE_HALF // 2
    inv_freq = ROPE_THETA ** (-jnp.arange(n_freq, dtype=F32) / n_freq)
    ang = jnp.concatenate([row_ids[:, None] * inv_freq, col_ids[:, None] * inv_freq], axis=-1)
    cos = jnp.concatenate([jnp.cos(ang), jnp.ones((ctx, ROPE_HALF), F32)], axis=0)
    sin = jnp.concatenate([jnp.sin(ang), jnp.zeros((ctx, ROPE_HALF), F32)], axis=0)
    pad = jnp.zeros((seq + ctx, LANES - 2 * ROPE_HALF), F32)
    return {
        "cosT": jnp.tile(cos.T, (HEADS, 1)), "sinT": jnp.tile(sin.T, (HEADS, 1)),
        "cosN": jnp.tile(cos, (1, HEADS)), "sinN": jnp.tile(sin, (1, HEADS)),
        "c128": jnp.concatenate([cos, cos, pad], axis=1),
        "s128": jnp.concatenate([sin, sin, pad], axis=1),
    }


def _placement_matrices():
    hw = ROPE_HALF
    perm = np.zeros((HEADS * 2 * DIFF_HD, HEADS * 2 * DIFF_HD), np.float32)
    for h in range(HEADS):
        for c in range(2):
            for half in range(2):
                for f in range(hw):
                    perm[(c * 2 + half) * LANES + h * hw + f, h * 2 * DIFF_HD + c * DIFF_HD + half * hw + f] = 1.0
    place = np.zeros((LANES, HEADS * LANES), np.float32)
    for h in range(HEADS):
        for f in range(MLA_ROPE):
            place[f, h * LANES + MLA_NOPE + f] = 1.0
    return jnp.asarray(place, BF16), jnp.asarray(perm, BF16)


def _stacked_weights(p):
    w_in = p["w_in"]
    nl, d, _ = w_in.shape
    hw = ROPE_HALF
    q_lora, kv_lora = p["mla_w_uq"].shape[1], p["mla_w_ukv"].shape[1]
    diff_w = HEADS * 2 * DIFF_HD
    widths = (q_lora, kv_lora, MLA_ROPE, diff_w, diff_w, HEADS * DIFF_V, 4 * POOL_G)
    o_cq, o_ckv, o_kr, o_dq, o_dk, o_dv, o_pool, o_gate = (int(v) for v in np.cumsum((0,) + widths))
    bf = lambda a: a.astype(BF16)
    t = lambda a: jnp.swapaxes(a, 1, 2)

    def block_order(cols):
        a = cols.reshape(nl, d, HEADS, 2, 2, hw)
        return a.transpose(0, 1, 3, 4, 2, 5).reshape(nl, d, HEADS * 2 * DIFF_HD)

    wkr = w_in[:, :, o_kr:o_kr + MLA_ROPE]
    wkr_rot = jnp.concatenate([-wkr[:, :, hw:], wkr[:, :, :hw]], axis=2)
    zpad = jnp.zeros((nl, d, LANES - MLA_ROPE), F32)
    wkr2 = jnp.concatenate([wkr, zpad, wkr_rot, zpad], axis=2)

    w_uq = p["mla_w_uq"].reshape(nl, q_lora, HEADS, MLA_QK)
    wuq = jnp.concatenate([w_uq[..., :MLA_NOPE].reshape(nl, q_lora, -1),
                           w_uq[..., MLA_NOPE:MLA_NOPE + hw].reshape(nl, q_lora, -1),
                           w_uq[..., MLA_NOPE + hw:].reshape(nl, q_lora, -1)], axis=2)
    w_ukv = p["mla_w_ukv"].reshape(nl, kv_lora, HEADS, MLA_NOPE + MLA_V)
    is_key = (jnp.arange(MLA_NOPE + MLA_V) < MLA_NOPE).astype(F32)
    wkpad = (w_ukv * is_key).reshape(nl, kv_lora, -1)
    wv = w_ukv[..., MLA_NOPE:].reshape(nl, kv_lora, -1)
    place, perm = _placement_matrices()
    return {
        "wcq": bf(w_in[:, :, o_cq:o_ckv]), "wckv": bf(w_in[:, :, o_ckv:o_kr]), "wkr2": bf(wkr2),
        "wdqT": bf(t(block_order(w_in[:, :, o_dq:o_dk]))), "wdk": bf(block_order(w_in[:, :, o_dk:o_dv])),
        "wdvT": bf(t(w_in[:, :, o_dv:o_pool])), "wpool": bf(w_in[:, :, o_pool:o_gate]), "wgates": bf(w_in[:, :, o_gate:]),
        "qng": p["mla_q_norm_g"][:, None], "kvng": p["mla_kv_norm_g"][:, None],
        "wuqT": bf(t(wuq)), "wkpad": bf(wkpad), "wvT": bf(t(wv)),
        "place": place, "perm": perm,
        "bgate": p["b_gate"].reshape(nl, 1, -1), "pproj": bf(p["pool_proj"]),
        "pb": p["pool_b"].reshape(nl, 1, -1), "ps": p["pool_scale"][:, None],
        "wa": bf(p["w_br_mla"]), "wd": bf(p["w_br_diff"]), "wp": bf(p["w_br_pool"]), "wo": bf(p["w_out"]),
    }


def kernel(x, c, ctx, c_ctx, ada_w, ada_b, norm_g, ffa_w_gate, ffa_w_up, ffa_w_down, ffb_w_gate, ffb_w_up, ffb_w_down, w_in, b_gate, mla_q_norm_g, mla_kv_norm_g, mla_w_uq, mla_w_ukv, diff_lambda, diff_subln_g, pool_proj, pool_b, pool_scale, w_br_mla, w_br_diff, w_br_pool, w_out, final_g):
    b, seq, d = x.shape
    n_ctx = ctx.shape[1]
    depth = ada_w.shape[0]
    nt = seq + n_ctx
    nsub = ATT_NSUB if seq % (ATT_NSUB * TQ_MLA) == 0 else 1
    assert n_ctx == TM and seq % (nsub * TQ_MLA) == 0 and seq % GRID_W == 0 and b + 1 <= MOD_ROWS
    x_tiles = seq // TM
    p = dict(w_in=w_in, b_gate=b_gate, mla_q_norm_g=mla_q_norm_g, mla_kv_norm_g=mla_kv_norm_g,
             mla_w_uq=mla_w_uq, mla_w_ukv=mla_w_ukv, pool_proj=pool_proj, pool_b=pool_b,
             pool_scale=pool_scale, w_br_mla=w_br_mla, w_br_diff=w_br_diff, w_br_pool=w_br_pool, w_out=w_out)

    c_rows = jnp.concatenate([c, c_ctx[None], jnp.zeros((MOD_ROWS - b - 1, d), F32)], axis=0)
    mod = _modulation(c_rows, ada_w, ada_b).reshape(depth, MOD_ROWS, N_ADA, d)
    mod_row = lambda i, t: jnp.where(t == x_tiles, b, i)

    tables = _rope_tables(seq, n_ctx)
    w = _stacked_weights(p)
    gains = norm_g[:, :, None, :]
    bf = lambda a: a.astype(BF16)
    ffa = (bf(ffa_w_gate), bf(ffa_w_up), bf(ffa_w_down))
    ffb = (bf(ffb_w_gate), bf(ffb_w_up), bf(ffb_w_down))
    extra = (diff_lambda, diff_subln_g[:, :, None])
    chunks = _key_chunks(nt)

    xs = x
    for l in range(depth):
        lam_init = 0.8 - 0.6 * math.exp(-0.3 * l)
        xs = _ffn(xs, mod, gains, *ffa, l=l, mi=0, mod_row=mod_row, ctx=ctx if l == 0 else None)
        q_mla, k_mla, v_mla, q_diff, k_diff, v_diff, pool_in, gates, kn_mla, kn_diff = _inproj(
            xs, mod, gains, w, tables, l=l, mod_row=mod_row)
        mla = (q_mla, k_mla, v_mla, kn_mla)
        dif = (q_diff, k_diff, v_diff, kn_diff)
        full = dict(nsub=nsub, q_block0=0, k_rows=nt, k_block0=0, chunks=chunks)
        ctx_only = dict(tq=TM, nsub=1, steps=1, k_rows=TM, k_block0=x_tiles, chunks=((0, TM),))
        oa = _attention(*mla, tq=TQ_MLA, steps=seq // (nsub * TQ_MLA), diff=False, **full)
        oa = _attention(*mla, q_block0=seq // TQ_MLA, diff=False, prev_out=oa, **ctx_only)
        od = _attention(*dif, tq=TQ_DIFF, steps=seq // (nsub * TQ_DIFF), diff=True,
                        extra=extra, layer=l, lam_init=lam_init, **full)
        od = _attention(*dif, q_block0=seq // TQ_DIFF, diff=True, extra=extra, layer=l,
                        lam_init=lam_init, prev_out=od, **ctx_only)
        last = l + 1 == depth
        xs = _mix(xs, mod, oa, od, pool_in, gates, w, gains, ffb, l=l, mod_row=mod_row, x_tiles=x_tiles,
                  final_g=final_g[None] if last else None)
    return xs[1]
```

```python
import functools
import math

import numpy as np
import jax
import jax.numpy as jnp
from jax import lax
from jax.experimental import pallas as pl
from jax.experimental.pallas import tpu as pltpu

F32 = jnp.float32
BF16 = jnp.bfloat16

EPS = 1e-6
ROPE_THETA = 10000.0
GRID_W = 64
N_ADA = 9

HEADS = 8
MLA_NOPE = 64
MLA_ROPE = 32
MLA_V = 64
V_ROWS = MLA_V + 16
MLA_QK = MLA_NOPE + MLA_ROPE
LOG2E = math.log2(math.e)
MLA_SCALE = MLA_QK ** -0.5 * LOG2E
DIFF_HD = 32
DIFF_V = 64
DIFF_SCALE = DIFF_HD ** -0.5 * LOG2E
ROPE_HALF = 16
POOL_WINDOWS = (2, 4, 8, 16)
POOL_G = 128
POOL_HALO = 16

LANES = 128
MOD_ROWS = 8
MOD_COL_BLOCKS = 8
TM = 256
TK = 512
CHUNK_HEAD = (256,)
CHUNK_TAIL = ()
ATT_AHEAD = 2
ATT_ROWS = 256
ATT_BOUND_MAX = 40.0
ATT_NSUB = 4
TQ_MLA = 1024
TQ_DIFF = 512
VMEM_LIMIT = 52 * 1024 * 1024

ATT_FLAGS = {}

NT_DIMS = (((1,), (1,)), ((), ()))
TN_DIMS = (((0,), (0,)), ((), ()))


def _params():
    return pltpu.CompilerParams(vmem_limit_bytes=VMEM_LIMIT)


def _const_spec(shape):
    zeros = (0,) * len(shape)
    return pl.BlockSpec(shape, lambda *_: zeros, pipeline_mode=pl.Buffered(1))


def _layer_spec(a, *lead):
    tail = a.shape[len(lead):]
    zeros = (0,) * len(tail)
    return pl.BlockSpec((None,) * len(lead) + tail, lambda *_: tuple(lead) + zeros, pipeline_mode=pl.Buffered(1))


def _mod_spec(mod, l, mod_row):
    return pl.BlockSpec((None, 1) + mod.shape[2:], lambda i, t: (l, mod_row(i, t), 0, 0))


def _dot(a, b):
    return jnp.dot(a, b, preferred_element_type=F32)


def _rms(x):
    return x * lax.rsqrt(jnp.mean(x * x, axis=-1, keepdims=True) + EPS)


def _mod_kernel(c_ref, w_ref, b_ref, o_ref):
    c = c_ref[...]
    a = c * jax.nn.sigmoid(c)
    o_ref[0] = jnp.dot(a, w_ref[0], preferred_element_type=F32,
                       precision=lax.Precision.HIGHEST) + b_ref[0]


def _modulation(c_rows, ada_w, ada_b):
    depth, d, n = ada_w.shape
    bn = n // MOD_COL_BLOCKS
    return pl.pallas_call(
        _mod_kernel,
        grid=(depth, MOD_COL_BLOCKS),
        in_specs=[pl.BlockSpec((MOD_ROWS, d), lambda l, j: (0, 0)),
                  pl.BlockSpec((1, d, bn), lambda l, j: (l, 0, j)),
                  pl.BlockSpec((1, 1, bn), lambda l, j: (l, 0, j))],
        out_specs=pl.BlockSpec((1, MOD_ROWS, bn), lambda l, j: (l, 0, j)),
        out_shape=jax.ShapeDtypeStruct((depth, MOD_ROWS, n), F32),
        compiler_params=_params(),
        name="modulation",
    )(c_rows, ada_w, ada_b.reshape(depth, 1, n))


def _swiglu_half_step(x, m, g_ref, wg_ref, wu_ref, wd_ref, mi, f_chunks):
    shift, scale, gate = m[3 * mi:3 * mi + 1], m[3 * mi + 1:3 * mi + 2], m[3 * mi + 2:3 * mi + 3]
    u = (_rms(x) * g_ref[...]) * (1.0 + scale) + shift
    ub = u.astype(BF16)
    y = jnp.zeros(x.shape, F32)
    for lo, hi in f_chunks:
        a = _dot(ub, wg_ref[:, lo:hi])
        b = _dot(ub, wu_ref[:, lo:hi])
        hid = (a * jax.nn.sigmoid(a) * b).astype(BF16)
        y = y + _dot(hid, wd_ref[lo:hi, :])
    return x + (0.5 * gate) * y


def _inproj_kernel(x_ref, mod_ref, g_ref, wcq, wckv, wkr2, wdqT, wdk, wdvT, wpool, wgates,
                   qng, kvng, wuqT, wkpad, wvT, place, perm,
                   cosT, sinT, cosN, sinN, c128, s128,
                   g0_ref, fg_ref_w, fu_ref_w, fd_ref_w, *rest, ctx_tile, f_chunks):
    (oq_mla, ok_mla, ov_mla, oq_diff, ok_diff, ov_diff, opool, ogates, okn_mla, okn_diff, oxs) = rest[-11:]
    x = x_ref[0]
    if ctx_tile is not None:
        x = jnp.where(pl.program_id(1) == ctx_tile, rest[0][0], x)
    m = mod_ref[0]
    x = _swiglu_half_step(x, m, g0_ref, fg_ref_w, fu_ref_w, fd_ref_w, 0, f_chunks)
    oxs[0] = x
    u = (_rms(x) * g_ref[...]) * (1.0 + m[4:5]) + m[3:4]
    ub = u.astype(BF16)
    tm = x.shape[0]
    cT, sT = cosT[...], sinT[...]
    hw = ROPE_HALF

    cqn = (_rms(_dot(ub, wcq[...])) * qng[...]).astype(BF16)
    qT = lax.dot_general(wuqT[...], cqn, NT_DIMS, preferred_element_type=F32) * MLA_SCALE
    nope_w = HEADS * MLA_NOPE
    x1, x2 = qT[nope_w:nope_w + LANES], qT[nope_w + LANES:nope_w + 2 * LANES]
    qn = qT[0:nope_w].astype(BF16)
    r1 = (x1 * cT - x2 * sT).astype(BF16)
    r2 = (x1 * sT + x2 * cT).astype(BF16)
    zpad = jnp.zeros((LANES - MLA_QK, tm), BF16)
    for h in range(HEADS):
        oq_mla[0, h, 0, 0:MLA_NOPE, :] = qn[h * MLA_NOPE:(h + 1) * MLA_NOPE]
        oq_mla[0, h, 0, MLA_NOPE:MLA_NOPE + hw, :] = r1[h * hw:(h + 1) * hw]
        oq_mla[0, h, 0, MLA_NOPE + hw:MLA_QK, :] = r2[h * hw:(h + 1) * hw]
        oq_mla[0, h, 0, MLA_QK:LANES, :] = zpad

    ckvn = (_rms(_dot(ub, wckv[...])) * kvng[...]).astype(BF16)
    kr2 = _dot(ub, wkr2[...])
    krr = (kr2[:, 0:LANES] * c128[...] + kr2[:, LANES:2 * LANES] * s128[...]).astype(BF16)
    kall = (_dot(ckvn, wkpad[...]) + _dot(krr, place[...])).astype(BF16)
    vT = lax.dot_general(wvT[...], ckvn, NT_DIMS, preferred_element_type=F32).astype(BF16)
    head_row = lax.broadcasted_iota(jnp.int32, (HEADS, LANES), 0)
    left_half = lax.broadcasted_iota(jnp.int32, (tm, LANES), 1) < 2 * DIFF_HD

    def max_sq_norm(sq):
        return jnp.max(jnp.sum(sq, axis=1, keepdims=True), axis=0, keepdims=True)

    kn = jnp.zeros((HEADS, LANES), F32)
    for h in range(HEADS):
        kh = kall[:, h * LANES:(h + 1) * LANES]
        ok_mla[0, h] = kh
        khf = kh.astype(F32)
        kn = jnp.where(head_row == h, max_sq_norm(khf * khf), kn)
    okn_mla[0, 0] = kn
    ones_rows = (lax.broadcasted_iota(jnp.int32, (V_ROWS - MLA_V, tm), 0) == 0).astype(BF16)

    def store_values(ov, vals):
        for h in range(HEADS):
            ov[0, h * V_ROWS:h * V_ROWS + MLA_V, :] = vals[h * MLA_V:(h + 1) * MLA_V]
            ov[0, h * V_ROWS + MLA_V:(h + 1) * V_ROWS, :] = ones_rows

    store_values(ov_mla, vT)

    dqT = lax.dot_general(wdqT[...], ub, NT_DIMS, preferred_element_type=F32) * DIFF_SCALE
    a1, a2, b1, b2 = (dqT[i * LANES:(i + 1) * LANES] for i in range(4))
    parts = [(a1 * cT - a2 * sT).astype(BF16), (a1 * sT + a2 * cT).astype(BF16),
             (b1 * cT - b2 * sT).astype(BF16), (b1 * sT + b2 * cT).astype(BF16)]
    zhalf = jnp.zeros((2 * DIFF_HD, tm), BF16)
    for h in range(HEADS):
        base = (h % 2) * 2 * DIFF_HD
        for i, part in enumerate(parts):
            oq_diff[0, h, 0, base + i * hw:base + (i + 1) * hw, :] = part[h * hw:(h + 1) * hw]
        other = 2 * DIFF_HD - base
        oq_diff[0, h, 0, other:other + 2 * DIFF_HD, :] = zhalf

    dk = _dot(ub, wdk[...])
    cN, sN = cosN[...], sinN[...]
    k1a, k1b, k2a, k2b = (dk[:, i * LANES:(i + 1) * LANES] for i in range(4))
    rk = jnp.concatenate([k1a * cN - k1b * sN, k1a * sN + k1b * cN,
                          k2a * cN - k2b * sN, k2a * sN + k2b * cN], axis=1).astype(BF16)
    kd = _dot(rk, perm[...]).astype(BF16)
    kn = jnp.zeros((HEADS, LANES), F32)
    for p in range(HEADS // 2):
        kp = kd[:, p * LANES:(p + 1) * LANES]
        ok_diff[0, p] = kp
        kpf = kp.astype(F32)
        sq = kpf * kpf
        kn = jnp.where(head_row == 2 * p, max_sq_norm(jnp.where(left_half, sq, 0.0)), kn)
        kn = jnp.where(head_row == 2 * p + 1, max_sq_norm(jnp.where(left_half, 0.0, sq)), kn)
    okn_diff[0, 0] = kn
    store_values(ov_diff, lax.dot_general(wdvT[...], ub, NT_DIMS, preferred_element_type=F32).astype(BF16))

    opool[0] = _dot(ub, wpool[...])
    ogates[0] = _dot(ub, wgates[...]).astype(BF16)


def _inproj(xs, mod, gains, w, tables, ffa, *, l, mod_row, ctx=None):
    b, nt, d = xs.shape
    f = ffa[0].shape[2]
    f_chunks = tuple((lo, min(lo + 1024, f)) for lo in range(0, f, 1024))
    tile = pl.BlockSpec((1, TM, d), lambda i, t: (i, t, 0))
    x_spec, extra_in, extra_args, aliases, ctx_tile = tile, [], [], {0: 10}, None
    if ctx is not None:
        ctx_tile = nt // TM
        nt += ctx.shape[1]
        x_spec = pl.BlockSpec((1, TM, d), lambda i, t: (i, jnp.minimum(t, ctx_tile - 1), 0))
        extra_in, extra_args, aliases = [pl.BlockSpec((1, TM, d), lambda i, t: (i, 0, 0))], [ctx], {}
    tabT = pl.BlockSpec((LANES, TM), lambda i, t: (0, t))
    tabN = pl.BlockSpec((TM, LANES), lambda i, t: (t, 0))
    stacked = [w["wcq"], w["wckv"], w["wkr2"], w["wdqT"], w["wdk"], w["wdvT"], w["wpool"], w["wgates"],
               w["qng"], w["kvng"], w["wuqT"], w["wkpad"], w["wvT"]]
    shared = [w["place"], w["perm"]]
    vT_spec = pl.BlockSpec((1, HEADS * V_ROWS, TM), lambda i, t: (i, 0, t))

    def q_spec(tq):
        r = tq // TM
        return pl.BlockSpec((1, HEADS, 1, LANES, TM), lambda i, t: (i, 0, t // r, 0, t % r))

    out_shapes = [
        jax.ShapeDtypeStruct((b, HEADS, pl.cdiv(nt, TQ_MLA), LANES, TQ_MLA), BF16),
        jax.ShapeDtypeStruct((b, HEADS, nt, LANES), BF16),
        jax.ShapeDtypeStruct((b, HEADS * V_ROWS, nt), BF16),
        jax.ShapeDtypeStruct((b, HEADS, pl.cdiv(nt, TQ_DIFF), LANES, TQ_DIFF), BF16),
        jax.ShapeDtypeStruct((b, HEADS // 2, nt, LANES), BF16),
        jax.ShapeDtypeStruct((b, HEADS * V_ROWS, nt), BF16),
        jax.ShapeDtypeStruct((b, nt, 4 * POOL_G), F32),
        jax.ShapeDtypeStruct((b, nt, 3 * d), BF16),
        jax.ShapeDtypeStruct((b, nt // TM, HEADS, LANES), F32),
        jax.ShapeDtypeStruct((b, nt // TM, HEADS, LANES), F32),
    ]
    kn_spec = pl.BlockSpec((1, 1, HEADS, LANES), lambda i, t: (i, t, 0, 0))
    out_specs = [
        q_spec(TQ_MLA),
        pl.BlockSpec((1, HEADS, TM, LANES), lambda i, t: (i, 0, t, 0)),
        vT_spec,
        q_spec(TQ_DIFF),
        pl.BlockSpec((1, HEADS // 2, TM, LANES), lambda i, t: (i, 0, t, 0)),
        vT_spec,
        pl.BlockSpec((1, TM, 4 * POOL_G), lambda i, t: (i, t, 0)),
        pl.BlockSpec((1, TM, 3 * d), lambda i, t: (i, t, 0)),
        kn_spec, kn_spec,
    ]
    return pl.pallas_call(
        functools.partial(_inproj_kernel, ctx_tile=ctx_tile, f_chunks=f_chunks),
        grid=(b, nt // TM),
        in_specs=([x_spec, _mod_spec(mod, l, mod_row), _layer_spec(gains, l, 1)]
                  + [_layer_spec(a, l) for a in stacked] + [_const_spec(a.shape) for a in shared]
                  + [tabT, tabT, tabN, tabN, tabN, tabN]
                  + [_layer_spec(gains, l, 0)] + [_layer_spec(a, l) for a in ffa] + extra_in),
        out_specs=out_specs + [tile],
        out_shape=out_shapes + [jax.ShapeDtypeStruct((b, nt, d), F32)],
        input_output_aliases=aliases,
        compiler_params=_params(),
        name="ffn0_inproj",
    )(xs, mod, gains, *stacked, *shared, tables["cosT"], tables["sinT"], tables["cosN"], tables["sinN"],
      tables["c128"], tables["s128"], gains, *ffa, *extra_args)


def _attn_kernel(*refs, chunks, diff, lam_init, aliased):
    refs = list(refs)
    q_ref, k_ref, v_ref, kn_ref = refs[:4]
    n_buf = ATT_AHEAD + 1
    s_bufs = refs[-n_buf:]
    o_ref = refs[-n_buf - 1]
    assert len(refs) == 5 + 2 * diff + aliased + n_buf
    nsub, tq = q_ref.shape[2], q_ref.shape[4]
    nc = len(chunks)
    head = pl.program_id(1)

    tile_max = jnp.max(kn_ref[0], axis=0)
    rows = lax.broadcasted_iota(jnp.int32, tile_max.shape, 0)
    kmax2 = jnp.max(jnp.where(rows == head, tile_max, 0.0), axis=0, keepdims=True)[:, 0:1]

    def finish(num, den, sub):
        o = num / den
        if diff:
            dl_ref, sg_ref = refs[4:6]
            dl = dl_ref[...]
            lam = (jnp.exp(jnp.sum(dl[0:1] * dl[1:2], axis=1, keepdims=True))
                   - jnp.exp(jnp.sum(dl[2:3] * dl[3:4], axis=1, keepdims=True)) + lam_init)
            o = o[:, :tq] - lam * o[:, tq:]
            o = o * lax.rsqrt(jnp.mean(o * o, axis=0, keepdims=True) + EPS)
            o = o * sg_ref[...] * (1.0 - lam_init)
        o_ref[0, sub] = o.astype(o_ref.dtype)

    def bounded_shift(rhs, bound, sub):
        acc = jnp.zeros((MLA_V, rhs.shape[1]), F32)
        den = jnp.zeros((1, rhs.shape[1]), F32)
        pieces = lambda c: [(lo, min(lo + ATT_ROWS, chunks[c][1])) for lo in range(*chunks[c], ATT_ROWS)]
        logits = lambda lo, hi: _dot(k_ref[0, 0, lo:hi, :], rhs)
        s_next = [logits(lo, hi) for lo, hi in pieces(0)]
        for c in range(nc):
            s_cur, s_next = s_next, []
            nxt = pieces(c + 1) if c + 1 < nc else []
            for i, (lo, hi) in enumerate(pieces(c)):
                if i < len(nxt):
                    s_next.append(logits(*nxt[i]))
                p = jnp.exp2(s_cur[i] - bound)
                den = den + jnp.sum(p, axis=0, keepdims=True)
                acc = acc + _dot(v_ref[0, 0:MLA_V, lo:hi], p.astype(BF16))
            s_next += [logits(lo, hi) for lo, hi in nxt[len(pieces(c)):]]
        finish(acc, den, sub)

    def running_max(rhs, sub, zero_row):
        n = rhs.shape[1]

        def scores(c):
            lo, hi = chunks[c]
            s = _dot(k_ref[0, 0, lo:hi, :], rhs)
            s_bufs[c % n_buf][pl.ds(zero_row, hi - lo), :] = s
            return jnp.max(s, axis=0, keepdims=True)

        m = jnp.full((1, n), -1e30, F32)
        acc = jnp.zeros((V_ROWS, n), F32)
        cmaxes = {c: scores(c) for c in range(min(ATT_AHEAD, nc))}
        for c in range(nc):
            lo, hi = chunks[c]
            m_new = jnp.maximum(m, cmaxes.pop(c))
            alpha = jnp.exp2(m - m_new)
            if c + ATT_AHEAD < nc:
                cmaxes[c + ATT_AHEAD] = scores(c + ATT_AHEAD)
            p = jnp.exp2(s_bufs[c % n_buf][pl.ds(zero_row, hi - lo), :] - m_new)
            m, acc = m_new, alpha * acc + _dot(v_ref[0, :, lo:hi], p.astype(BF16))
        finish(acc[0:MLA_V], acc[MLA_V:MLA_V + 1], sub)

    def queries(sub):
        q = q_ref[0, 0, sub]
        if diff:
            first = (lax.broadcasted_iota(jnp.int32, q.shape, 0) & DIFF_HD) == 0
            zero = jnp.zeros_like(q)
            rhs = jnp.concatenate([jnp.where(first, q, zero), jnp.where(first, zero, q)], axis=1)
        else:
            rhs = q
        qf = rhs.astype(F32)
        return rhs, jnp.sqrt(jnp.sum(qf * qf, axis=0, keepdims=True) * kmax2)

    def for_each_block(fn):
        if nsub == 1:
            fn(0)
        else:
            lax.fori_loop(0, nsub, lambda sub, carry: (fn(sub), carry)[1], 0)

    qa = q_ref[0, 0].astype(F32)
    q_norm2 = jnp.max(jnp.sum(qa * qa, axis=1, keepdims=True))
    small = q_norm2 * jnp.max(kmax2) <= ATT_BOUND_MAX * ATT_BOUND_MAX

    @pl.when(small)
    def _():
        for_each_block(lambda sub: bounded_shift(*queries(sub), sub))

    @pl.when(jnp.logical_not(small))
    def _():
        zero_row = pl.multiple_of(jnp.minimum(pl.program_id(2), 0), 16)
        for_each_block(lambda sub: running_max(queries(sub)[0], sub, zero_row))


def _key_chunks(nk):
    head = [c for c in CHUNK_HEAD if sum(CHUNK_HEAD) + sum(CHUNK_TAIL) + TK <= nk]
    tail = CHUNK_TAIL if head else ()
    sizes = list(head)
    body = nk - sum(head) - sum(tail)
    sizes += [TK] * (body // TK) + ([body % TK] if body % TK else [])
    sizes += list(tail)
    edges = np.cumsum([0] + sizes)
    assert edges[-1] == nk and all(s % LANES == 0 for s in sizes)
    return tuple((int(a), int(b)) for a, b in zip(edges[:-1], edges[1:]))


def _attention(qT, k, vT, kn, *, tq, nsub, steps, q_block0, k_rows, k_block0, chunks,
               diff, extra=(), layer=0, lam_init=0.0, prev_out=None):
    b, heads, nqb, _, tqw = qT.shape
    dv = MLA_V
    kdiv = 2 if diff else 1
    in_specs = [
        pl.BlockSpec((1, 1, nsub, LANES, tq), lambda i, h, j: (i, h, q_block0 + j, 0, 0)),
        pl.BlockSpec((1, 1, k_rows, LANES), lambda i, h, j: (i, h // kdiv, k_block0, 0)),
        pl.BlockSpec((1, V_ROWS, k_rows), lambda i, h, j: (i, h, k_block0)),
        pl.BlockSpec((1,) + kn.shape[1:], lambda i, h, j: (i, 0, 0, 0)),
    ]
    args = [qT, k, vT, kn]
    if diff:
        in_specs += [_layer_spec(a, layer) for a in extra]
        args += list(extra)
    aliases = {}
    if prev_out is not None:
        in_specs.append(pl.BlockSpec(memory_space=pl.ANY))
        args.append(prev_out)
        aliases = {len(args) - 1: 0}
    return pl.pallas_call(
        functools.partial(_attn_kernel, chunks=chunks, diff=diff, lam_init=lam_init,
                          aliased=prev_out is not None),
        grid=(b, heads, steps),
        in_specs=in_specs,
        out_specs=pl.BlockSpec((1, nsub, dv, tq), lambda i, h, j: (i, q_block0 + j, h, 0)),
        out_shape=jax.ShapeDtypeStruct((b, nqb, heads * dv, tqw), BF16),
        scratch_shapes=[pltpu.VMEM((max(hi - lo for lo, hi in chunks), (2 * tq if diff else tq)), dt)
                        for dt in [F32] * (ATT_AHEAD + 1)],
        input_output_aliases=aliases,
        compiler_params=pltpu.CompilerParams(vmem_limit_bytes=VMEM_LIMIT, flags=ATT_FLAGS),
        name=("diff" if diff else "mla") + ("_ctx" if prev_out is not None else "_x"),
    )(*args)


def _mix_kernel(x_ref, mod_ref, oa_ref, od_ref, pc_ref, pp_ref, pn_ref, gt_ref, bg_ref,
                pproj_ref, pb_ref, ps_ref, wa_ref, wd_ref, wp_ref, wo_ref,
                g2_ref, fg_ref_w, fu_ref_w, fd_ref_w, *rest, x_tiles, f_chunks, final):
    o_ref = rest[-2] if final else rest[-1]
    t = pl.program_id(1)
    x = x_ref[0]
    gate = mod_ref[0][5:6]
    tm, d = x.shape
    a = lax.dot_general(oa_ref[0, 0], wa_ref[...], TN_DIMS, preferred_element_type=F32)
    dd = lax.dot_general(od_ref[0, 0], wd_ref[...], TN_DIMS, preferred_element_type=F32)

    is_ctx = t == x_tiles
    has_prev = jnp.logical_and(t != 0, jnp.logical_not(is_ctx))
    has_next = jnp.logical_and(t != x_tiles - 1, jnp.logical_not(is_ctx))
    cur = pc_ref[0]
    prev = jnp.where(has_prev, pp_ref[0], 0.0)
    nxt = jnp.where(has_next, pn_ref[0], 0.0)
    ext = jnp.concatenate([prev, cur, nxt], axis=0)
    rows = ext.shape[0]
    seq_len = jnp.where(is_ctx, tm, x_tiles * tm)
    pos = jnp.where(is_ctx, 0, t * tm) + lax.broadcasted_iota(jnp.int32, (tm, 1), 0)
    outs = []
    for g, w in enumerate(POOL_WINDOWS):
        lanes = slice(g * POOL_G, (g + 1) * POOL_G)
        run = ext[:, lanes]
        span = 1
        while span < w:
            run = run + pltpu.roll(run, rows - span, axis=0)
            span *= 2
        win = pltpu.roll(run, rows - (POOL_HALO - w // 2), axis=0)[0:tm]
        cnt = (jnp.minimum(pos + w // 2, seq_len) - jnp.maximum(pos - w // 2, 0)).astype(F32)
        pooled = (win / cnt - cur[:, lanes]).astype(BF16)
        outs.append((_dot(pooled, pproj_ref[g]) + pb_ref[:, lanes]) * ps_ref[:, lanes])
    pooled_out = _dot(jnp.concatenate(outs, axis=1).astype(BF16), wp_ref[...])

    gs = jax.nn.sigmoid(gt_ref[0].astype(F32) + bg_ref[...])
    merged = gs[:, 0:d] * a + gs[:, d:2 * d] * dd + gs[:, 2 * d:3 * d] * pooled_out
    mixed = x + gate * _dot(merged.astype(BF16), wo_ref[...])
    new = _swiglu_half_step(mixed, mod_ref[0], g2_ref, fg_ref_w, fu_ref_w, fd_ref_w, 2, f_chunks)
    o_ref[0] = new
    if final:
        final_g_ref, final_ref = rest[0], rest[-1]

        @pl.when(t < x_tiles)
        def _():
            final_ref[0] = _rms(new) * final_g_ref[...]


def _mix(xs, mod, oa, od, pool_in, gates, w, gains, ffb, *, l, mod_row, x_tiles, final_g=None):
    b, nt, d = xs.shape
    f = ffb[0].shape[2]
    f_chunks = tuple((lo, min(lo + 1024, f)) for lo in range(0, f, 1024))
    hb = TM // POOL_HALO
    n_halo = nt // POOL_HALO
    tile = pl.BlockSpec((1, TM, d), lambda i, t: (i, t, 0))
    pw = 4 * POOL_G
    weights = [w["bgate"], w["pproj"], w["pb"], w["ps"], w["wa"], w["wd"], w["wp"], w["wo"]]

    def o_spec(o):
        r = o.shape[3] // TM
        return pl.BlockSpec((1, 1, o.shape[2], TM), lambda i, t: (i, t // r, 0, t % r))

    in_specs = ([tile, _mod_spec(mod, l, mod_row),
                 o_spec(oa), o_spec(od),
                 pl.BlockSpec((1, TM, pw), lambda i, t: (i, t, 0)),
                 pl.BlockSpec((1, POOL_HALO, pw), lambda i, t: (i, jnp.maximum(t * hb - 1, 0), 0)),
                 pl.BlockSpec((1, POOL_HALO, pw), lambda i, t: (i, jnp.minimum((t + 1) * hb, n_halo - 1), 0)),
                 pl.BlockSpec((1, TM, 3 * d), lambda i, t: (i, t, 0))]
                + [_layer_spec(a, l) for a in weights]
                + [_layer_spec(gains, l, 2)] + [_layer_spec(a, l) for a in ffb])
    args = [xs, mod, oa, od, pool_in, pool_in, pool_in, gates, *weights, gains, *ffb]
    out_specs, out_shape = tile, jax.ShapeDtypeStruct(xs.shape, F32)
    if final_g is not None:
        in_specs.append(_const_spec(final_g.shape))
        args.append(final_g)
        out_specs = [tile, pl.BlockSpec((1, TM, d), lambda i, t: (i, jnp.minimum(t, x_tiles - 1), 0))]
        out_shape = [out_shape, jax.ShapeDtypeStruct((b, x_tiles * TM, d), F32)]
    return pl.pallas_call(
        functools.partial(_mix_kernel, x_tiles=x_tiles, f_chunks=f_chunks, final=final_g is not None),
        grid=(b, nt // TM),
        in_specs=in_specs,
        out_specs=out_specs,
        out_shape=out_shape,
        input_output_aliases={0: 0},
        compiler_params=_params(),
        name="mix_ffn2",
    )(*args)


def _rope_tables(seq, ctx):
    rows = seq // GRID_W
    row_ids = jnp.repeat(jnp.arange(rows), GRID_W).astype(F32)
    col_ids = jnp.tile(jnp.arange(GRID_W), rows).astype(F32)
    n_freq = ROPE_HALF // 2
    inv_freq = ROPE_THETA ** (-jnp.arange(n_freq, dtype=F32) / n_freq)
    ang = jnp.concatenate([row_ids[:, None] * inv_freq, col_ids[:, None] * inv_freq], axis=-1)
    cos = jnp.concatenate([jnp.cos(ang), jnp.ones((ctx, ROPE_HALF), F32)], axis=0)
    sin = jnp.concatenate([jnp.sin(ang), jnp.zeros((ctx, ROPE_HALF), F32)], axis=0)
    pad = jnp.zeros((seq + ctx, LANES - 2 * ROPE_HALF), F32)
    return {
        "cosT": jnp.tile(cos.T, (HEADS, 1)), "sinT": jnp.tile(sin.T, (HEADS, 1)),
        "cosN": jnp.tile(cos, (1, HEADS)), "sinN": jnp.tile(sin, (1, HEADS)),
        "c128": jnp.concatenate([cos, cos, pad], axis=1),
        "s128": jnp.concatenate([sin, sin, pad], axis=1),
    }


def _placement_matrices():
    hw = ROPE_HALF
    perm = np.zeros((HEADS * 2 * DIFF_HD, HEADS * 2 * DIFF_HD), np.float32)
    for h in range(HEADS):
        for c in range(2):
            for half in range(2):
                for f in range(hw):
                    perm[(c * 2 + half) * LANES + h * hw + f, h * 2 * DIFF_HD + c * DIFF_HD + half * hw + f] = 1.0
    place = np.zeros((LANES, HEADS * LANES), np.float32)
    for h in range(HEADS):
        for f in range(MLA_ROPE):
            place[f, h * LANES + MLA_NOPE + f] = 1.0
    return jnp.asarray(place, BF16), jnp.asarray(perm, BF16)


def _stacked_weights(p):
    w_in = p["w_in"]
    nl, d, _ = w_in.shape
    hw = ROPE_HALF
    q_lora, kv_lora = p["mla_w_uq"].shape[1], p["mla_w_ukv"].shape[1]
    diff_w = HEADS * 2 * DIFF_HD
    widths = (q_lora, kv_lora, MLA_ROPE, diff_w, diff_w, HEADS * DIFF_V, 4 * POOL_G)
    o_cq, o_ckv, o_kr, o_dq, o_dk, o_dv, o_pool, o_gate = (int(v) for v in np.cumsum((0,) + widths))
    bf = lambda a: a.astype(BF16)
    t = lambda a: jnp.swapaxes(a, 1, 2)

    def block_order(cols):
        a = cols.reshape(nl, d, HEADS, 2, 2, hw)
        return a.transpose(0, 1, 3, 4, 2, 5).reshape(nl, d, HEADS * 2 * DIFF_HD)

    wkr = w_in[:, :, o_kr:o_kr + MLA_ROPE]
    wkr_rot = jnp.concatenate([-wkr[:, :, hw:], wkr[:, :, :hw]], axis=2)
    zpad = jnp.zeros((nl, d, LANES - MLA_ROPE), F32)
    wkr2 = jnp.concatenate([wkr, zpad, wkr_rot, zpad], axis=2)

    w_uq = p["mla_w_uq"].reshape(nl, q_lora, HEADS, MLA_QK)
    wuq = jnp.concatenate([w_uq[..., :MLA_NOPE].reshape(nl, q_lora, -1),
                           w_uq[..., MLA_NOPE:MLA_NOPE + hw].reshape(nl, q_lora, -1),
                           w_uq[..., MLA_NOPE + hw:].reshape(nl, q_lora, -1)], axis=2)
    w_ukv = p["mla_w_ukv"].reshape(nl, kv_lora, HEADS, MLA_NOPE + MLA_V)
    is_key = (jnp.arange(MLA_NOPE + MLA_V) < MLA_NOPE).astype(F32)
    wkpad = (w_ukv * is_key).reshape(nl, kv_lora, -1)
    wv = w_ukv[..., MLA_NOPE:].reshape(nl, kv_lora, -1)
    place, perm = _placement_matrices()
    return {
        "wcq": bf(w_in[:, :, o_cq:o_ckv]), "wckv": bf(w_in[:, :, o_ckv:o_kr]), "wkr2": bf(wkr2),
        "wdqT": bf(t(block_order(w_in[:, :, o_dq:o_dk]))), "wdk": bf(block_order(w_in[:, :, o_dk:o_dv])),
        "wdvT": bf(t(w_in[:, :, o_dv:o_pool])), "wpool": bf(w_in[:, :, o_pool:o_gate]), "wgates": bf(w_in[:, :, o_gate:]),
        "qng": p["mla_q_norm_g"][:, None], "kvng": p["mla_kv_norm_g"][:, None],
        "wuqT": bf(t(wuq)), "wkpad": bf(wkpad), "wvT": bf(t(wv)),
        "place": place, "perm": perm,
        "bgate": p["b_gate"].reshape(nl, 1, -1), "pproj": bf(p["pool_proj"]),
        "pb": p["pool_b"].reshape(nl, 1, -1), "ps": p["pool_scale"][:, None],
        "wa": bf(p["w_br_mla"]), "wd": bf(p["w_br_diff"]), "wp": bf(p["w_br_pool"]), "wo": bf(p["w_out"]),
    }


def kernel(x, c, ctx, c_ctx, ada_w, ada_b, norm_g, ffa_w_gate, ffa_w_up, ffa_w_down, ffb_w_gate, ffb_w_up, ffb_w_down, w_in, b_gate, mla_q_norm_g, mla_kv_norm_g, mla_w_uq, mla_w_ukv, diff_lambda, diff_subln_g, pool_proj, pool_b, pool_scale, w_br_mla, w_br_diff, w_br_pool, w_out, final_g):
    b, seq, d = x.shape
    n_ctx = ctx.shape[1]
    depth = ada_w.shape[0]
    nt = seq + n_ctx
    nsub = ATT_NSUB if seq % (ATT_NSUB * TQ_MLA) == 0 else 1
    assert n_ctx == TM and seq % (nsub * TQ_MLA) == 0 and seq % GRID_W == 0 and b + 1 <= MOD_ROWS
    x_tiles = seq // TM
    p = dict(w_in=w_in, b_gate=b_gate, mla_q_norm_g=mla_q_norm_g, mla_kv_norm_g=mla_kv_norm_g,
             mla_w_uq=mla_w_uq, mla_w_ukv=mla_w_ukv, pool_proj=pool_proj, pool_b=pool_b,
             pool_scale=pool_scale, w_br_mla=w_br_mla, w_br_diff=w_br_diff, w_br_pool=w_br_pool, w_out=w_out)

    c_rows = jnp.concatenate([c, c_ctx[None], jnp.zeros((MOD_ROWS - b - 1, d), F32)], axis=0)
    mod = _modulation(c_rows, ada_w, ada_b).reshape(depth, MOD_ROWS, N_ADA, d)
    mod_row = lambda i, t: jnp.where(t == x_tiles, b, i)

    tables = _rope_tables(seq, n_ctx)
    w = _stacked_weights(p)
    gains = norm_g[:, :, None, :]
    bf = lambda a: a.astype(BF16)
    ffa = (bf(ffa_w_gate), bf(ffa_w_up), bf(ffa_w_down))
    ffb = (bf(ffb_w_gate), bf(ffb_w_up), bf(ffb_w_down))
    extra = (diff_lambda, diff_subln_g[:, :, None])
    chunks = _key_chunks(nt)

    xs = x
    for l in range(depth):
        lam_init = 0.8 - 0.6 * math.exp(-0.3 * l)
        q_mla, k_mla, v_mla, q_diff, k_diff, v_diff, pool_in, gates, kn_mla, kn_diff, xs = _inproj(
            xs, mod, gains, w, tables, ffa, l=l, mod_row=mod_row, ctx=ctx if l == 0 else None)
        mla = (q_mla, k_mla, v_mla, kn_mla)
        dif = (q_diff, k_diff, v_diff, kn_diff)
        full = dict(nsub=nsub, q_block0=0, k_rows=nt, k_block0=0, chunks=chunks)
        ctx_only = dict(tq=TM, nsub=1, steps=1, k_rows=TM, k_block0=x_tiles, chunks=((0, TM),))
        oa = _attention(*mla, tq=TQ_MLA, steps=seq // (nsub * TQ_MLA), diff=False, **full)
        oa = _attention(*mla, q_block0=seq // TQ_MLA, diff=False, prev_out=oa, **ctx_only)
        od = _attention(*dif, tq=TQ_DIFF, steps=seq // (nsub * TQ_DIFF), diff=True,
                        extra=extra, layer=l, lam_init=lam_init, **full)
        od = _attention(*dif, q_block0=seq // TQ_DIFF, diff=True, extra=extra, layer=l,
                        lam_init=lam_init, prev_out=od, **ctx_only)
        last = l + 1 == depth
        xs = _mix(xs, mod, oa, od, pool_in, gates, w, gains, ffb, l=l, mod_row=mod_row, x_tiles=x_tiles,
                  final_g=final_g[None] if last else None)
    return xs[1]
```

```python
import functools
import math

import numpy as np
import jax
import jax.numpy as jnp
from jax import lax
from jax.experimental import pallas as pl
from jax.experimental.pallas import tpu as pltpu

F32 = jnp.float32
BF16 = jnp.bfloat16

EPS = 1e-6
ROPE_THETA = 10000.0
GRID_W = 64
N_ADA = 9

HEADS = 8
MLA_NOPE = 64
MLA_ROPE = 32
MLA_V = 64
V_ROWS = MLA_V + 16
MLA_QK = MLA_NOPE + MLA_ROPE
LOG2E = math.log2(math.e)
MLA_SCALE = MLA_QK ** -0.5 * LOG2E
DIFF_HD = 32
DIFF_V = 64
DIFF_SCALE = DIFF_HD ** -0.5 * LOG2E
ROPE_HALF = 16
POOL_WINDOWS = (2, 4, 8, 16)
POOL_G = 128
POOL_HALO = 16

LANES = 128
MOD_ROWS = 8
MOD_COL_BLOCKS = 8
TM = 256
TK = 512
CHUNK_HEAD = (256,)
CHUNK_TAIL = ()
ATT_AHEAD = 2
ATT_ROWS = 256
ATT_BOUND_MAX = 40.0
ATT_NSUB = 8
TQ_MLA = 1024
TQ_DIFF = 512
VMEM_LIMIT = 52 * 1024 * 1024

ATT_FLAGS = {}

NT_DIMS = (((1,), (1,)), ((), ()))
TN_DIMS = (((0,), (0,)), ((), ()))


def _params():
    return pltpu.CompilerParams(vmem_limit_bytes=VMEM_LIMIT)


def _const_spec(shape):
    zeros = (0,) * len(shape)
    return pl.BlockSpec(shape, lambda *_: zeros, pipeline_mode=pl.Buffered(1))


def _layer_spec(a, *lead):
    tail = a.shape[len(lead):]
    zeros = (0,) * len(tail)
    return pl.BlockSpec((None,) * len(lead) + tail, lambda *_: tuple(lead) + zeros, pipeline_mode=pl.Buffered(1))


def _mod_spec(mod, l, mod_row):
    return pl.BlockSpec((None, 1) + mod.shape[2:], lambda i, t: (l, mod_row(i, t), 0, 0))


def _dot(a, b):
    return jnp.dot(a, b, preferred_element_type=F32)


def _rms(x):
    return x * lax.rsqrt(jnp.mean(x * x, axis=-1, keepdims=True) + EPS)


def _mod_kernel(c_ref, w_ref, b_ref, o_ref):
    c = c_ref[...]
    a = c * jax.nn.sigmoid(c)
    o_ref[0] = jnp.dot(a, w_ref[0], preferred_element_type=F32,
                       precision=lax.Precision.HIGHEST) + b_ref[0]


def _modulation(c_rows, ada_w, ada_b):
    depth, d, n = ada_w.shape
    bn = n // MOD_COL_BLOCKS
    return pl.pallas_call(
        _mod_kernel,
        grid=(depth, MOD_COL_BLOCKS),
        in_specs=[pl.BlockSpec((MOD_ROWS, d), lambda l, j: (0, 0)),
                  pl.BlockSpec((1, d, bn), lambda l, j: (l, 0, j)),
                  pl.BlockSpec((1, 1, bn), lambda l, j: (l, 0, j))],
        out_specs=pl.BlockSpec((1, MOD_ROWS, bn), lambda l, j: (l, 0, j)),
        out_shape=jax.ShapeDtypeStruct((depth, MOD_ROWS, n), F32),
        compiler_params=_params(),
        name="modulation",
    )(c_rows, ada_w, ada_b.reshape(depth, 1, n))


def _swiglu_half_step(x, m, g_ref, wg_ref, wu_ref, wd_ref, mi, f_chunks):
    shift, scale, gate = m[3 * mi:3 * mi + 1], m[3 * mi + 1:3 * mi + 2], m[3 * mi + 2:3 * mi + 3]
    u = (_rms(x) * g_ref[...]) * (1.0 + scale) + shift
    ub = u.astype(BF16)
    y = jnp.zeros(x.shape, F32)
    for lo, hi in f_chunks:
        a = _dot(ub, wg_ref[:, lo:hi])
        b = _dot(ub, wu_ref[:, lo:hi])
        hid = (a * jax.nn.sigmoid(a) * b).astype(BF16)
        y = y + _dot(hid, wd_ref[lo:hi, :])
    return x + (0.5 * gate) * y


def _inproj_kernel(x_ref, mod_ref, g_ref, wcq, wckv, wkr2, wdqT, wdk, wdvT, wpool, wgates,
                   qng, kvng, wuqT, wkpad, wvT, place, perm,
                   cosT, sinT, cosN, sinN, c128, s128,
                   g0_ref, fg_ref_w, fu_ref_w, fd_ref_w, *rest, ctx_tile, f_chunks):
    (oq_mla, ok_mla, ov_mla, oq_diff, ok_diff, ov_diff, opool, ogates, okn_mla, okn_diff, oxs) = rest[-11:]
    x = x_ref[0]
    if ctx_tile is not None:
        x = jnp.where(pl.program_id(1) == ctx_tile, rest[0][0], x)
    m = mod_ref[0]
    x = _swiglu_half_step(x, m, g0_ref, fg_ref_w, fu_ref_w, fd_ref_w, 0, f_chunks)
    oxs[0] = x
    u = (_rms(x) * g_ref[...]) * (1.0 + m[4:5]) + m[3:4]
    ub = u.astype(BF16)
    tm = x.shape[0]
    cT, sT = cosT[...], sinT[...]
    hw = ROPE_HALF

    cqn = (_rms(_dot(ub, wcq[...])) * qng[...]).astype(BF16)
    qT = lax.dot_general(wuqT[...], cqn, NT_DIMS, preferred_element_type=F32) * MLA_SCALE
    nope_w = HEADS * MLA_NOPE
    x1, x2 = qT[nope_w:nope_w + LANES], qT[nope_w + LANES:nope_w + 2 * LANES]
    qn = qT[0:nope_w].astype(BF16)
    r1 = (x1 * cT - x2 * sT).astype(BF16)
    r2 = (x1 * sT + x2 * cT).astype(BF16)
    zpad = jnp.zeros((LANES - MLA_QK, tm), BF16)
    for h in range(HEADS):
        oq_mla[0, h, 0, 0:MLA_NOPE, :] = qn[h * MLA_NOPE:(h + 1) * MLA_NOPE]
        oq_mla[0, h, 0, MLA_NOPE:MLA_NOPE + hw, :] = r1[h * hw:(h + 1) * hw]
        oq_mla[0, h, 0, MLA_NOPE + hw:MLA_QK, :] = r2[h * hw:(h + 1) * hw]
        oq_mla[0, h, 0, MLA_QK:LANES, :] = zpad

    ckvn = (_rms(_dot(ub, wckv[...])) * kvng[...]).astype(BF16)
    kr2 = _dot(ub, wkr2[...])
    krr = (kr2[:, 0:LANES] * c128[...] + kr2[:, LANES:2 * LANES] * s128[...]).astype(BF16)
    kall = (_dot(ckvn, wkpad[...]) + _dot(krr, place[...])).astype(BF16)
    vT = lax.dot_general(wvT[...], ckvn, NT_DIMS, preferred_element_type=F32).astype(BF16)
    head_row = lax.broadcasted_iota(jnp.int32, (HEADS, LANES), 0)
    left_half = lax.broadcasted_iota(jnp.int32, (tm, LANES), 1) < 2 * DIFF_HD

    def max_sq_norm(sq):
        return jnp.max(jnp.sum(sq, axis=1, keepdims=True), axis=0, keepdims=True)

    kn = jnp.zeros((HEADS, LANES), F32)
    for h in range(HEADS):
        kh = kall[:, h * LANES:(h + 1) * LANES]
        ok_mla[0, h] = kh
        khf = kh.astype(F32)
        kn = jnp.where(head_row == h, max_sq_norm(khf * khf), kn)
    okn_mla[0, 0] = kn
    ones_rows = (lax.broadcasted_iota(jnp.int32, (V_ROWS - MLA_V, tm), 0) == 0).astype(BF16)

    def store_values(ov, vals):
        for h in range(HEADS):
            ov[0, h * V_ROWS:h * V_ROWS + MLA_V, :] = vals[h * MLA_V:(h + 1) * MLA_V]
            ov[0, h * V_ROWS + MLA_V:(h + 1) * V_ROWS, :] = ones_rows

    store_values(ov_mla, vT)

    dqT = lax.dot_general(wdqT[...], ub, NT_DIMS, preferred_element_type=F32) * DIFF_SCALE
    a1, a2, b1, b2 = (dqT[i * LANES:(i + 1) * LANES] for i in range(4))
    parts = [(a1 * cT - a2 * sT).astype(BF16), (a1 * sT + a2 * cT).astype(BF16),
             (b1 * cT - b2 * sT).astype(BF16), (b1 * sT + b2 * cT).astype(BF16)]
    zhalf = jnp.zeros((2 * DIFF_HD, tm), BF16)
    for h in range(HEADS):
        base = (h % 2) * 2 * DIFF_HD
        for i, part in enumerate(parts):
            oq_diff[0, h, 0, base + i * hw:base + (i + 1) * hw, :] = part[h * hw:(h + 1) * hw]
        other = 2 * DIFF_HD - base
        oq_diff[0, h, 0, other:other + 2 * DIFF_HD, :] = zhalf

    dk = _dot(ub, wdk[...])
    cN, sN = cosN[...], sinN[...]
    k1a, k1b, k2a, k2b = (dk[:, i * LANES:(i + 1) * LANES] for i in range(4))
    rk = jnp.concatenate([k1a * cN - k1b * sN, k1a * sN + k1b * cN,
                          k2a * cN - k2b * sN, k2a * sN + k2b * cN], axis=1).astype(BF16)
    kd = _dot(rk, perm[...]).astype(BF16)
    kn = jnp.zeros((HEADS, LANES), F32)
    for p in range(HEADS // 2):
        kp = kd[:, p * LANES:(p + 1) * LANES]
        ok_diff[0, p] = kp
        kpf = kp.astype(F32)
        sq = kpf * kpf
        kn = jnp.where(head_row == 2 * p, max_sq_norm(jnp.where(left_half, sq, 0.0)), kn)
        kn = jnp.where(head_row == 2 * p + 1, max_sq_norm(jnp.where(left_half, 0.0, sq)), kn)
    okn_diff[0, 0] = kn
    store_values(ov_diff, lax.dot_general(wdvT[...], ub, NT_DIMS, preferred_element_type=F32).astype(BF16))

    opool[0] = _dot(ub, wpool[...])
    ogates[0] = _dot(ub, wgates[...]).astype(BF16)


def _inproj(xs, mod, gains, w, tables, ffa, *, l, mod_row, ctx=None):
    b, nt, d = xs.shape
    f = ffa[0].shape[2]
    f_chunks = tuple((lo, min(lo + 1024, f)) for lo in range(0, f, 1024))
    tile = pl.BlockSpec((1, TM, d), lambda i, t: (i, t, 0))
    x_spec, extra_in, extra_args, aliases, ctx_tile = tile, [], [], {0: 10}, None
    if ctx is not None:
        ctx_tile = nt // TM
        nt += ctx.shape[1]
        x_spec = pl.BlockSpec((1, TM, d), lambda i, t: (i, jnp.minimum(t, ctx_tile - 1), 0))
        extra_in, extra_args, aliases = [pl.BlockSpec((1, TM, d), lambda i, t: (i, 0, 0))], [ctx], {}
    tabT = pl.BlockSpec((LANES, TM), lambda i, t: (0, t))
    tabN = pl.BlockSpec((TM, LANES), lambda i, t: (t, 0))
    stacked = [w["wcq"], w["wckv"], w["wkr2"], w["wdqT"], w["wdk"], w["wdvT"], w["wpool"], w["wgates"],
               w["qng"], w["kvng"], w["wuqT"], w["wkpad"], w["wvT"]]
    shared = [w["place"], w["perm"]]
    vT_spec = pl.BlockSpec((1, HEADS * V_ROWS, TM), lambda i, t: (i, 0, t))

    def q_spec(tq):
        r = tq // TM
        return pl.BlockSpec((1, HEADS, 1, LANES, TM), lambda i, t: (i, 0, t // r, 0, t % r))

    out_shapes = [
        jax.ShapeDtypeStruct((b, HEADS, pl.cdiv(nt, TQ_MLA), LANES, TQ_MLA), BF16),
        jax.ShapeDtypeStruct((b, HEADS, nt, LANES), BF16),
        jax.ShapeDtypeStruct((b, HEADS * V_ROWS, nt), BF16),
        jax.ShapeDtypeStruct((b, HEADS, pl.cdiv(nt, TQ_DIFF), LANES, TQ_DIFF), BF16),
        jax.ShapeDtypeStruct((b, HEADS // 2, nt, LANES), BF16),
        jax.ShapeDtypeStruct((b, HEADS * V_ROWS, nt), BF16),
        jax.ShapeDtypeStruct((b, nt, 4 * POOL_G), F32),
        jax.ShapeDtypeStruct((b, nt, 3 * d), BF16),
        jax.ShapeDtypeStruct((b, nt // TM, HEADS, LANES), F32),
        jax.ShapeDtypeStruct((b, nt // TM, HEADS, LANES), F32),
    ]
    kn_spec = pl.BlockSpec((1, 1, HEADS, LANES), lambda i, t: (i, t, 0, 0))
    out_specs = [
        q_spec(TQ_MLA),
        pl.BlockSpec((1, HEADS, TM, LANES), lambda i, t: (i, 0, t, 0)),
        vT_spec,
        q_spec(TQ_DIFF),
        pl.BlockSpec((1, HEADS // 2, TM, LANES), lambda i, t: (i, 0, t, 0)),
        vT_spec,
        pl.BlockSpec((1, TM, 4 * POOL_G), lambda i, t: (i, t, 0)),
        pl.BlockSpec((1, TM, 3 * d), lambda i, t: (i, t, 0)),
        kn_spec, kn_spec,
    ]
    return pl.pallas_call(
        functools.partial(_inproj_kernel, ctx_tile=ctx_tile, f_chunks=f_chunks),
        grid=(b, nt // TM),
        in_specs=([x_spec, _mod_spec(mod, l, mod_row), _layer_spec(gains, l, 1)]
                  + [_layer_spec(a, l) for a in stacked] + [_const_spec(a.shape) for a in shared]
                  + [tabT, tabT, tabN, tabN, tabN, tabN]
                  + [_layer_spec(gains, l, 0)] + [_layer_spec(a, l) for a in ffa] + extra_in),
        out_specs=out_specs + [tile],
        out_shape=out_shapes + [jax.ShapeDtypeStruct((b, nt, d), F32)],
        input_output_aliases=aliases,
        compiler_params=_params(),
        name="ffn0_inproj",
    )(xs, mod, gains, *stacked, *shared, tables["cosT"], tables["sinT"], tables["cosN"], tables["sinN"],
      tables["c128"], tables["s128"], gains, *ffa, *extra_args)


def _attn_kernel(*refs, chunks, diff, lam_init, aliased):
    refs = list(refs)
    q_ref, k_ref, v_ref, kn_ref = refs[:4]
    n_buf = ATT_AHEAD + 1
    s_bufs = refs[-n_buf:]
    o_ref = refs[-n_buf - 1]
    assert len(refs) == 5 + 2 * diff + aliased + n_buf
    nsub, tq = q_ref.shape[2], q_ref.shape[4]
    nc = len(chunks)
    head = pl.program_id(1)

    tile_max = jnp.max(kn_ref[0], axis=0)
    rows = lax.broadcasted_iota(jnp.int32, tile_max.shape, 0)
    kmax2 = jnp.max(jnp.where(rows == head, tile_max, 0.0), axis=0, keepdims=True)[:, 0:1]

    def finish(num, den, sub):
        o = num / den
        if diff:
            dl_ref, sg_ref = refs[4:6]
            dl = dl_ref[...]
            lam = (jnp.exp(jnp.sum(dl[0:1] * dl[1:2], axis=1, keepdims=True))
                   - jnp.exp(jnp.sum(dl[2:3] * dl[3:4], axis=1, keepdims=True)) + lam_init)
            o = o[:, :tq] - lam * o[:, tq:]
            o = o * lax.rsqrt(jnp.mean(o * o, axis=0, keepdims=True) + EPS)
            o = o * sg_ref[...] * (1.0 - lam_init)
        o_ref[0, sub] = o.astype(o_ref.dtype)

    def bounded_shift(rhs, bound, sub):
        acc = jnp.zeros((MLA_V, rhs.shape[1]), F32)
        den = jnp.zeros((1, rhs.shape[1]), F32)
        pieces = lambda c: [(lo, min(lo + ATT_ROWS, chunks[c][1])) for lo in range(*chunks[c], ATT_ROWS)]
        logits = lambda lo, hi: _dot(k_ref[0, 0, lo:hi, :], rhs)
        s_next = [logits(lo, hi) for lo, hi in pieces(0)]
        for c in range(nc):
            s_cur, s_next = s_next, []
            nxt = pieces(c + 1) if c + 1 < nc else []
            for i, (lo, hi) in enumerate(pieces(c)):
                if i < len(nxt):
                    s_next.append(logits(*nxt[i]))
                p = jnp.exp2(s_cur[i] - bound)
                den = den + jnp.sum(p, axis=0, keepdims=True)
                acc = acc + _dot(v_ref[0, 0:MLA_V, lo:hi], p.astype(BF16))
            s_next += [logits(lo, hi) for lo, hi in nxt[len(pieces(c)):]]
        finish(acc, den, sub)

    def running_max(rhs, sub, zero_row):
        n = rhs.shape[1]

        def scores(c):
            lo, hi = chunks[c]
            s = _dot(k_ref[0, 0, lo:hi, :], rhs)
            s_bufs[c % n_buf][pl.ds(zero_row, hi - lo), :] = s
            return jnp.max(s, axis=0, keepdims=True)

        m = jnp.full((1, n), -1e30, F32)
        acc = jnp.zeros((V_ROWS, n), F32)
        cmaxes = {c: scores(c) for c in range(min(ATT_AHEAD, nc))}
        for c in range(nc):
            lo, hi = chunks[c]
            m_new = jnp.maximum(m, cmaxes.pop(c))
            alpha = jnp.exp2(m - m_new)
            if c + ATT_AHEAD < nc:
                cmaxes[c + ATT_AHEAD] = scores(c + ATT_AHEAD)
            p = jnp.exp2(s_bufs[c % n_buf][pl.ds(zero_row, hi - lo), :] - m_new)
            m, acc = m_new, alpha * acc + _dot(v_ref[0, :, lo:hi], p.astype(BF16))
        finish(acc[0:MLA_V], acc[MLA_V:MLA_V + 1], sub)

    def queries(sub):
        q = q_ref[0, 0, sub]
        if diff:
            first = (lax.broadcasted_iota(jnp.int32, q.shape, 0) & DIFF_HD) == 0
            zero = jnp.zeros_like(q)
            rhs = jnp.concatenate([jnp.where(first, q, zero), jnp.where(first, zero, q)], axis=1)
        else:
            rhs = q
        qf = rhs.astype(F32)
        return rhs, jnp.sqrt(jnp.sum(qf * qf, axis=0, keepdims=True) * kmax2)

    def for_each_block(fn):
        if nsub == 1:
            fn(0)
        else:
            lax.fori_loop(0, nsub, lambda sub, carry: (fn(sub), carry)[1], 0)

    qa = q_ref[0, 0].astype(F32)
    q_norm2 = jnp.max(jnp.sum(qa * qa, axis=1, keepdims=True))
    small = q_norm2 * jnp.max(kmax2) <= ATT_BOUND_MAX * ATT_BOUND_MAX

    @pl.when(small)
    def _():
        for_each_block(lambda sub: bounded_shift(*queries(sub), sub))

    @pl.when(jnp.logical_not(small))
    def _():
        zero_row = pl.multiple_of(jnp.minimum(pl.program_id(2), 0), 16)
        for_each_block(lambda sub: running_max(queries(sub)[0], sub, zero_row))


def _key_chunks(nk):
    head = [c for c in CHUNK_HEAD if sum(CHUNK_HEAD) + sum(CHUNK_TAIL) + TK <= nk]
    tail = CHUNK_TAIL if head else ()
    sizes = list(head)
    body = nk - sum(head) - sum(tail)
    sizes += [TK] * (body // TK) + ([body % TK] if body % TK else [])
    sizes += list(tail)
    edges = np.cumsum([0] + sizes)
    assert edges[-1] == nk and all(s % LANES == 0 for s in sizes)
    return tuple((int(a), int(b)) for a, b in zip(edges[:-1], edges[1:]))


def _attention(qT, k, vT, kn, *, tq, nsub, steps, q_block0, k_rows, k_block0, chunks,
               diff, extra=(), layer=0, lam_init=0.0, prev_out=None):
    b, heads, nqb, _, tqw = qT.shape
    dv = MLA_V
    kdiv = 2 if diff else 1
    in_specs = [
        pl.BlockSpec((1, 1, nsub, LANES, tq), lambda i, h, j: (i, h, q_block0 + j, 0, 0)),
        pl.BlockSpec((1, 1, k_rows, LANES), lambda i, h, j: (i, h // kdiv, k_block0, 0)),
        pl.BlockSpec((1, V_ROWS, k_rows), lambda i, h, j: (i, h, k_block0)),
        pl.BlockSpec((1,) + kn.shape[1:], lambda i, h, j: (i, 0, 0, 0)),
    ]
    args = [qT, k, vT, kn]
    if diff:
        in_specs += [_layer_spec(a, layer) for a in extra]
        args += list(extra)
    aliases = {}
    if prev_out is not None:
        in_specs.append(pl.BlockSpec(memory_space=pl.ANY))
        args.append(prev_out)
        aliases = {len(args) - 1: 0}
    return pl.pallas_call(
        functools.partial(_attn_kernel, chunks=chunks, diff=diff, lam_init=lam_init,
                          aliased=prev_out is not None),
        grid=(b, heads, steps),
        in_specs=in_specs,
        out_specs=pl.BlockSpec((1, nsub, dv, tq), lambda i, h, j: (i, q_block0 + j, h, 0)),
        out_shape=jax.ShapeDtypeStruct((b, nqb, heads * dv, tqw), BF16),
        scratch_shapes=[pltpu.VMEM((max(hi - lo for lo, hi in chunks), (2 * tq if diff else tq)), dt)
                        for dt in [F32] * (ATT_AHEAD + 1)],
        input_output_aliases=aliases,
        compiler_params=pltpu.CompilerParams(vmem_limit_bytes=VMEM_LIMIT, flags=ATT_FLAGS),
        name=("diff" if diff else "mla") + ("_ctx" if prev_out is not None else "_x"),
    )(*args)


def _mix_kernel(x_ref, mod_ref, oa_ref, od_ref, pc_ref, pp_ref, pn_ref, gt_ref, bg_ref,
                pproj_ref, pb_ref, ps_ref, wa_ref, wd_ref, wp_ref, wo_ref,
                g2_ref, fg_ref_w, fu_ref_w, fd_ref_w, *rest, x_tiles, f_chunks, final):
    o_ref = rest[-2] if final else rest[-1]
    t = pl.program_id(1)
    x = x_ref[0]
    gate = mod_ref[0][5:6]
    tm, d = x.shape
    a = lax.dot_general(oa_ref[0, 0], wa_ref[...], TN_DIMS, preferred_element_type=F32)
    dd = lax.dot_general(od_ref[0, 0], wd_ref[...], TN_DIMS, preferred_element_type=F32)

    is_ctx = t == x_tiles
    has_prev = jnp.logical_and(t != 0, jnp.logical_not(is_ctx))
    has_next = jnp.logical_and(t != x_tiles - 1, jnp.logical_not(is_ctx))
    cur = pc_ref[0]
    prev = jnp.where(has_prev, pp_ref[0], 0.0)
    nxt = jnp.where(has_next, pn_ref[0], 0.0)
    ext = jnp.concatenate([prev, cur, nxt], axis=0)
    rows = ext.shape[0]
    seq_len = jnp.where(is_ctx, tm, x_tiles * tm)
    pos = jnp.where(is_ctx, 0, t * tm) + lax.broadcasted_iota(jnp.int32, (tm, 1), 0)
    outs = []
    for g, w in enumerate(POOL_WINDOWS):
        lanes = slice(g * POOL_G, (g + 1) * POOL_G)
        run = ext[:, lanes]
        span = 1
        while span < w:
            run = run + pltpu.roll(run, rows - span, axis=0)
            span *= 2
        win = pltpu.roll(run, rows - (POOL_HALO - w // 2), axis=0)[0:tm]
        cnt = (jnp.minimum(pos + w // 2, seq_len) - jnp.maximum(pos - w // 2, 0)).astype(F32)
        pooled = (win / cnt - cur[:, lanes]).astype(BF16)
        outs.append((_dot(pooled, pproj_ref[g]) + pb_ref[:, lanes]) * ps_ref[:, lanes])
    pooled_out = _dot(jnp.concatenate(outs, axis=1).astype(BF16), wp_ref[...])

    gs = jax.nn.sigmoid(gt_ref[0].astype(F32) + bg_ref[...])
    merged = gs[:, 0:d] * a + gs[:, d:2 * d] * dd + gs[:, 2 * d:3 * d] * pooled_out
    mixed = x + gate * _dot(merged.astype(BF16), wo_ref[...])
    new = _swiglu_half_step(mixed, mod_ref[0], g2_ref, fg_ref_w, fu_ref_w, fd_ref_w, 2, f_chunks)
    o_ref[0] = new
    if final:
        final_g_ref, final_ref = rest[0], rest[-1]

        @pl.when(t < x_tiles)
        def _():
            final_ref[0] = _rms(new) * final_g_ref[...]


def _mix(xs, mod, oa, od, pool_in, gates, w, gains, ffb, *, l, mod_row, x_tiles, final_g=None):
    b, nt, d = xs.shape
    f = ffb[0].shape[2]
    f_chunks = tuple((lo, min(lo + 1024, f)) for lo in range(0, f, 1024))
    hb = TM // POOL_HALO
    n_halo = nt // POOL_HALO
    tile = pl.BlockSpec((1, TM, d), lambda i, t: (i, t, 0))
    pw = 4 * POOL_G
    weights = [w["bgate"], w["pproj"], w["pb"], w["ps"], w["wa"], w["wd"], w["wp"], w["wo"]]

    def o_spec(o):
        r = o.shape[3] // TM
        return pl.BlockSpec((1, 1, o.shape[2], TM), lambda i, t: (i, t // r, 0, t % r))

    in_specs = ([tile, _mod_spec(mod, l, mod_row),
                 o_spec(oa), o_spec(od),
                 pl.BlockSpec((1, TM, pw), lambda i, t: (i, t, 0)),
                 pl.BlockSpec((1, POOL_HALO, pw), lambda i, t: (i, jnp.maximum(t * hb - 1, 0), 0)),
                 pl.BlockSpec((1, POOL_HALO, pw), lambda i, t: (i, jnp.minimum((t + 1) * hb, n_halo - 1), 0)),
                 pl.BlockSpec((1, TM, 3 * d), lambda i, t: (i, t, 0))]
                + [_layer_spec(a, l) for a in weights]
                + [_layer_spec(gains, l, 2)] + [_layer_spec(a, l) for a in ffb])
    args = [xs, mod, oa, od, pool_in, pool_in, pool_in, gates, *weights, gains, *ffb]
    out_specs, out_shape = tile, jax.ShapeDtypeStruct(xs.shape, F32)
    if final_g is not None:
        in_specs.append(_const_spec(final_g.shape))
        args.append(final_g)
        out_specs = [tile, pl.BlockSpec((1, TM, d), lambda i, t: (i, jnp.minimum(t, x_tiles - 1), 0))]
        out_shape = [out_shape, jax.ShapeDtypeStruct((b, x_tiles * TM, d), F32)]
    return pl.pallas_call(
        functools.partial(_mix_kernel, x_tiles=x_tiles, f_chunks=f_chunks, final=final_g is not None),
        grid=(b, nt // TM),
        in_specs=in_specs,
        out_specs=out_specs,
        out_shape=out_shape,
        input_output_aliases={0: 0},
        compiler_params=_params(),
        name="mix_ffn2",
    )(*args)


def _rope_tables(seq, ctx):
    rows = seq // GRID_W
    row_ids = jnp.repeat(jnp.arange(rows), GRID_W).astype(F32)
    col_ids = jnp.tile(jnp.arange(GRID_W), rows).astype(F32)
    n_freq = ROPE_HALF // 2
    inv_freq = ROPE_THETA ** (-jnp.arange(n_freq, dtype=F32) / n_freq)
    ang = jnp.concatenate([row_ids[:, None] * inv_freq, col_ids[:, None] * inv_freq], axis=-1)
    cos = jnp.concatenate([jnp.cos(ang), jnp.ones((ctx, ROPE_HALF), F32)], axis=0)
    sin = jnp.concatenate([jnp.sin(ang), jnp.zeros((ctx, ROPE_HALF), F32)], axis=0)
    pad = jnp.zeros((seq + ctx, LANES - 2 * ROPE_HALF), F32)
    return {
        "cosT": jnp.tile(cos.T, (HEADS, 1)), "sinT": jnp.tile(sin.T, (HEADS, 1)),
        "cosN": jnp.tile(cos, (1, HEADS)), "sinN": jnp.tile(sin, (1, HEADS)),
        "c128": jnp.concatenate([cos, cos, pad], axis=1),
        "s128": jnp.concatenate([sin, sin, pad], axis=1),
    }


def _placement_matrices():
    hw = ROPE_HALF
    perm = np.zeros((HEADS * 2 * DIFF_HD, HEADS * 2 * DIFF_HD), np.float32)
    for h in range(HEADS):
        for c in range(2):
            for half in range(2):
                for f in range(hw):
                    perm[(c * 2 + half) * LANES + h * hw + f, h * 2 * DIFF_HD + c * DIFF_HD + half * hw + f] = 1.0
    place = np.zeros((LANES, HEADS * LANES), np.float32)
    for h in range(HEADS):
        for f in range(MLA_ROPE):
            place[f, h * LANES + MLA_NOPE + f] = 1.0
    return jnp.asarray(place, BF16), jnp.asarray(perm, BF16)


def _stacked_weights(p):
    w_in = p["w_in"]
    nl, d, _ = w_in.shape
    hw = ROPE_HALF
    q_lora, kv_lora = p["mla_w_uq"].shape[1], p["mla_w_ukv"].shape[1]
    diff_w = HEADS * 2 * DIFF_HD
    widths = (q_lora, kv_lora, MLA_ROPE, diff_w, diff_w, HEADS * DIFF_V, 4 * POOL_G)
    o_cq, o_ckv, o_kr, o_dq, o_dk, o_dv, o_pool, o_gate = (int(v) for v in np.cumsum((0,) + widths))
    bf = lambda a: a.astype(BF16)
    t = lambda a: jnp.swapaxes(a, 1, 2)

    def block_order(cols):
        a = cols.reshape(nl, d, HEADS, 2, 2, hw)
        return a.transpose(0, 1, 3, 4, 2, 5).reshape(nl, d, HEADS * 2 * DIFF_HD)

    wkr = w_in[:, :, o_kr:o_kr + MLA_ROPE]
    wkr_rot = jnp.concatenate([-wkr[:, :, hw:], wkr[:, :, :hw]], axis=2)
    zpad = jnp.zeros((nl, d, LANES - MLA_ROPE), F32)
    wkr2 = jnp.concatenate([wkr, zpad, wkr_rot, zpad], axis=2)

    w_uq = p["mla_w_uq"].reshape(nl, q_lora, HEADS, MLA_QK)
    wuq = jnp.concatenate([w_uq[..., :MLA_NOPE].reshape(nl, q_lora, -1),
                           w_uq[..., MLA_NOPE:MLA_NOPE + hw].reshape(nl, q_lora, -1),
                           w_uq[..., MLA_NOPE + hw:].reshape(nl, q_lora, -1)], axis=2)
    w_ukv = p["mla_w_ukv"].reshape(nl, kv_lora, HEADS, MLA_NOPE + MLA_V)
    is_key = (jnp.arange(MLA_NOPE + MLA_V) < MLA_NOPE).astype(F32)
    wkpad = (w_ukv * is_key).reshape(nl, kv_lora, -1)
    wv = w_ukv[..., MLA_NOPE:].reshape(nl, kv_lora, -1)
    place, perm = _placement_matrices()
    return {
        "wcq": bf(w_in[:, :, o_cq:o_ckv]), "wckv": bf(w_in[:, :, o_ckv:o_kr]), "wkr2": bf(wkr2),
        "wdqT": bf(t(block_order(w_in[:, :, o_dq:o_dk]))), "wdk": bf(block_order(w_in[:, :, o_dk:o_dv])),
        "wdvT": bf(t(w_in[:, :, o_dv:o_pool])), "wpool": bf(w_in[:, :, o_pool:o_gate]), "wgates": bf(w_in[:, :, o_gate:]),
        "qng": p["mla_q_norm_g"][:, None], "kvng": p["mla_kv_norm_g"][:, None],
        "wuqT": bf(t(wuq)), "wkpad": bf(wkpad), "wvT": bf(t(wv)),
        "place": place, "perm": perm,
        "bgate": p["b_gate"].reshape(nl, 1, -1), "pproj": bf(p["pool_proj"]),
        "pb": p["pool_b"].reshape(nl, 1, -1), "ps": p["pool_scale"][:, None],
        "wa": bf(p["w_br_mla"]), "wd": bf(p["w_br_diff"]), "wp": bf(p["w_br_pool"]), "wo": bf(p["w_out"]),
    }


def kernel(x, c, ctx, c_ctx, ada_w, ada_b, norm_g, ffa_w_gate, ffa_w_up, ffa_w_down, ffb_w_gate, ffb_w_up, ffb_w_down, w_in, b_gate, mla_q_norm_g, mla_kv_norm_g, mla_w_uq, mla_w_ukv, diff_lambda, diff_subln_g, pool_proj, pool_b, pool_scale, w_br_mla, w_br_diff, w_br_pool, w_out, final_g):
    b, seq, d = x.shape
    n_ctx = ctx.shape[1]
    depth = ada_w.shape[0]
    nt = seq + n_ctx
    nsub = ATT_NSUB if seq % (ATT_NSUB * TQ_MLA) == 0 else 1
    assert n_ctx == TM and seq % (nsub * TQ_MLA) == 0 and seq % GRID_W == 0 and b + 1 <= MOD_ROWS
    x_tiles = seq // TM
    p = dict(w_in=w_in, b_gate=b_gate, mla_q_norm_g=mla_q_norm_g, mla_kv_norm_g=mla_kv_norm_g,
             mla_w_uq=mla_w_uq, mla_w_ukv=mla_w_ukv, pool_proj=pool_proj, pool_b=pool_b,
             pool_scale=pool_scale, w_br_mla=w_br_mla, w_br_diff=w_br_diff, w_br_pool=w_br_pool, w_out=w_out)

    c_rows = jnp.concatenate([c, c_ctx[None], jnp.zeros((MOD_ROWS - b - 1, d), F32)], axis=0)
    mod = _modulation(c_rows, ada_w, ada_b).reshape(depth, MOD_ROWS, N_ADA, d)
    mod_row = lambda i, t: jnp.where(t == x_tiles, b, i)

    tables = _rope_tables(seq, n_ctx)
    w = _stacked_weights(p)
    gains = norm_g[:, :, None, :]
    bf = lambda a: a.astype(BF16)
    ffa = (bf(ffa_w_gate), bf(ffa_w_up), bf(ffa_w_down))
    ffb = (bf(ffb_w_gate), bf(ffb_w_up), bf(ffb_w_down))
    extra = (diff_lambda, diff_subln_g[:, :, None])
    chunks = _key_chunks(nt)

    xs = x
    for l in range(depth):
        lam_init = 0.8 - 0.6 * math.exp(-0.3 * l)
        q_mla, k_mla, v_mla, q_diff, k_diff, v_diff, pool_in, gates, kn_mla, kn_diff, xs = _inproj(
            xs, mod, gains, w, tables, ffa, l=l, mod_row=mod_row, ctx=ctx if l == 0 else None)
        mla = (q_mla, k_mla, v_mla, kn_mla)
        dif = (q_diff, k_diff, v_diff, kn_diff)
        full = dict(nsub=nsub, q_block0=0, k_rows=nt, k_block0=0, chunks=chunks)
        ctx_only = dict(tq=TM, nsub=1, steps=1, k_rows=TM, k_block0=x_tiles, chunks=((0, TM),))
        oa = _attention(*mla, tq=TQ_MLA, steps=seq // (nsub * TQ_MLA), diff=False, **full)
        oa = _attention(*mla, q_block0=seq // TQ_MLA, diff=False, prev_out=oa, **ctx_only)
        od = _attention(*dif, tq=TQ_DIFF, steps=seq // (nsub * TQ_DIFF), diff=True,
                        extra=extra, layer=l, lam_init=lam_init, **full)
        od = _attention(*dif, q_block0=seq // TQ_DIFF, diff=True, extra=extra, layer=l,
                        lam_init=lam_init, prev_out=od, **ctx_only)
        last = l + 1 == depth
        xs = _mix(xs, mod, oa, od, pool_in, gates, w, gains, ffb, l=l, mod_row=mod_row, x_tiles=x_tiles,
                  final_g=final_g[None] if last else None)
    return xs[1]
```
